```python
import math
import jax, jax.numpy as jnp
from jax import lax
import numpy as np

D_MODEL = 1024
BATCH = 4
SEQ = 4096
DEPTH = 2

CHUNK = 64
MEM_LEN = 256
QBLOCK = 128
EPS = 1e-6

DIFF_HEADS = 4
DIFF_DK = 64
DIFF_DV = 2 * DIFF_DK
DIFF_SCALE = DIFF_DK ** -0.5
MLA_HEADS = 4
MLA_NOPE = 128
MLA_ROPE = 64
MLA_DV = 128
MLA_Q_LORA = 256
MLA_KV_LORA = 128
MLA_SCALE = (MLA_NOPE + MLA_ROPE) ** -0.5
ROPE_BASE = 10000.0
EVEN_SPLITS = (DIFF_HEADS * 2 * DIFF_DK, DIFF_HEADS * 2 * DIFF_DK, DIFF_HEADS * DIFF_DV, MLA_Q_LORA, MLA_KV_LORA, MLA_ROPE)
EVEN_IN = sum(EVEN_SPLITS)
EVEN_MIX = DIFF_HEADS * DIFF_DV + MLA_HEADS * MLA_DV
SB_HEADS = 16
SB_DH = 64
SB_WIDTH = SB_HEADS * SB_DH
SB_SCALE = SB_DH ** -0.5
XA_HEADS = 4
XA_DH = D_MODEL // XA_HEADS
D_FF = 4 * D_MODEL

N_EVEN = (DEPTH + 1) // 2
N_ODD = DEPTH // 2

kernel_name = 'chunk_causal_hybrid_diff_mla_stickbreak'


def _rms(x, g):
    xf = x.astype(jnp.float32)
    y = xf * lax.rsqrt(jnp.mean(xf * xf, axis=-1, keepdims=True) + EPS)
    return (y * g.astype(jnp.float32)).astype(x.dtype)


def _heads(x, h):
    b, s, _ = x.shape
    return x.reshape(b, s, h, -1).transpose(0, 2, 1, 3)


def _merge(x):
    b, h, s, d = x.shape
    return x.transpose(0, 2, 1, 3).reshape(b, s, h * d)


def _rope(x, pos):
    half = x.shape[-1] // 2
    inv = ROPE_BASE ** (-jnp.arange(half, dtype=jnp.float32) / half)
    ang = pos.astype(jnp.float32)[:, None] * inv[None, :]
    cos, sin = jnp.cos(ang), jnp.sin(ang)
    xf = x.astype(jnp.float32)
    x1, x2 = xf[..., :half], xf[..., half:]
    return jnp.concatenate([x1 * cos - x2 * sin, x2 * cos + x1 * sin], axis=-1).astype(x.dtype)


def _chunk_mask(t, s):
    return (s // CHUNK)[None, :] <= (t // CHUNK)[:, None]


def _sweep(block_fn, qs):
    b, _, s, _ = qs[0].shape
    nb = s // QBLOCK
    qs_b = tuple(jnp.moveaxis(q.reshape(b, q.shape[1], nb, QBLOCK, q.shape[-1]), 2, 0) for q in qs)
    out = lax.map(lambda a: block_fn(a[0], *a[1]), (jnp.arange(nb), qs_b))
    out = jnp.moveaxis(out, 0, 2)
    return out.reshape(b, out.shape[1], s, out.shape[-1])


def _even_mixer(h, w_in, lq1, lk1, lq2, lk2, g_sub, g_cq, w_uq, g_ckv, w_ukv, w_out, lambda_init):
    b, s, _ = h.shape
    pos = jnp.arange(s)
    proj = h @ w_in
    cuts = np.cumsum(EVEN_SPLITS)[:-1].tolist()
    aq, ak, av, cq, ckv, kr = jnp.split(proj, cuts, axis=-1)

    aq = aq.reshape(b, s, DIFF_HEADS, 2, DIFF_DK)
    ak = ak.reshape(b, s, DIFF_HEADS, 2, DIFF_DK)
    q1 = aq[:, :, :, 0].transpose(0, 2, 1, 3)
    q2 = aq[:, :, :, 1].transpose(0, 2, 1, 3)
    k1 = ak[:, :, :, 0].transpose(0, 2, 1, 3)
    k2 = ak[:, :, :, 1].transpose(0, 2, 1, 3)
    va = _heads(av, DIFF_HEADS)
    lam = (jnp.exp(jnp.sum(lq1.astype(jnp.float32) * lk1.astype(jnp.float32)))
           - jnp.exp(jnp.sum(lq2.astype(jnp.float32) * lk2.astype(jnp.float32))) + lambda_init)
    slopes = 2.0 ** (-8.0 * jnp.arange(1, DIFF_HEADS + 1, dtype=jnp.float32) / DIFF_HEADS)

    def diff_block(i, q1b, q2b):
        t = i * QBLOCK + jnp.arange(QBLOCK)
        mask = _chunk_mask(t, pos)
        dist = jnp.abs(t[:, None] - pos[None, :]).astype(jnp.float32)
        bias = -slopes[:, None, None] * dist[None]

        def probs(qb, kk):
            sc = jnp.einsum('bhqd,bhkd->bhqk', qb, kk).astype(jnp.float32) * DIFF_SCALE + bias
            return jax.nn.softmax(jnp.where(mask, sc, -jnp.inf), axis=-1)

        p = probs(q1b, k1) - lam * probs(q2b, k2)
        return jnp.einsum('bhqk,bhkd->bhqd', p.astype(va.dtype), va)

    o_a = _sweep(diff_block, (q1, q2))
    o_a = _rms(o_a, g_sub) * (1.0 - lambda_init)

    q = _heads(_rms(cq, g_cq) @ w_uq, MLA_HEADS)
    q_nope, q_rope = q[..., :MLA_NOPE], _rope(q[..., MLA_NOPE:], pos)
    kv = _heads(_rms(ckv, g_ckv) @ w_ukv, MLA_HEADS)
    k_nope, vb = kv[..., :MLA_NOPE], kv[..., MLA_NOPE:]
    k_rope = _rope(kr, pos)

    def mla_block(i, qnb, qrb):
        t = i * QBLOCK + jnp.arange(QBLOCK)
        mask = _chunk_mask(t, pos)
        sc = (jnp.einsum('bhqd,bhkd->bhqk', qnb, k_nope)
              + jnp.einsum('bhqr,bkr->bhqk', qrb, k_rope)).astype(jnp.float32) * MLA_SCALE
        p = jax.nn.softmax(jnp.where(mask, sc, -jnp.inf), axis=-1)
        return jnp.einsum('bhqk,bhkd->bhqd', p.astype(vb.dtype), vb)

    o_b = _sweep(mla_block, (q_nope, q_rope))

    return jnp.concatenate([_merge(o_a), _merge(o_b)], axis=-1) @ w_out


def _odd_mixer(h, w_in, w_out):
    s = h.shape[1]
    pos = jnp.arange(s)
    q, k, v = jnp.split(h @ w_in, 3, axis=-1)
    q, k, v = _heads(q, SB_HEADS), _heads(k, SB_HEADS), _heads(v, SB_HEADS)

    def sb_block(i, qb):
        t = i * QBLOCK + jnp.arange(QBLOCK)
        strict = pos[None, :] < t[:, None]
        z = jnp.einsum('bhqd,bhkd->bhqk', qb, k).astype(jnp.float32) * SB_SCALE
        log_beta = jax.nn.log_sigmoid(z)
        log_keep = jnp.where(strict, jax.nn.log_sigmoid(-z), 0.0)
        between = lax.cumsum(log_keep, axis=3, reverse=True) - log_keep
        a = jnp.where(strict, jnp.exp(log_beta + between), 0.0)
        return jnp.einsum('bhqk,bhkd->bhqd', a.astype(v.dtype), v)

    o = _sweep(sb_block, (q,))
    return _merge(o) @ w_out


def _cross(h, mem_n, wq, wkv, wo):
    q = _heads(h @ wq, XA_HEADS)
    k, v = jnp.split(mem_n @ wkv, 2, axis=-1)
    k, v = _heads(k, XA_HEADS), _heads(v, XA_HEADS)
    sc = jnp.einsum('bhqd,bhkd->bhqk', q, k).astype(jnp.float32) * (XA_DH ** -0.5)
    p = jax.nn.softmax(sc, axis=-1)
    return _merge(jnp.einsum('bhqk,bhkd->bhqd', p.astype(v.dtype), v)) @ wo


def _mlp(h, w1, w2):
    return jnp.square(jax.nn.relu(h @ w1)) @ w2


def setup_inputs(seed: int = 0) -> dict:
    key = jax.random.key(seed)
    ks = iter(jax.random.split(key, 40))

    def nrm(shape, scale):
        return jax.random.normal(next(ks), shape, jnp.float32) * scale

    def gain(shape):
        return 1.0 + nrm(shape, 0.02)

    L, E, O, D = DEPTH, N_EVEN, N_ODD, D_MODEL
    return {
        'x': nrm((BATCH, SEQ, D), 1.0),
        'mem': nrm((BATCH, MEM_LEN, D), 1.0),
        'ev_norm': gain((E, D)),
        'ev_w_in': nrm((E, D, EVEN_IN), D ** -0.5),
        'diff_lq1': nrm((E, DIFF_DK), 0.1),
        'diff_lk1': nrm((E, DIFF_DK), 0.1),
        'diff_lq2': nrm((E, DIFF_DK), 0.1),
        'diff_lk2': nrm((E, DIFF_DK), 0.1),
        'diff_subln': gain((E, DIFF_DV)),
        'mla_g_cq': gain((E, MLA_Q_LORA)),
        'mla_w_uq': nrm((E, MLA_Q_LORA, MLA_HEADS * (MLA_NOPE + MLA_ROPE)), MLA_Q_LORA ** -0.5),
        'mla_g_ckv': gain((E, MLA_KV_LORA)),
        'mla_w_ukv': nrm((E, MLA_KV_LORA, MLA_HEADS * (MLA_NOPE + MLA_DV)), MLA_KV_LORA ** -0.5),
        'ev_w_out': nrm((E, EVEN_MIX, D), EVEN_MIX ** -0.5),
        'od_norm': gain((O, D)),
        'sb_w_in': nrm((O, D, 3 * SB_WIDTH), D ** -0.5),
        'sb_w_out': nrm((O, SB_WIDTH, D), SB_WIDTH ** -0.5),
        'xa_norm': gain((L, D)),
        'xa_mem_norm': gain((L, D)),
        'xa_wq': nrm((L, D, D), D ** -0.5),
        'xa_wkv': nrm((L, D, 2 * D), D ** -0.5),
        'xa_wo': nrm((L, D, D), D ** -0.5),
        'mlp_norm': gain((L, D)),
        'mlp_w1': nrm((L, D, D_FF), D ** -0.5),
        'mlp_w2': nrm((L, D_FF, D), D_FF ** -0.5),
        'final_norm': gain((D,)),
    }


def reference(x, mem, ev_norm, ev_w_in, diff_lq1, diff_lk1, diff_lq2, diff_lk2, diff_subln,
              mla_g_cq, mla_w_uq, mla_g_ckv, mla_w_ukv, ev_w_out, od_norm, sb_w_in, sb_w_out,
              xa_norm, xa_mem_norm, xa_wq, xa_wkv, xa_wo, mlp_norm, mlp_w1, mlp_w2, final_norm):
    h = x
    for i in range(DEPTH):
        j = i // 2
        if i % 2 == 0:
            lambda_init = 0.8 - 0.6 * math.exp(-0.3 * i)
            h = h + _even_mixer(_rms(h, ev_norm[j]), ev_w_in[j], diff_lq1[j], diff_lk1[j], diff_lq2[j],
                                diff_lk2[j], diff_subln[j], mla_g_cq[j], mla_w_uq[j], mla_g_ckv[j],
                                mla_w_ukv[j], ev_w_out[j], lambda_init)
        else:
            h = h + _odd_mixer(_rms(h, od_norm[j]), sb_w_in[j], sb_w_out[j])
        h = h + _cross(_rms(h, xa_norm[i]), _rms(mem, xa_mem_norm[i]), xa_wq[i], xa_wkv[i], xa_wo[i])
        h = h + _mlp(_rms(h, mlp_norm[i]), mlp_w1[i], mlp_w2[i])
    return _rms(h, final_norm)
```

```python
import functools
import math

import jax
import jax.numpy as jnp
from jax import lax
from jax.experimental import pallas as pl
from jax.experimental.pallas import tpu as pltpu

F32 = jnp.float32
BF16 = jnp.bfloat16

EPS = 1e-6
CHUNK_SHIFT = 6
DIFF_HEADS = 4
DIFF_DK = 64
MLA_HEADS = 4
MLA_NOPE = 128
MLA_ROPE = 64
MLA_SCALE = (MLA_NOPE + MLA_ROPE) ** -0.5
ROPE_BASE = 10000.0
SB_HEADS = 16
SB_DH = 64
XA_HEADS = 4
NEG_BIG = -1e30
LANES = 128

VMEM_LIMIT_BYTES = 48 * 1024 * 1024


def _params(n_grid):
    return pltpu.CompilerParams(
        dimension_semantics=("arbitrary",) * n_grid, vmem_limit_bytes=VMEM_LIMIT_BYTES
    )


def _resident(shape):
    zeros = (0,) * len(shape)
    return pl.BlockSpec(shape, lambda *_: zeros, pipeline_mode=pl.Buffered(1))


def _rms(x, g):
    ms = jnp.mean(x * x, axis=-1, keepdims=True)
    return x * lax.rsqrt(ms + EPS) * g


def _dot(a, b):
    return jnp.dot(a, b, preferred_element_type=F32)


def _dot_nt(a, b):
    return lax.dot_general(a, b, (((1,), (1,)), ((), ())), preferred_element_type=F32)


def _norm_matmul_kernel(x_ref, g_ref, w_ref, o_ref):
    xn = _rms(x_ref[...], g_ref[...]).astype(BF16)
    o_ref[...] = _dot(xn, w_ref[...]).astype(o_ref.dtype)


def _norm_matmul(x, g, w, *, tm, name):
    n, d = x.shape
    f = w.shape[1]
    return pl.pallas_call(
        _norm_matmul_kernel,
        grid=(n // tm,),
        in_specs=[
            pl.BlockSpec((tm, d), lambda i: (i, 0)),
            _resident((1, d)),
            _resident((d, f)),
        ],
        out_specs=pl.BlockSpec((tm, f), lambda i: (i, 0)),
        out_shape=jax.ShapeDtypeStruct((n, f), BF16),
        compiler_params=_params(1),
        name=name,
    )(x, g, w)


def _even_proj_kernel(h_ref, g_ref, w_ref, gcq_ref, wq_ref, gckv_ref, wkv_ref, cos_ref, sin_ref,
                      dq_ref, dk_ref, dv_ref, mq_ref, mk_ref, mv_ref):
    tm = h_ref.shape[0]
    xn = _rms(h_ref[...], g_ref[...]).astype(BF16)
    proj = _dot(xn, w_ref[...])
    low = lax.broadcasted_iota(jnp.int32, (tm, LANES), 1) < DIFF_DK
    for h in range(DIFF_HEADS):
        blk = proj[:, h * 128:(h + 1) * 128]
        dq_ref[:, h * 256:h * 256 + 128] = jnp.where(low, blk, 0.0).astype(BF16)
        dq_ref[:, h * 256 + 128:(h + 1) * 256] = jnp.where(low, 0.0, blk).astype(BF16)
    dk_ref[...] = proj[:, 512:1024].astype(BF16)
    dv_ref[...] = proj[:, 1024:1536].astype(BF16)

    cos = cos_ref[...]
    sin = sin_ref[...]
    k_rope = (proj[:, 1920:2048] * cos + proj[:, 2048:2176] * sin).astype(BF16)

    cqn = _rms(proj[:, 1536:1792], gcq_ref[...]).astype(BF16)
    qm = _dot(cqn, wq_ref[...])
    ckvn = _rms(proj[:, 1792:1920], gckv_ref[...]).astype(BF16)
    kv = _dot(ckvn, wkv_ref[...])
    for h in range(MLA_HEADS):
        mq_ref[:, h * 256:h * 256 + 128] = qm[:, h * 128:(h + 1) * 128].astype(BF16)
        q_rope = qm[:, 512 + h * 128:512 + (h + 1) * 128] * cos + qm[:, 1024 + h * 128:1024 + (h + 1) * 128] * sin
        mq_ref[:, h * 256 + 128:(h + 1) * 256] = q_rope.astype(BF16)
        mk_ref[:, h * 256:h * 256 + 128] = kv[:, h * 256:h * 256 + 128].astype(BF16)
        mk_ref[:, h * 256 + 128:(h + 1) * 256] = k_rope
        mv_ref[:, h * 128:(h + 1) * 128] = kv[:, h * 256 + 128:(h + 1) * 256].astype(BF16)


def _rotated(w):
    half = w.shape[1] // 2
    return jnp.concatenate([-w[:, half:], w[:, :half]], axis=1)


def _pad_cols(w, width):
    return jnp.pad(w, ((0, 0), (0, width - w.shape[1])))


def _even_proj(h, g, w_in, g_cq, w_uq, g_ckv, w_ukv, cos_pad, sin_pad, *, seq, tm):
    n, d = h.shape
    kr = w_in[:, 1920:1984]
    w_ext = jnp.concatenate(
        [w_in[:, :512] * (DIFF_DK ** -0.5), w_in[:, 512:1920], _pad_cols(kr, 128), _pad_cols(_rotated(kr), 128)],
        axis=1).astype(BF16)
    per_head = MLA_NOPE + MLA_ROPE
    nope = [w_uq[:, hh * per_head:hh * per_head + MLA_NOPE] for hh in range(MLA_HEADS)]
    rope = [w_uq[:, hh * per_head + MLA_NOPE:(hh + 1) * per_head] for hh in range(MLA_HEADS)]
    wq_ext = jnp.concatenate(
        nope + [_pad_cols(r, 128) for r in rope] + [_pad_cols(_rotated(r), 128) for r in rope], axis=1).astype(BF16)
    pos_blocks = seq // tm
    bf = lambda width: jax.ShapeDtypeStruct((n, width), BF16)
    row = lambda width: pl.BlockSpec((tm, width), lambda i: (i, 0))
    return pl.pallas_call(
        _even_proj_kernel,
        grid=(n // tm,),
        in_specs=[
            row(d),
            _resident((1, d)),
            _resident(w_ext.shape),
            _resident((1, g_cq.shape[-1])),
            _resident(wq_ext.shape),
            _resident((1, g_ckv.shape[-1])),
            _resident(w_ukv.shape),
            pl.BlockSpec((tm, 128), lambda i: (i % pos_blocks, 0)),
            pl.BlockSpec((tm, 128), lambda i: (i % pos_blocks, 0)),
        ],
        out_specs=[row(1024), row(512), row(512), row(1024), row(1024), row(512)],
        out_shape=[bf(1024), bf(512), bf(512), bf(1024), bf(1024), bf(512)],
        compiler_params=_params(1),
        name="even_proj",
    )(h, g.reshape(1, -1), w_ext, g_cq.reshape(1, -1), wq_ext, g_ckv.reshape(1, -1), w_ukv.astype(BF16),
      cos_pad, sin_pad)


def _softmax_step(s, v, m_ref, l_ref, acc_ref):
    m_prev = m_ref[...]
    m_new = jnp.maximum(m_prev, jnp.max(s, axis=-1, keepdims=True))
    alpha = jnp.exp(m_prev - m_new)
    p = jnp.exp(s - m_new)
    l_ref[...] = alpha * l_ref[...] + jnp.sum(p, axis=-1, keepdims=True)
    acc_ref[...] = alpha * acc_ref[...] + _dot(p.astype(BF16), v)
    m_ref[...] = m_new


def _chunk_mask(t):
    row = lax.broadcasted_iota(jnp.int32, (t, t), 0)
    col = lax.broadcasted_iota(jnp.int32, (t, t), 1)
    return (col >> CHUNK_SHIFT) <= (row >> CHUNK_SHIFT)


def _diff_attn_kernel(q_ref, k_ref, v_ref, slope_ref, lq1_ref, lk1_ref, lq2_ref, lk2_ref, gsub_ref, o_ref,
                      m1, l1, acc1, m2, l2, acc2, *, t, lambda_init):
    i = pl.program_id(2)
    for m_ref, l_ref, acc_ref in ((m1, l1, acc1), (m2, l2, acc2)):
        m_ref[...] = jnp.full(m_ref.shape, NEG_BIG, F32)
        l_ref[...] = jnp.zeros(l_ref.shape, F32)
        acc_ref[...] = jnp.zeros(acc_ref.shape, F32)
    neg_slope = -slope_ref[:, :1]
    rel = (lax.broadcasted_iota(jnp.int32, (t, t), 0) - lax.broadcasted_iota(jnp.int32, (t, t), 1)).astype(F32)

    def step(j, mask):
        start = pl.multiple_of(j * t, t)
        k = k_ref[pl.ds(start, t), :]
        v = v_ref[pl.ds(start, t), :]
        bias = neg_slope * jnp.abs(rel + ((i - j) * t).astype(F32))
        for q, m_ref, l_ref, acc_ref in ((q_ref[:, :128], m1, l1, acc1), (q_ref[:, 128:], m2, l2, acc2)):
            s = _dot_nt(q, k) + bias
            if mask is not None:
                s = jnp.where(mask, s, NEG_BIG)
            _softmax_step(s, v, m_ref, l_ref, acc_ref)

    def body(j, carry):
        step(j, None)
        return carry

    lax.fori_loop(0, i, body, 0)
    step(i, _chunk_mask(t))

    lam = (jnp.exp(jnp.sum(lq1_ref[...] * lk1_ref[...], axis=-1, keepdims=True))
           - jnp.exp(jnp.sum(lq2_ref[...] * lk2_ref[...], axis=-1, keepdims=True)) + lambda_init)
    o = acc1[...] / l1[...] - lam * (acc2[...] / l2[...])
    o_ref[...] = (_rms(o, gsub_ref[...]) * (1.0 - lambda_init)).astype(o_ref.dtype)


def _diff_attn(dq, dk, dv, lq1, lk1, lq2, lk2, g_sub, lambda_init, *, t):
    b, s, _ = dq.shape
    slopes = 2.0 ** (-8.0 * jnp.arange(1, DIFF_HEADS + 1, dtype=F32) / DIFF_HEADS)
    slopes = jnp.broadcast_to(slopes[:, None, None], (DIFF_HEADS, 1, LANES))
    vec = lambda a: a.reshape(1, -1)
    small = lambda width: pl.BlockSpec((1, width), lambda bi, h, i: (0, 0))
    kern = functools.partial(_diff_attn_kernel, t=t, lambda_init=lambda_init)
    return pl.pallas_call(
        kern,
        grid=(b, DIFF_HEADS, s // t),
        in_specs=[
            pl.BlockSpec((None, t, 256), lambda bi, h, i: (bi, i, h)),
            pl.BlockSpec((None, s, 128), lambda bi, h, i: (bi, 0, h)),
            pl.BlockSpec((None, s, 128), lambda bi, h, i: (bi, 0, h)),
            pl.BlockSpec((None, 1, LANES), lambda bi, h, i: (h, 0, 0)),
            small(DIFF_DK), small(DIFF_DK), small(DIFF_DK), small(DIFF_DK),
            small(2 * DIFF_DK),
        ],
        out_specs=pl.BlockSpec((None, t, 128), lambda bi, h, i: (bi, i, h)),
        out_shape=jax.ShapeDtypeStruct((b, s, DIFF_HEADS * 128), BF16),
        scratch_shapes=[pltpu.VMEM((t, 1), F32), pltpu.VMEM((t, 1), F32), pltpu.VMEM((t, 128), F32)] * 2,
        compiler_params=_params(3),
        name="diff_attn",
    )(dq, dk, dv, slopes, vec(lq1), vec(lk1), vec(lq2), vec(lk2), vec(g_sub))


def _mla_attn_kernel(q_ref, k_ref, v_ref, o_ref, m_ref, l_ref, acc_ref, *, t):
    i = pl.program_id(2)
    m_ref[...] = jnp.full(m_ref.shape, NEG_BIG, F32)
    l_ref[...] = jnp.zeros(l_ref.shape, F32)
    acc_ref[...] = jnp.zeros(acc_ref.shape, F32)
    q = q_ref[...]

    def step(j, mask):
        start = pl.multiple_of(j * t, t)
        s = _dot_nt(q, k_ref[pl.ds(start, t), :]) * MLA_SCALE
        if mask is not None:
            s = jnp.where(mask, s, NEG_BIG)
        _softmax_step(s, v_ref[pl.ds(start, t), :], m_ref, l_ref, acc_ref)

    def body(j, carry):
        step(j, None)
        return carry

    lax.fori_loop(0, i, body, 0)
    step(i, _chunk_mask(t))
    o_ref[...] = (acc_ref[...] / l_ref[...]).astype(o_ref.dtype)


def _mla_attn(mq, mk, mv, *, t):
    b, s, _ = mq.shape
    return pl.pallas_call(
        functools.partial(_mla_attn_kernel, t=t),
        grid=(b, MLA_HEADS, s // t),
        in_specs=[
            pl.BlockSpec((None, t, 256), lambda bi, h, i: (bi, i, h)),
            pl.BlockSpec((None, s, 256), lambda bi, h, i: (bi, 0, h)),
            pl.BlockSpec((None, s, 128), lambda bi, h, i: (bi, 0, h)),
        ],
        out_specs=pl.BlockSpec((None, t, 128), lambda bi, h, i: (bi, i, h)),
        out_shape=jax.ShapeDtypeStruct((b, s, MLA_HEADS * 128), BF16),
        scratch_shapes=[pltpu.VMEM((t, 1), F32), pltpu.VMEM((t, 1), F32), pltpu.VMEM((t, 128), F32)],
        compiler_params=_params(3),
        name="mla_attn",
    )(mq, mk, mv)


def _sb_attn_kernel(q_ref, k_ref, v_ref, o_ref, carry_a, carry_b, acc_a, acc_b, *, tq, tk):
    i = pl.program_id(2)
    low = lax.broadcasted_iota(jnp.int32, (tq, LANES), 1) < SB_DH
    q = q_ref[...]
    zero = jnp.zeros_like(q)
    q_a = jnp.where(low, q, zero)
    q_b = jnp.where(low, zero, q)
    later = (lax.broadcasted_iota(jnp.int32, (tk, tk), 0) > lax.broadcasted_iota(jnp.int32, (tk, tk), 1)).astype(BF16)
    for ref in (carry_a, carry_b, acc_a, acc_b):
        ref[...] = jnp.zeros(ref.shape, F32)
    row = lax.broadcasted_iota(jnp.int32, (tq, tk), 0)
    col = lax.broadcasted_iota(jnp.int32, (tq, tk), 1)

    def step(j, masked):
        start = pl.multiple_of(j * tk, tk)
        k = k_ref[pl.ds(start, tk), :]
        v = v_ref[pl.ds(start, tk), :]
        if masked:
            strict = (col + (j * tk - i * tq)) < row
        for qh, carry_ref, acc_ref in ((q_a, carry_a, acc_a), (q_b, carry_b, acc_b)):
            z = _dot_nt(qh, k)
            log_keep = jnp.minimum(-z, 0.0) - jnp.log1p(jnp.exp(-jnp.abs(z)))
            if masked:
                log_keep = jnp.where(strict, log_keep, 0.0)
            hi = log_keep.astype(BF16)
            lo = (log_keep - hi.astype(F32)).astype(BF16)
            between = _dot(hi, later) + _dot(lo, later) + carry_ref[...]
            a = jnp.exp(z + log_keep + between)
            if masked:
                a = jnp.where(strict, a, 0.0)
            acc_ref[...] += _dot(a.astype(BF16), v)
            carry_ref[...] += jnp.sum(log_keep, axis=-1, keepdims=True)

    n_diag = tq // tk
    last = (i + 1) * n_diag - 1
    for d in range(n_diag):
        step(last - d, True)

    def body(jj, carry):
        step(i * n_diag - 1 - jj, False)
        return carry

    lax.fori_loop(0, i * n_diag, body, 0)
    o_ref[...] = jnp.where(low, acc_a[...], acc_b[...]).astype(o_ref.dtype)


def _sb_attn(qkv, *, tq, tk):
    b, s, _ = qkv.shape
    pairs = SB_HEADS * SB_DH // LANES
    return pl.pallas_call(
        functools.partial(_sb_attn_kernel, tq=tq, tk=tk),
        grid=(b, pairs, s // tq),
        in_specs=[
            pl.BlockSpec((None, tq, LANES), lambda bi, p, i: (bi, i, p)),
            pl.BlockSpec((None, s, LANES), lambda bi, p, i: (bi, 0, pairs + p)),
            pl.BlockSpec((None, s, LANES), lambda bi, p, i: (bi, 0, 2 * pairs + p)),
        ],
        out_specs=pl.BlockSpec((None, tq, LANES), lambda bi, p, i: (bi, i, p)),
        out_shape=jax.ShapeDtypeStruct((b, s, SB_HEADS * SB_DH), BF16),
        scratch_shapes=[pltpu.VMEM((tq, 1), F32), pltpu.VMEM((tq, 1), F32),
                        pltpu.VMEM((tq, LANES), F32), pltpu.VMEM((tq, LANES), F32)],
        compiler_params=_params(3),
        name="sb_attn",
    )(qkv, qkv, qkv)


def _cross_kernel(h_ref, mix_ref, wmix_ref, g_ref, wq_ref, kv_ref, wo_ref, out_ref, o_scr):
    d = h_ref.shape[1]
    dh = d // XA_HEADS
    h1 = h_ref[...] + _dot(mix_ref[...], wmix_ref[...])
    q = _dot(_rms(h1, g_ref[...]).astype(BF16), wq_ref[...]).astype(BF16)
    for hd in range(XA_HEADS):
        s = _dot_nt(q[:, hd * dh:(hd + 1) * dh], kv_ref[:, hd * dh:(hd + 1) * dh])
        p = jnp.exp(s - jnp.max(s, axis=-1, keepdims=True))
        o = _dot(p.astype(BF16), kv_ref[:, d + hd * dh:d + (hd + 1) * dh]) / jnp.sum(p, axis=-1, keepdims=True)
        o_scr[:, hd * dh:(hd + 1) * dh] = o.astype(BF16)
    out_ref[...] = h1 + _dot(o_scr[...], wo_ref[...])


def _cross(h, mix, w_mix, g, wq, mem_kv, wo, *, seq, tm):
    n, d = h.shape
    blocks = seq // tm
    mem_len = mem_kv.shape[1]
    dh = d // XA_HEADS
    return pl.pallas_call(
        _cross_kernel,
        grid=(n // tm,),
        in_specs=[
            pl.BlockSpec((tm, d), lambda i: (i, 0)),
            pl.BlockSpec((tm, mix.shape[1]), lambda i: (i, 0)),
            _resident(w_mix.shape),
            _resident((1, d)),
            _resident(wq.shape),
            pl.BlockSpec((None, mem_len, 2 * d), lambda i: (i // blocks, 0, 0)),
            _resident(wo.shape),
        ],
        out_specs=pl.BlockSpec((tm, d), lambda i: (i, 0)),
        out_shape=jax.ShapeDtypeStruct((n, d), F32),
        scratch_shapes=[pltpu.VMEM((tm, d), BF16)],
        compiler_params=_params(1),
        name="cross_attn",
    )(h, mix, w_mix.astype(BF16), g.reshape(1, -1), (wq * dh ** -0.5).astype(BF16), mem_kv, wo.astype(BF16))


def _mlp_kernel(h_ref, g_ref, w1_ref, w2_ref, gf_ref, out_ref, *, tf, final_norm):
    x = h_ref[...]
    xn = _rms(x, g_ref[...]).astype(BF16)
    acc = x
    for c in range(w1_ref.shape[1] // tf):
        a = jnp.maximum(_dot(xn, w1_ref[:, c * tf:(c + 1) * tf]), 0.0)
        acc = acc + _dot((a * a).astype(BF16), w2_ref[c * tf:(c + 1) * tf, :])
    if final_norm:
        acc = _rms(acc, gf_ref[...])
    out_ref[...] = acc


def _mlp(h, g, w1, w2, g_final, *, tm, tf, final_norm):
    n, d = h.shape
    return pl.pallas_call(
        functools.partial(_mlp_kernel, tf=tf, final_norm=final_norm),
        grid=(n // tm,),
        in_specs=[
            pl.BlockSpec((tm, d), lambda i: (i, 0)),
            _resident((1, d)),
            _resident(w1.shape),
            _resident(w2.shape),
            _resident((1, d)),
        ],
        out_specs=pl.BlockSpec((tm, d), lambda i: (i, 0)),
        out_shape=jax.ShapeDtypeStruct((n, d), F32),
        compiler_params=_params(1),
        name="mlp",
    )(h, g.reshape(1, -1), w1.astype(BF16), w2.astype(BF16), g_final.reshape(1, -1))


def _rope_tables(seq):
    half = MLA_ROPE // 2
    inv = ROPE_BASE ** (-jnp.arange(half, dtype=F32) / half)
    ang = jnp.arange(seq).astype(F32)[:, None] * inv[None, :]
    pad = lambda a: jnp.concatenate([a, a, jnp.zeros((seq, LANES - 2 * half), F32)], axis=1)
    return pad(jnp.cos(ang)), pad(jnp.sin(ang))


def kernel(x, mem, ev_norm, ev_w_in, diff_lq1, diff_lk1, diff_lq2, diff_lk2, diff_subln, mla_g_cq, mla_w_uq, mla_g_ckv, mla_w_ukv, ev_w_out, od_norm, sb_w_in, sb_w_out, xa_norm, xa_mem_norm, xa_wq, xa_wkv, xa_wo, mlp_norm, mlp_w1, mlp_w2, final_norm):
    b, seq, d = x.shape
    mem_len = mem.shape[1]
    depth = xa_norm.shape[0]
    tm = 512
    h = x.reshape(b * seq, d)
    mem2 = mem.reshape(b * mem_len, d)
    cos_pad, sin_pad = _rope_tables(seq)
    for i in range(depth):
        j = i // 2
        if i % 2 == 0:
            lambda_init = 0.8 - 0.6 * math.exp(-0.3 * i)
            dq, dk, dv, mq, mk, mv = _even_proj(
                h, ev_norm[j], ev_w_in[j], mla_g_cq[j], mla_w_uq[j], mla_g_ckv[j], mla_w_ukv[j],
                cos_pad, sin_pad, seq=seq, tm=tm)
            split = lambda a: a.reshape(b, seq, a.shape[-1])
            o_a = _diff_attn(split(dq), split(dk), split(dv), diff_lq1[j], diff_lk1[j], diff_lq2[j], diff_lk2[j],
                             diff_subln[j], lambda_init, t=512)
            o_b = _mla_attn(split(mq), split(mk), split(mv), t=512)
            mix = jnp.concatenate([o_a, o_b], axis=-1).reshape(b * seq, -1)
            w_mix = ev_w_out[j]
        else:
            w_in = sb_w_in[j]
            width = SB_HEADS * SB_DH
            w_in = jnp.concatenate([w_in[:, :width] * (SB_DH ** -0.5), w_in[:, width:]], axis=1).astype(BF16)
            qkv = _norm_matmul(h, od_norm[j].reshape(1, -1), w_in, tm=tm, name="sb_proj")
            mix = _sb_attn(qkv.reshape(b, seq, -1), tq=512, tk=256).reshape(b * seq, -1)
            w_mix = sb_w_out[j]
        mem_kv = _norm_matmul(mem2, xa_mem_norm[i].reshape(1, -1), xa_wkv[i].astype(BF16), tm=mem_len, name="mem_kv")
        h = _cross(h, mix, w_mix, xa_norm[i], xa_wq[i], mem_kv.reshape(b, mem_len, -1), xa_wo[i], seq=seq, tm=tm)
        h = _mlp(h, mlp_norm[i], mlp_w1[i], mlp_w2[i], final_norm, tm=tm, tf=1024, final_norm=(i == depth - 1))
    return h.reshape(b, seq, d)
```

```python
import functools
import math

import jax
import jax.numpy as jnp
from jax import lax
from jax.experimental import pallas as pl
from jax.experimental.pallas import tpu as pltpu

F32 = jnp.float32
BF16 = jnp.bfloat16

EPS = 1e-6
CHUNK_SHIFT = 6
DIFF_HEADS = 4
DIFF_DK = 64
MLA_HEADS = 4
MLA_NOPE = 128
MLA_ROPE = 64
MLA_SCALE = (MLA_NOPE + MLA_ROPE) ** -0.5
ROPE_BASE = 10000.0
SB_HEADS = 16
SB_DH = 64
XA_HEADS = 4
NEG_BIG = -1e30
EXP_UNDERFLOW = -104.0
LANES = 128

VMEM_LIMIT_BYTES = 48 * 1024 * 1024


def _params(n_grid):
    return pltpu.CompilerParams(
        dimension_semantics=("arbitrary",) * n_grid, vmem_limit_bytes=VMEM_LIMIT_BYTES
    )


def _resident(shape):
    zeros = (0,) * len(shape)
    return pl.BlockSpec(shape, lambda *_: zeros, pipeline_mode=pl.Buffered(1))


def _rms(x, g):
    ms = jnp.mean(x * x, axis=-1, keepdims=True)
    return x * lax.rsqrt(ms + EPS) * g


def _dot(a, b):
    return jnp.dot(a, b, preferred_element_type=F32)


def _dot_nt(a, b):
    return lax.dot_general(a, b, (((1,), (1,)), ((), ())), preferred_element_type=F32)


def _norm_matmul_kernel(x_ref, g_ref, w_ref, o_ref):
    xn = _rms(x_ref[...], g_ref[...]).astype(BF16)
    o_ref[...] = _dot(xn, w_ref[...]).astype(o_ref.dtype)


def _norm_matmul(x, g, w, *, tm, name):
    n, d = x.shape
    f = w.shape[1]
    return pl.pallas_call(
        _norm_matmul_kernel,
        grid=(n // tm,),
        in_specs=[
            pl.BlockSpec((tm, d), lambda i: (i, 0)),
            _resident((1, d)),
            _resident((d, f)),
        ],
        out_specs=pl.BlockSpec((tm, f), lambda i: (i, 0)),
        out_shape=jax.ShapeDtypeStruct((n, f), BF16),
        compiler_params=_params(1),
        name=name,
    )(x, g, w)


def _even_proj_kernel(h_ref, g_ref, w_ref, gcq_ref, wq_ref, gckv_ref, wkv_ref, cos_ref, sin_ref,
                      dq_ref, dk_ref, dv_ref, mq_ref, mk_ref, mv_ref):
    tm = h_ref.shape[0]
    xn = _rms(h_ref[...], g_ref[...]).astype(BF16)
    proj = _dot(xn, w_ref[...])
    low = lax.broadcasted_iota(jnp.int32, (tm, LANES), 1) < DIFF_DK
    for h in range(DIFF_HEADS):
        blk = proj[:, h * 128:(h + 1) * 128]
        dq_ref[:, h * 256:h * 256 + 128] = jnp.where(low, blk, 0.0).astype(BF16)
        dq_ref[:, h * 256 + 128:(h + 1) * 256] = jnp.where(low, 0.0, blk).astype(BF16)
    dk_ref[...] = proj[:, 512:1024].astype(BF16)
    dv_ref[...] = proj[:, 1024:1536].astype(BF16)

    cos = cos_ref[...]
    sin = sin_ref[...]
    k_rope = (proj[:, 1920:2048] * cos + proj[:, 2048:2176] * sin).astype(BF16)

    cqn = _rms(proj[:, 1536:1792], gcq_ref[...]).astype(BF16)
    qm = _dot(cqn, wq_ref[...])
    ckvn = _rms(proj[:, 1792:1920], gckv_ref[...]).astype(BF16)
    kv = _dot(ckvn, wkv_ref[...])
    for h in range(MLA_HEADS):
        mq_ref[:, h * 256:h * 256 + 128] = qm[:, h * 128:(h + 1) * 128].astype(BF16)
        q_rope = qm[:, 512 + h * 128:512 + (h + 1) * 128] * cos + qm[:, 1024 + h * 128:1024 + (h + 1) * 128] * sin
        mq_ref[:, h * 256 + 128:(h + 1) * 256] = q_rope.astype(BF16)
        mk_ref[:, h * 256:h * 256 + 128] = kv[:, h * 256:h * 256 + 128].astype(BF16)
        mk_ref[:, h * 256 + 128:(h + 1) * 256] = k_rope
        mv_ref[:, h * 128:(h + 1) * 128] = kv[:, h * 256 + 128:(h + 1) * 256].astype(BF16)


def _rotated(w):
    half = w.shape[1] // 2
    return jnp.concatenate([-w[:, half:], w[:, :half]], axis=1)


def _pad_cols(w, width):
    return jnp.pad(w, ((0, 0), (0, width - w.shape[1])))


def _even_proj(h, g, w_in, g_cq, w_uq, g_ckv, w_ukv, cos_pad, sin_pad, *, seq, tm):
    n, d = h.shape
    kr = w_in[:, 1920:1984]
    w_ext = jnp.concatenate(
        [w_in[:, :512] * (DIFF_DK ** -0.5), w_in[:, 512:1920], _pad_cols(kr, 128), _pad_cols(_rotated(kr), 128)],
        axis=1).astype(BF16)
    per_head = MLA_NOPE + MLA_ROPE
    nope = [w_uq[:, hh * per_head:hh * per_head + MLA_NOPE] for hh in range(MLA_HEADS)]
    rope = [w_uq[:, hh * per_head + MLA_NOPE:(hh + 1) * per_head] for hh in range(MLA_HEADS)]
    wq_ext = jnp.concatenate(
        nope + [_pad_cols(r, 128) for r in rope] + [_pad_cols(_rotated(r), 128) for r in rope], axis=1).astype(BF16)
    pos_blocks = seq // tm
    bf = lambda width: jax.ShapeDtypeStruct((n, width), BF16)
    row = lambda width: pl.BlockSpec((tm, width), lambda i: (i, 0))
    return pl.pallas_call(
        _even_proj_kernel,
        grid=(n // tm,),
        in_specs=[
            row(d),
            _resident((1, d)),
            _resident(w_ext.shape),
            _resident((1, g_cq.shape[-1])),
            _resident(wq_ext.shape),
            _resident((1, g_ckv.shape[-1])),
            _resident(w_ukv.shape),
            pl.BlockSpec((tm, 128), lambda i: (i % pos_blocks, 0)),
            pl.BlockSpec((tm, 128), lambda i: (i % pos_blocks, 0)),
        ],
        out_specs=[row(1024), row(512), row(512), row(1024), row(1024), row(512)],
        out_shape=[bf(1024), bf(512), bf(512), bf(1024), bf(1024), bf(512)],
        compiler_params=_params(1),
        name="even_proj",
    )(h, g.reshape(1, -1), w_ext, g_cq.reshape(1, -1), wq_ext, g_ckv.reshape(1, -1), w_ukv.astype(BF16),
      cos_pad, sin_pad)


def _softmax_step(s, v, m_ref, l_ref, acc_ref):
    m_prev = m_ref[...]
    m_new = jnp.maximum(m_prev, jnp.max(s, axis=-1, keepdims=True))
    alpha = jnp.exp(m_prev - m_new)
    p = jnp.exp(s - m_new)
    l_ref[...] = alpha * l_ref[...] + jnp.sum(p, axis=-1, keepdims=True)
    acc_ref[...] = alpha * acc_ref[...] + _dot(p.astype(BF16), v)
    m_ref[...] = m_new


def _chunk_mask(t):
    row = lax.broadcasted_iota(jnp.int32, (t, t), 0)
    col = lax.broadcasted_iota(jnp.int32, (t, t), 1)
    return (col >> CHUNK_SHIFT) <= (row >> CHUNK_SHIFT)


def _diff_attn_kernel(q_ref, k_ref, v_ref, slope_ref, lq1_ref, lk1_ref, lq2_ref, lk2_ref, gsub_ref, o_ref,
                      m1, l1, acc1, m2, l2, acc2, *, t, lambda_init):
    i = pl.program_id(2)
    for m_ref, l_ref, acc_ref in ((m1, l1, acc1), (m2, l2, acc2)):
        m_ref[...] = jnp.full(m_ref.shape, NEG_BIG, F32)
        l_ref[...] = jnp.zeros(l_ref.shape, F32)
        acc_ref[...] = jnp.zeros(acc_ref.shape, F32)
    neg_slope = -slope_ref[:, :1]
    rel = (lax.broadcasted_iota(jnp.int32, (t, t), 0) - lax.broadcasted_iota(jnp.int32, (t, t), 1)).astype(F32)

    def step(j, mask):
        start = pl.multiple_of(j * t, t)
        k = k_ref[pl.ds(start, t), :]
        v = v_ref[pl.ds(start, t), :]
        bias = neg_slope * jnp.abs(rel + ((i - j) * t).astype(F32))
        for q, m_ref, l_ref, acc_ref in ((q_ref[:, :128], m1, l1, acc1), (q_ref[:, 128:], m2, l2, acc2)):
            s = _dot_nt(q, k) + bias
            if mask is not None:
                s = jnp.where(mask, s, NEG_BIG)
            _softmax_step(s, v, m_ref, l_ref, acc_ref)

    def body(j, carry):
        step(j, None)
        return carry

    lax.fori_loop(0, i, body, 0)
    step(i, _chunk_mask(t))

    lam = (jnp.exp(jnp.sum(lq1_ref[...] * lk1_ref[...], axis=-1, keepdims=True))
           - jnp.exp(jnp.sum(lq2_ref[...] * lk2_ref[...], axis=-1, keepdims=True)) + lambda_init)
    o = acc1[...] / l1[...] - lam * (acc2[...] / l2[...])
    o_ref[...] = (_rms(o, gsub_ref[...]) * (1.0 - lambda_init)).astype(o_ref.dtype)


def _diff_attn(dq, dk, dv, lq1, lk1, lq2, lk2, g_sub, lambda_init, *, t):
    b, s, _ = dq.shape
    slopes = 2.0 ** (-8.0 * jnp.arange(1, DIFF_HEADS + 1, dtype=F32) / DIFF_HEADS)
    slopes = jnp.broadcast_to(slopes[:, None, None], (DIFF_HEADS, 1, LANES))
    vec = lambda a: a.reshape(1, -1)
    small = lambda width: pl.BlockSpec((1, width), lambda bi, h, i: (0, 0))
    kern = functools.partial(_diff_attn_kernel, t=t, lambda_init=lambda_init)
    return pl.pallas_call(
        kern,
        grid=(b, DIFF_HEADS, s // t),
        in_specs=[
            pl.BlockSpec((None, t, 256), lambda bi, h, i: (bi, i, h)),
            pl.BlockSpec((None, s, 128), lambda bi, h, i: (bi, 0, h)),
            pl.BlockSpec((None, s, 128), lambda bi, h, i: (bi, 0, h)),
            pl.BlockSpec((None, 1, LANES), lambda bi, h, i: (h, 0, 0)),
            small(DIFF_DK), small(DIFF_DK), small(DIFF_DK), small(DIFF_DK),
            small(2 * DIFF_DK),
        ],
        out_specs=pl.BlockSpec((None, t, 128), lambda bi, h, i: (bi, i, h)),
        out_shape=jax.ShapeDtypeStruct((b, s, DIFF_HEADS * 128), BF16),
        scratch_shapes=[pltpu.VMEM((t, 1), F32), pltpu.VMEM((t, 1), F32), pltpu.VMEM((t, 128), F32)] * 2,
        compiler_params=_params(3),
        name="diff_attn",
    )(dq, dk, dv, slopes, vec(lq1), vec(lk1), vec(lq2), vec(lk2), vec(g_sub))


def _mla_attn_kernel(q_ref, k_ref, v_ref, o_ref, m_ref, l_ref, acc_ref, *, t):
    i = pl.program_id(2)
    m_ref[...] = jnp.full(m_ref.shape, NEG_BIG, F32)
    l_ref[...] = jnp.zeros(l_ref.shape, F32)
    acc_ref[...] = jnp.zeros(acc_ref.shape, F32)
    q = q_ref[...]

    def step(j, mask):
        start = pl.multiple_of(j * t, t)
        s = _dot_nt(q, k_ref[pl.ds(start, t), :]) * MLA_SCALE
        if mask is not None:
            s = jnp.where(mask, s, NEG_BIG)
        _softmax_step(s, v_ref[pl.ds(start, t), :], m_ref, l_ref, acc_ref)

    def body(j, carry):
        step(j, None)
        return carry

    lax.fori_loop(0, i, body, 0)
    step(i, _chunk_mask(t))
    o_ref[...] = (acc_ref[...] / l_ref[...]).astype(o_ref.dtype)


def _mla_attn(mq, mk, mv, *, t):
    b, s, _ = mq.shape
    return pl.pallas_call(
        functools.partial(_mla_attn_kernel, t=t),
        grid=(b, MLA_HEADS, s // t),
        in_specs=[
            pl.BlockSpec((None, t, 256), lambda bi, h, i: (bi, i, h)),
            pl.BlockSpec((None, s, 256), lambda bi, h, i: (bi, 0, h)),
            pl.BlockSpec((None, s, 128), lambda bi, h, i: (bi, 0, h)),
        ],
        out_specs=pl.BlockSpec((None, t, 128), lambda bi, h, i: (bi, i, h)),
        out_shape=jax.ShapeDtypeStruct((b, s, MLA_HEADS * 128), BF16),
        scratch_shapes=[pltpu.VMEM((t, 1), F32), pltpu.VMEM((t, 1), F32), pltpu.VMEM((t, 128), F32)],
        compiler_params=_params(3),
        name="mla_attn",
    )(mq, mk, mv)


def _sb_attn_kernel(q_ref, k_ref, v_ref, o_ref, carry_a, carry_b, acc_a, acc_b, z_a_ref, z_b_ref, *, tq, tk):
    i = pl.program_id(2)
    low = lax.broadcasted_iota(jnp.int32, (tq, LANES), 1) < SB_DH
    q = q_ref[...]
    zero = jnp.zeros_like(q)
    q_a = jnp.where(low, q, zero)
    q_b = jnp.where(low, zero, q)
    later = (lax.broadcasted_iota(jnp.int32, (tk, tk), 0) > lax.broadcasted_iota(jnp.int32, (tk, tk), 1)).astype(BF16)
    for ref in (carry_a, carry_b, acc_a, acc_b):
        ref[...] = jnp.zeros(ref.shape, F32)
    row = lax.broadcasted_iota(jnp.int32, (tq, tk), 0)
    col = lax.broadcasted_iota(jnp.int32, (tq, tk), 1)
    rc = tq

    def scores(j):
        k = k_ref[pl.ds(pl.multiple_of(jnp.maximum(j, 0) * tk, tk), tk), :]
        return _dot_nt(q_a, k), _dot_nt(q_b, k)

    def keep_sums(z, strict):
        neg_z = -z
        log_keep = jnp.minimum(neg_z, 0.0) - jnp.log(1.0 + jnp.exp(jnp.minimum(z, neg_z)))
        if strict is not None:
            log_keep = jnp.where(strict, log_keep, 0.0)
        hi = log_keep.astype(BF16)
        lo = (log_keep - hi.astype(F32)).astype(BF16)
        sums = _dot(jnp.concatenate([hi, lo], axis=0), later)
        return z + log_keep, jnp.sum(log_keep, axis=-1, keepdims=True), sums[:rc] + sums[rc:]

    def weights(log_beta, sums, total, carry_ref, rows, strict):
        a = jnp.exp(log_beta + (sums + carry_ref[rows, :]))
        if strict is not None:
            a = jnp.where(strict, a, 0.0)
        carry_ref[rows, :] += total
        return a.astype(BF16)

    def step(j, z_a, z_b, masked):
        v = v_ref[pl.ds(pl.multiple_of(j * tk, tk), tk), :]
        strict = (col + (j * tk - i * tq)) < row if masked else None
        staged = []
        for c in range(tq // rc):
            rows = slice(c * rc, (c + 1) * rc)
            strict_c = strict[rows] if masked else None
            for z, carry_ref, acc_ref in ((z_a, carry_a, acc_a), (z_b, carry_b, acc_b)):
                staged.append((rows, strict_c, carry_ref, acc_ref) + keep_sums(z[rows], strict_c))
        for rows, strict_c, carry_ref, acc_ref, log_beta, total, sums in staged:
            acc_ref[rows, :] += _dot(weights(log_beta, sums, total, carry_ref, rows, strict_c), v)

    def alive():
        return jnp.max(jnp.maximum(carry_a[...], carry_b[...])) > EXP_UNDERFLOW

    n_diag = tq // tk
    last = (i + 1) * n_diag - 1
    z = scores(last)
    for d in range(n_diag):
        z_next = scores(last - d - 1)
        step(last - d, z[0], z[1], True)
        z = z_next
    z_a_ref[...] = z[0]
    z_b_ref[...] = z[1]
    n_full = i * n_diag

    def cond(state):
        return jnp.logical_and(state[0] < n_full, state[1])

    def body(state):
        j = n_full - 1 - state[0]
        z_next = scores(j - 1)
        step(j, z_a_ref[...], z_b_ref[...], False)
        z_a_ref[...] = z_next[0]
        z_b_ref[...] = z_next[1]
        return state[0] + 1, alive()

    lax.while_loop(cond, body, (jnp.int32(0), alive()))
    o_ref[...] = jnp.where(low, acc_a[...], acc_b[...]).astype(o_ref.dtype)


def _sb_attn(qkv, *, tq, tk):
    b, s, _ = qkv.shape
    pairs = SB_HEADS * SB_DH // LANES
    return pl.pallas_call(
        functools.partial(_sb_attn_kernel, tq=tq, tk=tk),
        grid=(b, pairs, s // tq),
        in_specs=[
            pl.BlockSpec((None, tq, LANES), lambda bi, p, i: (bi, i, p)),
            pl.BlockSpec((None, s, LANES), lambda bi, p, i: (bi, 0, pairs + p)),
            pl.BlockSpec((None, s, LANES), lambda bi, p, i: (bi, 0, 2 * pairs + p)),
        ],
        out_specs=pl.BlockSpec((None, tq, LANES), lambda bi, p, i: (bi, i, p)),
        out_shape=jax.ShapeDtypeStruct((b, s, SB_HEADS * SB_DH), BF16),
        scratch_shapes=[pltpu.VMEM((tq, 1), F32), pltpu.VMEM((tq, 1), F32),
                        pltpu.VMEM((tq, LANES), F32), pltpu.VMEM((tq, LANES), F32),
                        pltpu.VMEM((tq, tk), F32), pltpu.VMEM((tq, tk), F32)],
        compiler_params=_params(3),
        name="sb_attn",
    )(qkv, qkv, qkv)


def _cross_kernel(h_ref, mix_ref, wmix_ref, g_ref, wq_ref, kv_ref, wo_ref, out_ref, o_scr):
    d = h_ref.shape[1]
    dh = d // XA_HEADS
    h1 = h_ref[...] + _dot(mix_ref[...], wmix_ref[...])
    q = _dot(_rms(h1, g_ref[...]).astype(BF16), wq_ref[...]).astype(BF16)
    for hd in range(XA_HEADS):
        s = _dot_nt(q[:, hd * dh:(hd + 1) * dh], kv_ref[:, hd * dh:(hd + 1) * dh])
        p = jnp.exp(s - jnp.max(s, axis=-1, keepdims=True))
        o = _dot(p.astype(BF16), kv_ref[:, d + hd * dh:d + (hd + 1) * dh]) / jnp.sum(p, axis=-1, keepdims=True)
        o_scr[:, hd * dh:(hd + 1) * dh] = o.astype(BF16)
    out_ref[...] = h1 + _dot(o_scr[...], wo_ref[...])


def _cross(h, mix, w_mix, g, wq, mem_kv, wo, *, seq, tm):
    n, d = h.shape
    blocks = seq // tm
    mem_len = mem_kv.shape[1]
    dh = d // XA_HEADS
    return pl.pallas_call(
        _cross_kernel,
        grid=(n // tm,),
        in_specs=[
            pl.BlockSpec((tm, d), lambda i: (i, 0)),
            pl.BlockSpec((tm, mix.shape[1]), lambda i: (i, 0)),
            _resident(w_mix.shape),
            _resident((1, d)),
            _resident(wq.shape),
            pl.BlockSpec((None, mem_len, 2 * d), lambda i: (i // blocks, 0, 0)),
            _resident(wo.shape),
        ],
        out_specs=pl.BlockSpec((tm, d), lambda i: (i, 0)),
        out_shape=jax.ShapeDtypeStruct((n, d), F32),
        scratch_shapes=[pltpu.VMEM((tm, d), BF16)],
        compiler_params=_params(1),
        name="cross_attn",
    )(h, mix, w_mix.astype(BF16), g.reshape(1, -1), (wq * dh ** -0.5).astype(BF16), mem_kv, wo.astype(BF16))


def _mlp_kernel(h_ref, g_ref, w1_ref, w2_ref, gf_ref, out_ref, *, tf, final_norm):
    x = h_ref[...]
    xn = _rms(x, g_ref[...]).astype(BF16)
    acc = x
    for c in range(w1_ref.shape[1] // tf):
        a = jnp.maximum(_dot(xn, w1_ref[:, c * tf:(c + 1) * tf]), 0.0)
        acc = acc + _dot((a * a).astype(BF16), w2_ref[c * tf:(c + 1) * tf, :])
    if final_norm:
        acc = _rms(acc, gf_ref[...])
    out_ref[...] = acc


def _mlp(h, g, w1, w2, g_final, *, tm, tf, final_norm):
    n, d = h.shape
    return pl.pallas_call(
        functools.partial(_mlp_kernel, tf=tf, final_norm=final_norm),
        grid=(n // tm,),
        in_specs=[
            pl.BlockSpec((tm, d), lambda i: (i, 0)),
            _resident((1, d)),
            _resident(w1.shape),
            _resident(w2.shape),
            _resident((1, d)),
        ],
        out_specs=pl.BlockSpec((tm, d), lambda i: (i, 0)),
        out_shape=jax.ShapeDtypeStruct((n, d), F32),
        compiler_params=_params(1),
        name="mlp",
    )(h, g.reshape(1, -1), w1.astype(BF16), w2.astype(BF16), g_final.reshape(1, -1))


def _rope_tables(seq):
    half = MLA_ROPE // 2
    inv = ROPE_BASE ** (-jnp.arange(half, dtype=F32) / half)
    ang = jnp.arange(seq).astype(F32)[:, None] * inv[None, :]
    pad = lambda a: jnp.concatenate([a, a, jnp.zeros((seq, LANES - 2 * half), F32)], axis=1)
    return pad(jnp.cos(ang)), pad(jnp.sin(ang))


def kernel(x, mem, ev_norm, ev_w_in, diff_lq1, diff_lk1, diff_lq2, diff_lk2, diff_subln, mla_g_cq, mla_w_uq, mla_g_ckv, mla_w_ukv, ev_w_out, od_norm, sb_w_in, sb_w_out, xa_norm, xa_mem_norm, xa_wq, xa_wkv, xa_wo, mlp_norm, mlp_w1, mlp_w2, final_norm):
    b, seq, d = x.shape
    mem_len = mem.shape[1]
    depth = xa_norm.shape[0]
    tm = 512
    h = x.reshape(b * seq, d)
    mem2 = mem.reshape(b * mem_len, d)
    cos_pad, sin_pad = _rope_tables(seq)
    for i in range(depth):
        j = i // 2
        if i % 2 == 0:
            lambda_init = 0.8 - 0.6 * math.exp(-0.3 * i)
            dq, dk, dv, mq, mk, mv = _even_proj(
                h, ev_norm[j], ev_w_in[j], mla_g_cq[j], mla_w_uq[j], mla_g_ckv[j], mla_w_ukv[j],
                cos_pad, sin_pad, seq=seq, tm=tm)
            split = lambda a: a.reshape(b, seq, a.shape[-1])
            o_a = _diff_attn(split(dq), split(dk), split(dv), diff_lq1[j], diff_lk1[j], diff_lq2[j], diff_lk2[j],
                             diff_subln[j], lambda_init, t=512)
            o_b = _mla_attn(split(mq), split(mk), split(mv), t=512)
            mix = jnp.concatenate([o_a, o_b], axis=-1).reshape(b * seq, -1)
            w_mix = ev_w_out[j]
        else:
            w_in = sb_w_in[j]
            width = SB_HEADS * SB_DH
            w_in = jnp.concatenate([w_in[:, :width] * (SB_DH ** -0.5), w_in[:, width:]], axis=1).astype(BF16)
            qkv = _norm_matmul(h, od_norm[j].reshape(1, -1), w_in, tm=tm, name="sb_proj")
            mix = _sb_attn(qkv.reshape(b, seq, -1), tq=512, tk=256).reshape(b * seq, -1)
            w_mix = sb_w_out[j]
        mem_kv = _norm_matmul(mem2, xa_mem_norm[i].reshape(1, -1), xa_wkv[i].astype(BF16), tm=mem_len, name="mem_kv")
        h = _cross(h, mix, w_mix, xa_norm[i], xa_wq[i], mem_kv.reshape(b, mem_len, -1), xa_wo[i], seq=seq, tm=tm)
        h = _mlp(h, mlp_norm[i], mlp_w1[i], mlp_w2[i], final_norm, tm=tm, tf=1024, final_norm=(i == depth - 1))
    return h.reshape(b, seq, d)
```

```python
import functools
import math

import jax
import jax.numpy as jnp
from jax import lax
from jax.experimental import pallas as pl
from jax.experimental.pallas import tpu as pltpu

F32 = jnp.float32
BF16 = jnp.bfloat16

EPS = 1e-6
CHUNK_SHIFT = 6
DIFF_HEADS = 4
DIFF_DK = 64
MLA_HEADS = 4
MLA_NOPE = 128
MLA_ROPE = 64
MLA_SCALE = (MLA_NOPE + MLA_ROPE) ** -0.5
ROPE_BASE = 10000.0
SB_HEADS = 16
SB_DH = 64
XA_HEADS = 4
NEG_BIG = -1e30
LOG2E = math.log2(math.e)
EXP_UNDERFLOW = -104.0
LANES = 128

VMEM_LIMIT_BYTES = 48 * 1024 * 1024


def _params(n_grid):
    return pltpu.CompilerParams(
        dimension_semantics=("arbitrary",) * n_grid, vmem_limit_bytes=VMEM_LIMIT_BYTES
    )


def _resident(shape):
    zeros = (0,) * len(shape)
    return pl.BlockSpec(shape, lambda *_: zeros, pipeline_mode=pl.Buffered(1))


def _rms(x, g):
    ms = jnp.mean(x * x, axis=-1, keepdims=True)
    return x * lax.rsqrt(ms + EPS) * g


def _dot(a, b):
    return jnp.dot(a, b, preferred_element_type=F32)


def _dot_nt(a, b):
    return lax.dot_general(a, b, (((1,), (1,)), ((), ())), preferred_element_type=F32)


def _norm_matmul_kernel(x_ref, g_ref, w_ref, o_ref):
    xn = _rms(x_ref[...], g_ref[...]).astype(BF16)
    o_ref[...] = _dot(xn, w_ref[...]).astype(o_ref.dtype)


def _norm_matmul(x, g, w, *, tm, name):
    n, d = x.shape
    f = w.shape[1]
    return pl.pallas_call(
        _norm_matmul_kernel,
        grid=(n // tm,),
        in_specs=[
            pl.BlockSpec((tm, d), lambda i: (i, 0)),
            _resident((1, d)),
            _resident((d, f)),
        ],
        out_specs=pl.BlockSpec((tm, f), lambda i: (i, 0)),
        out_shape=jax.ShapeDtypeStruct((n, f), BF16),
        compiler_params=_params(1),
        name=name,
    )(x, g, w)


def _even_proj_kernel(h_ref, g_ref, w_ref, gcq_ref, wq_ref, gckv_ref, wkv_ref, cos_ref, sin_ref,
                      dq_ref, dk_ref, dv_ref, mq_ref, mk_ref, mv_ref):
    tm = h_ref.shape[0]
    xn = _rms(h_ref[...], g_ref[...]).astype(BF16)
    proj = _dot(xn, w_ref[...])
    low = lax.broadcasted_iota(jnp.int32, (tm, LANES), 1) < DIFF_DK
    for h in range(DIFF_HEADS):
        blk = proj[:, h * 128:(h + 1) * 128]
        dq_ref[:, h * 256:h * 256 + 128] = jnp.where(low, blk, 0.0).astype(BF16)
        dq_ref[:, h * 256 + 128:(h + 1) * 256] = jnp.where(low, 0.0, blk).astype(BF16)
    dk_ref[...] = proj[:, 512:1024].astype(BF16)
    dv_ref[...] = proj[:, 1024:1536].T.astype(BF16)

    cos = cos_ref[...]
    sin = sin_ref[...]
    k_rope = (proj[:, 1920:2048] * cos + proj[:, 2048:2176] * sin).astype(BF16)

    cqn = _rms(proj[:, 1536:1792], gcq_ref[...]).astype(BF16)
    qm = _dot(cqn, wq_ref[...])
    ckvn = _rms(proj[:, 1792:1920], gckv_ref[...]).astype(BF16)
    kv = _dot(ckvn, wkv_ref[...])
    for h in range(MLA_HEADS):
        mq_ref[:, h * 256:h * 256 + 128] = qm[:, h * 128:(h + 1) * 128].astype(BF16)
        q_rope = qm[:, 512 + h * 128:512 + (h + 1) * 128] * cos + qm[:, 1024 + h * 128:1024 + (h + 1) * 128] * sin
        mq_ref[:, h * 256 + 128:(h + 1) * 256] = q_rope.astype(BF16)
        mk_ref[:, h * 256:h * 256 + 128] = kv[:, h * 256:h * 256 + 128].astype(BF16)
        mk_ref[:, h * 256 + 128:(h + 1) * 256] = k_rope
        mv_ref[h * 128:(h + 1) * 128, :] = kv[:, h * 256 + 128:(h + 1) * 256].T.astype(BF16)


def _rotated(w):
    half = w.shape[1] // 2
    return jnp.concatenate([-w[:, half:], w[:, :half]], axis=1)


def _pad_cols(w, width):
    return jnp.pad(w, ((0, 0), (0, width - w.shape[1])))


def _even_proj(h, g, w_in, g_cq, w_uq, g_ckv, w_ukv, cos_pad, sin_pad, *, seq, tm):
    n, d = h.shape
    kr = w_in[:, 1920:1984]
    w_ext = jnp.concatenate(
        [w_in[:, :512] * (DIFF_DK ** -0.5), w_in[:, 512:1920], _pad_cols(kr, 128), _pad_cols(_rotated(kr), 128)],
        axis=1).astype(BF16)
    per_head = MLA_NOPE + MLA_ROPE
    nope = [w_uq[:, hh * per_head:hh * per_head + MLA_NOPE] for hh in range(MLA_HEADS)]
    rope = [w_uq[:, hh * per_head + MLA_NOPE:(hh + 1) * per_head] for hh in range(MLA_HEADS)]
    wq_ext = jnp.concatenate(
        nope + [_pad_cols(r, 128) for r in rope] + [_pad_cols(_rotated(r), 128) for r in rope], axis=1).astype(BF16)
    pos_blocks = seq // tm
    bf = lambda width: jax.ShapeDtypeStruct((n, width), BF16)
    row = lambda width: pl.BlockSpec((tm, width), lambda i: (i, 0))
    vt_shape = jax.ShapeDtypeStruct((n // seq, pos_blocks, 512, tm), BF16)
    vt_spec = pl.BlockSpec((None, None, 512, tm), lambda i: (i // pos_blocks, i % pos_blocks, 0, 0))
    return pl.pallas_call(
        _even_proj_kernel,
        grid=(n // tm,),
        in_specs=[
            row(d),
            _resident((1, d)),
            _resident(w_ext.shape),
            _resident((1, g_cq.shape[-1])),
            _resident(wq_ext.shape),
            _resident((1, g_ckv.shape[-1])),
            _resident(w_ukv.shape),
            pl.BlockSpec((tm, 128), lambda i: (i % pos_blocks, 0)),
            pl.BlockSpec((tm, 128), lambda i: (i % pos_blocks, 0)),
        ],
        out_specs=[row(1024), row(512), vt_spec, row(1024), row(1024), vt_spec],
        out_shape=[bf(1024), bf(512), vt_shape, bf(1024), bf(1024), vt_shape],
        compiler_params=_params(1),
        name="even_proj",
    )(h, g.reshape(1, -1), w_ext, g_cq.reshape(1, -1), wq_ext, g_ckv.reshape(1, -1), w_ukv.astype(BF16),
      cos_pad, sin_pad)


class _Softmax:
    def __init__(self, s0, s1, p, a, m, l, acc):
        self.s, self.p, self.a, self.m, self.l, self.acc = (s0, s1), p, a, m, l, acc

    def reset(self):
        self.m[...] = jnp.full(self.m.shape, NEG_BIG, F32)
        self.l[...] = jnp.zeros(self.l.shape, F32)
        self.acc[...] = jnp.zeros(self.acc.shape, F32)
        self.p[...] = jnp.zeros(self.p.shape, BF16)
        self.a[...] = jnp.ones(self.a.shape, F32)

    def weigh(self, cur, scale, bias, shift):
        logits = self.s[cur][...] * scale
        if bias is not None:
            logits = logits + bias
        m_prev = self.m[...]
        m_new = jnp.maximum(m_prev, jnp.max(logits, axis=0, keepdims=True) + shift)
        alpha = jnp.exp2(m_prev - m_new)
        p = jnp.exp2(logits - (m_new - shift))
        self.l[...] = alpha * self.l[...] + jnp.sum(p, axis=0, keepdims=True)
        self.m[...] = m_new
        self.p[...] = p.astype(BF16)
        self.a[...] = alpha

    def start_values(self, vt):
        return self.a[...], _dot(vt, self.p[...])

    def fold(self, alpha, pv):
        self.acc[...] = alpha * self.acc[...] + pv


def _run_key_blocks(i, sms, scores, vt_ref, off_diagonal, diagonal):
    for sm in sms:
        sm.reset()
    for sm, s in zip(sms, scores(0)):
        sm.s[0][...] = s

    def iteration(j, cur):
        vt = vt_ref[jnp.maximum(j - 1, 0)]
        pending = [sm.start_values(vt) for sm in sms]
        for sm, s in zip(sms, scores(j + 1)):
            sm.s[1 - cur][...] = s
        args = off_diagonal(j)
        for sm in sms:
            sm.weigh(cur, *args)
        for sm, pend in zip(sms, pending):
            sm.fold(*pend)

    def pair(jj, carry):
        iteration(2 * jj, 0)
        iteration(2 * jj + 1, 1)
        return carry

    lax.fori_loop(0, i >> 1, pair, 0)

    @pl.when((i & 1) == 1)
    def _():
        iteration(i - 1, 0)
        for sm in sms:
            sm.s[0][...] = sm.s[1][...]

    vt = vt_ref[jnp.maximum(i - 1, 0)]
    pending = [sm.start_values(vt) for sm in sms]
    for sm in sms:
        sm.weigh(0, *diagonal)
    for sm, pend in zip(sms, pending):
        sm.fold(*pend)
    for sm in sms:
        sm.fold(*sm.start_values(vt_ref[i]))


def _softmax_scratch(t):
    return [pltpu.VMEM((t, t), F32), pltpu.VMEM((t, t), F32), pltpu.VMEM((t, t), BF16), pltpu.VMEM((1, t), F32),
            pltpu.VMEM((1, t), F32), pltpu.VMEM((1, t), F32), pltpu.VMEM((128, t), F32)]


def _key_query_iotas(t):
    return lax.broadcasted_iota(jnp.int32, (t, t), 0), lax.broadcasted_iota(jnp.int32, (t, t), 1)


def _diff_attn_kernel(q_ref, k_ref, vt_ref, slope_ref, lq1_ref, lk1_ref, lq2_ref, lk2_ref, gsub_ref, o_ref,
                      bias_off, bias_diag, *scratch, t, lambda_init):
    i = pl.program_id(2)
    slope2 = slope_ref[:, :1] * LOG2E
    sm1 = _Softmax(*scratch[:7])
    sm2 = _Softmax(*scratch[7:])

    @pl.when(i == 0)
    def _():
        key, qry = _key_query_iotas(t)
        rel = (qry - key).astype(F32)
        bias_off[...] = -slope2 * rel
        allowed = (key >> CHUNK_SHIFT) <= (qry >> CHUNK_SHIFT)
        bias_diag[...] = jnp.where(allowed, -slope2 * jnp.abs(rel), NEG_BIG)

    def scores(j):
        k = k_ref[pl.ds(pl.multiple_of(j * t, t), t), :]
        return _dot_nt(k, q_ref[:, :128]), _dot_nt(k, q_ref[:, 128:])

    def off_diagonal(j):
        return LOG2E, bias_off[...], -slope2 * ((i - j) * t).astype(F32)

    _run_key_blocks(i, (sm1, sm2), scores, vt_ref, off_diagonal, (LOG2E, bias_diag[...], jnp.zeros((1, 1), F32)))

    lam = (jnp.exp(jnp.sum(lq1_ref[...] * lk1_ref[...], axis=-1, keepdims=True))
           - jnp.exp(jnp.sum(lq2_ref[...] * lk2_ref[...], axis=-1, keepdims=True)) + lambda_init)
    o = (sm1.acc[...] / sm1.l[...] - lam * (sm2.acc[...] / sm2.l[...])).T
    o_ref[...] = (_rms(o, gsub_ref[...]) * (1.0 - lambda_init)).astype(o_ref.dtype)


def _diff_attn(dq, dk, dv, lq1, lk1, lq2, lk2, g_sub, lambda_init, *, t):
    b, s, _ = dq.shape
    slopes = 2.0 ** (-8.0 * jnp.arange(1, DIFF_HEADS + 1, dtype=F32) / DIFF_HEADS)
    slopes = jnp.broadcast_to(slopes[:, None, None], (DIFF_HEADS, 1, LANES))
    vec = lambda a: a.reshape(1, -1)
    small = lambda width: pl.BlockSpec((1, width), lambda bi, h, i: (0, 0))
    kern = functools.partial(_diff_attn_kernel, t=t, lambda_init=lambda_init)
    return pl.pallas_call(
        kern,
        grid=(b, DIFF_HEADS, s // t),
        in_specs=[
            pl.BlockSpec((None, t, 256), lambda bi, h, i: (bi, i, h)),
            pl.BlockSpec((None, s, 128), lambda bi, h, i: (bi, 0, h)),
            pl.BlockSpec((None, s // t, 128, t), lambda bi, h, i: (bi, 0, h, 0)),
            pl.BlockSpec((None, 1, LANES), lambda bi, h, i: (h, 0, 0)),
            small(DIFF_DK), small(DIFF_DK), small(DIFF_DK), small(DIFF_DK),
            small(2 * DIFF_DK),
        ],
        out_specs=pl.BlockSpec((None, t, 128), lambda bi, h, i: (bi, i, h)),
        out_shape=jax.ShapeDtypeStruct((b, s, DIFF_HEADS * 128), BF16),
        scratch_shapes=[pltpu.VMEM((t, t), F32)] * 2 + _softmax_scratch(t) * 2,
        compiler_params=_params(3),
        name="diff_attn",
    )(dq, dk, dv, slopes, vec(lq1), vec(lk1), vec(lq2), vec(lk2), vec(g_sub))


def _mla_attn_kernel(q_ref, k_ref, vt_ref, o_ref, mask_diag, *scratch, t):
    i = pl.program_id(2)
    sm = _Softmax(*scratch)

    @pl.when(i == 0)
    def _():
        key, qry = _key_query_iotas(t)
        mask_diag[...] = jnp.where((key >> CHUNK_SHIFT) <= (qry >> CHUNK_SHIFT), 0.0, NEG_BIG)

    def scores(j):
        return (_dot_nt(k_ref[pl.ds(pl.multiple_of(j * t, t), t), :], q_ref[...]),)

    scale = MLA_SCALE * LOG2E
    no_shift = jnp.zeros((1, 1), F32)
    _run_key_blocks(i, (sm,), scores, vt_ref, lambda j: (scale, None, no_shift), (scale, mask_diag[...], no_shift))
    o_ref[...] = (sm.acc[...] / sm.l[...]).T.astype(o_ref.dtype)


def _mla_attn(mq, mk, mv, *, t):
    b, s, _ = mq.shape
    return pl.pallas_call(
        functools.partial(_mla_attn_kernel, t=t),
        grid=(b, MLA_HEADS, s // t),
        in_specs=[
            pl.BlockSpec((None, t, 256), lambda bi, h, i: (bi, i, h)),
            pl.BlockSpec((None, s, 256), lambda bi, h, i: (bi, 0, h)),
            pl.BlockSpec((None, s // t, 128, t), lambda bi, h, i: (bi, 0, h, 0)),
        ],
        out_specs=pl.BlockSpec((None, t, 128), lambda bi, h, i: (bi, i, h)),
        out_shape=jax.ShapeDtypeStruct((b, s, MLA_HEADS * 128), BF16),
        scratch_shapes=[pltpu.VMEM((t, t), F32)] + _softmax_scratch(t),
        compiler_params=_params(3),
        name="mla_attn",
    )(mq, mk, mv)


def _sb_attn_kernel(q_ref, k_ref, v_ref, o_ref, carry_a, carry_b, acc_a, acc_b, z_a_ref, z_b_ref, *, tq, tk):
    i = pl.program_id(2)
    low = lax.broadcasted_iota(jnp.int32, (tq, LANES), 1) < SB_DH
    q = q_ref[...]
    zero = jnp.zeros_like(q)
    q_a = jnp.where(low, q, zero)
    q_b = jnp.where(low, zero, q)
    later = (lax.broadcasted_iota(jnp.int32, (tk, tk), 0) > lax.broadcasted_iota(jnp.int32, (tk, tk), 1)).astype(BF16)
    for ref in (carry_a, carry_b, acc_a, acc_b):
        ref[...] = jnp.zeros(ref.shape, F32)
    row = lax.broadcasted_iota(jnp.int32, (tq, tk), 0)
    col = lax.broadcasted_iota(jnp.int32, (tq, tk), 1)
    rc = tq

    def scores(j):
        k = k_ref[pl.ds(pl.multiple_of(jnp.maximum(j, 0) * tk, tk), tk), :]
        return _dot_nt(q_a, k), _dot_nt(q_b, k)

    def keep_sums(z, strict):
        neg_z = -z
        log_keep = jnp.minimum(neg_z, 0.0) - jnp.log(1.0 + jnp.exp(jnp.minimum(z, neg_z)))
        if strict is not None:
            log_keep = jnp.where(strict, log_keep, 0.0)
        hi = log_keep.astype(BF16)
        lo = (log_keep - hi.astype(F32)).astype(BF16)
        sums = _dot(jnp.concatenate([hi, lo], axis=0), later)
        return z + log_keep, jnp.sum(log_keep, axis=-1, keepdims=True), sums[:rc] + sums[rc:]

    def weights(log_beta, sums, total, carry_ref, rows, strict):
        a = jnp.exp(log_beta + (sums + carry_ref[rows, :]))
        if strict is not None:
            a = jnp.where(strict, a, 0.0)
        carry_ref[rows, :] += total
        return a.astype(BF16)

    def step(j, z_a, z_b, masked):
        v = v_ref[pl.ds(pl.multiple_of(j * tk, tk), tk), :]
        strict = (col + (j * tk - i * tq)) < row if masked else None
        staged = []
        for c in range(tq // rc):
            rows = slice(c * rc, (c + 1) * rc)
            strict_c = strict[rows] if masked else None
            for z, carry_ref, acc_ref in ((z_a, carry_a, acc_a), (z_b, carry_b, acc_b)):
                staged.append((rows, strict_c, carry_ref, acc_ref) + keep_sums(z[rows], strict_c))
        for rows, strict_c, carry_ref, acc_ref, log_beta, total, sums in staged:
            acc_ref[rows, :] += _dot(weights(log_beta, sums, total, carry_ref, rows, strict_c), v)

    def alive():
        return jnp.max(jnp.maximum(carry_a[...], carry_b[...])) > EXP_UNDERFLOW

    n_diag = tq // tk
    last = (i + 1) * n_diag - 1
    z = scores(last)
    for d in range(n_diag):
        z_next = scores(last - d - 1)
        step(last - d, z[0], z[1], True)
        z = z_next
    z_a_ref[...] = z[0]
    z_b_ref[...] = z[1]
    n_full = i * n_diag

    def cond(state):
        return jnp.logical_and(state[0] < n_full, state[1])

    def body(state):
        j = n_full - 1 - state[0]
        z_next = scores(j - 1)
        step(j, z_a_ref[...], z_b_ref[...], False)
        z_a_ref[...] = z_next[0]
        z_b_ref[...] = z_next[1]
        return state[0] + 1, alive()

    lax.while_loop(cond, body, (jnp.int32(0), alive()))
    o_ref[...] = jnp.where(low, acc_a[...], acc_b[...]).astype(o_ref.dtype)


def _sb_attn(qkv, *, tq, tk):
    b, s, _ = qkv.shape
    pairs = SB_HEADS * SB_DH // LANES
    return pl.pallas_call(
        functools.partial(_sb_attn_kernel, tq=tq, tk=tk),
        grid=(b, pairs, s // tq),
        in_specs=[
            pl.BlockSpec((None, tq, LANES), lambda bi, p, i: (bi, i, p)),
            pl.BlockSpec((None, s, LANES), lambda bi, p, i: (bi, 0, pairs + p)),
            pl.BlockSpec((None, s, LANES), lambda bi, p, i: (bi, 0, 2 * pairs + p)),
        ],
        out_specs=pl.BlockSpec((None, tq, LANES), lambda bi, p, i: (bi, i, p)),
        out_shape=jax.ShapeDtypeStruct((b, s, SB_HEADS * SB_DH), BF16),
        scratch_shapes=[pltpu.VMEM((tq, 1), F32), pltpu.VMEM((tq, 1), F32),
                        pltpu.VMEM((tq, LANES), F32), pltpu.VMEM((tq, LANES), F32),
                        pltpu.VMEM((tq, tk), F32), pltpu.VMEM((tq, tk), F32)],
        compiler_params=_params(3),
        name="sb_attn",
    )(qkv, qkv, qkv)


def _cross_kernel(h_ref, mix_ref, wmix_ref, g_ref, wq_ref, kv_ref, wo_ref, out_ref, o_scr):
    d = h_ref.shape[1]
    dh = d // XA_HEADS
    h1 = h_ref[...] + _dot(mix_ref[...], wmix_ref[...])
    q = _dot(_rms(h1, g_ref[...]).astype(BF16), wq_ref[...]).astype(BF16)
    for hd in range(XA_HEADS):
        s = _dot_nt(q[:, hd * dh:(hd + 1) * dh], kv_ref[:, hd * dh:(hd + 1) * dh])
        p = jnp.exp(s - jnp.max(s, axis=-1, keepdims=True))
        o = _dot(p.astype(BF16), kv_ref[:, d + hd * dh:d + (hd + 1) * dh]) / jnp.sum(p, axis=-1, keepdims=True)
        o_scr[:, hd * dh:(hd + 1) * dh] = o.astype(BF16)
    out_ref[...] = h1 + _dot(o_scr[...], wo_ref[...])


def _cross(h, mix, w_mix, g, wq, mem_kv, wo, *, seq, tm):
    n, d = h.shape
    blocks = seq // tm
    mem_len = mem_kv.shape[1]
    dh = d // XA_HEADS
    return pl.pallas_call(
        _cross_kernel,
        grid=(n // tm,),
        in_specs=[
            pl.BlockSpec((tm, d), lambda i: (i, 0)),
            pl.BlockSpec((tm, mix.shape[1]), lambda i: (i, 0)),
            _resident(w_mix.shape),
            _resident((1, d)),
            _resident(wq.shape),
            pl.BlockSpec((None, mem_len, 2 * d), lambda i: (i // blocks, 0, 0)),
            _resident(wo.shape),
        ],
        out_specs=pl.BlockSpec((tm, d), lambda i: (i, 0)),
        out_shape=jax.ShapeDtypeStruct((n, d), F32),
        scratch_shapes=[pltpu.VMEM((tm, d), BF16)],
        compiler_params=_params(1),
        name="cross_attn",
    )(h, mix, w_mix.astype(BF16), g.reshape(1, -1), (wq * dh ** -0.5).astype(BF16), mem_kv, wo.astype(BF16))


def _mlp_kernel(h_ref, g_ref, w1_ref, w2_ref, gf_ref, out_ref, *, tf, final_norm):
    x = h_ref[...]
    xn = _rms(x, g_ref[...]).astype(BF16)
    acc = x
    for c in range(w1_ref.shape[1] // tf):
        a = jnp.maximum(_dot(xn, w1_ref[:, c * tf:(c + 1) * tf]), 0.0)
        acc = acc + _dot((a * a).astype(BF16), w2_ref[c * tf:(c + 1) * tf, :])
    if final_norm:
        acc = _rms(acc, gf_ref[...])
    out_ref[...] = acc


def _mlp(h, g, w1, w2, g_final, *, tm, tf, final_norm):
    n, d = h.shape
    return pl.pallas_call(
        functools.partial(_mlp_kernel, tf=tf, final_norm=final_norm),
        grid=(n // tm,),
        in_specs=[
            pl.BlockSpec((tm, d), lambda i: (i, 0)),
            _resident((1, d)),
            _resident(w1.shape),
            _resident(w2.shape),
            _resident((1, d)),
        ],
        out_specs=pl.BlockSpec((tm, d), lambda i: (i, 0)),
        out_shape=jax.ShapeDtypeStruct((n, d), F32),
        compiler_params=_params(1),
        name="mlp",
    )(h, g.reshape(1, -1), w1.astype(BF16), w2.astype(BF16), g_final.reshape(1, -1))


def _rope_tables(seq):
    half = MLA_ROPE // 2
    inv = ROPE_BASE ** (-jnp.arange(half, dtype=F32) / half)
    ang = jnp.arange(seq).astype(F32)[:, None] * inv[None, :]
    pad = lambda a: jnp.concatenate([a, a, jnp.zeros((seq, LANES - 2 * half), F32)], axis=1)
    return pad(jnp.cos(ang)), pad(jnp.sin(ang))


def kernel(x, mem, ev_norm, ev_w_in, diff_lq1, diff_lk1, diff_lq2, diff_lk2, diff_subln, mla_g_cq, mla_w_uq, mla_g_ckv, mla_w_ukv, ev_w_out, od_norm, sb_w_in, sb_w_out, xa_norm, xa_mem_norm, xa_wq, xa_wkv, xa_wo, mlp_norm, mlp_w1, mlp_w2, final_norm):
    b, seq, d = x.shape
    mem_len = mem.shape[1]
    depth = xa_norm.shape[0]
    tm = 512
    h = x.reshape(b * seq, d)
    mem2 = mem.reshape(b * mem_len, d)
    cos_pad, sin_pad = _rope_tables(seq)
    for i in range(depth):
        j = i // 2
        if i % 2 == 0:
            lambda_init = 0.8 - 0.6 * math.exp(-0.3 * i)
            dq, dk, dv, mq, mk, mv = _even_proj(
                h, ev_norm[j], ev_w_in[j], mla_g_cq[j], mla_w_uq[j], mla_g_ckv[j], mla_w_ukv[j],
                cos_pad, sin_pad, seq=seq, tm=tm)
            split = lambda a: a.reshape(b, seq, a.shape[-1])
            o_a = _diff_attn(split(dq), split(dk), dv, diff_lq1[j], diff_lk1[j], diff_lq2[j], diff_lk2[j],
                             diff_subln[j], lambda_init, t=tm)
            o_b = _mla_attn(split(mq), split(mk), mv, t=tm)
            mix = jnp.concatenate([o_a, o_b], axis=-1).reshape(b * seq, -1)
            w_mix = ev_w_out[j]
        else:
            w_in = sb_w_in[j]
            width = SB_HEADS * SB_DH
            w_in = jnp.concatenate([w_in[:, :width] * (SB_DH ** -0.5), w_in[:, width:]], axis=1).astype(BF16)
            qkv = _norm_matmul(h, od_norm[j].reshape(1, -1), w_in, tm=tm, name="sb_proj")
            mix = _sb_attn(qkv.reshape(b, seq, -1), tq=512, tk=256).reshape(b * seq, -1)
            w_mix = sb_w_out[j]
        mem_kv = _norm_matmul(mem2, xa_mem_norm[i].reshape(1, -1), xa_wkv[i].astype(BF16), tm=mem_len, name="mem_kv")
        h = _cross(h, mix, w_mix, xa_norm[i], xa_wq[i], mem_kv.reshape(b, mem_len, -1), xa_wo[i], seq=seq, tm=tm)
        h = _mlp(h, mlp_norm[i], mlp_w1[i], mlp_w2[i], final_norm, tm=tm, tf=1024, final_norm=(i == depth - 1))
    return h.reshape(b, seq, d)
```

```python
import functools
import math

import jax
import jax.numpy as jnp
from jax import lax
from jax.experimental import pallas as pl
from jax.experimental.pallas import tpu as pltpu

F32 = jnp.float32
BF16 = jnp.bfloat16

EPS = 1e-6
CHUNK_SHIFT = 6
DIFF_HEADS = 4
DIFF_DK = 64
MLA_HEADS = 4
MLA_NOPE = 128
MLA_ROPE = 64
MLA_SCALE = (MLA_NOPE + MLA_ROPE) ** -0.5
ROPE_BASE = 10000.0
SB_HEADS = 16
SB_DH = 64
XA_HEADS = 4
NEG_BIG = -1e30
LOG2E = math.log2(math.e)
EXP_UNDERFLOW = -104.0
LANES = 128

VMEM_LIMIT_BYTES = 48 * 1024 * 1024


def _params(n_grid):
    return pltpu.CompilerParams(
        dimension_semantics=("arbitrary",) * n_grid, vmem_limit_bytes=VMEM_LIMIT_BYTES
    )


def _resident(shape):
    zeros = (0,) * len(shape)
    return pl.BlockSpec(shape, lambda *_: zeros, pipeline_mode=pl.Buffered(1))


def _rms(x, g):
    ms = jnp.mean(x * x, axis=-1, keepdims=True)
    return x * lax.rsqrt(ms + EPS) * g


def _dot(a, b):
    return jnp.dot(a, b, preferred_element_type=F32)


def _dot_nt(a, b):
    return lax.dot_general(a, b, (((1,), (1,)), ((), ())), preferred_element_type=F32)


def _norm_matmul_kernel(x_ref, g_ref, w_ref, o_ref):
    xn = _rms(x_ref[...], g_ref[...]).astype(BF16)
    o_ref[...] = _dot(xn, w_ref[...]).astype(o_ref.dtype)


def _norm_matmul(x, g, w, *, tm, name):
    n, d = x.shape
    f = w.shape[1]
    return pl.pallas_call(
        _norm_matmul_kernel,
        grid=(n // tm,),
        in_specs=[
            pl.BlockSpec((tm, d), lambda i: (i, 0)),
            _resident((1, d)),
            _resident((d, f)),
        ],
        out_specs=pl.BlockSpec((tm, f), lambda i: (i, 0)),
        out_shape=jax.ShapeDtypeStruct((n, f), BF16),
        compiler_params=_params(1),
        name=name,
    )(x, g, w)


def _even_proj_kernel(h_ref, g_ref, w_ref, gcq_ref, wq_ref, gckv_ref, wkv_ref, cos_ref, sin_ref,
                      dq_ref, dk_ref, dv_ref, mq_ref, mk_ref, mv_ref):
    tm = h_ref.shape[0]
    xn = _rms(h_ref[...], g_ref[...]).astype(BF16)
    proj = _dot(xn, w_ref[...])
    low = lax.broadcasted_iota(jnp.int32, (tm, LANES), 1) < DIFF_DK
    for h in range(DIFF_HEADS):
        blk = proj[:, h * 128:(h + 1) * 128]
        dq_ref[:, h * 256:h * 256 + 128] = jnp.where(low, blk, 0.0).astype(BF16)
        dq_ref[:, h * 256 + 128:(h + 1) * 256] = jnp.where(low, 0.0, blk).astype(BF16)
    dk_ref[...] = proj[:, 512:1024].astype(BF16)
    dv_ref[...] = proj[:, 1024:1536].T.astype(BF16)

    cos = cos_ref[...]
    sin = sin_ref[...]
    k_rope = (proj[:, 1920:2048] * cos + proj[:, 2048:2176] * sin).astype(BF16)

    cqn = _rms(proj[:, 1536:1792], gcq_ref[...]).astype(BF16)
    qm = _dot(cqn, wq_ref[...])
    ckvn = _rms(proj[:, 1792:1920], gckv_ref[...]).astype(BF16)
    kv = _dot(ckvn, wkv_ref[...])
    for h in range(MLA_HEADS):
        mq_ref[:, h * 256:h * 256 + 128] = qm[:, h * 128:(h + 1) * 128].astype(BF16)
        q_rope = qm[:, 512 + h * 128:512 + (h + 1) * 128] * cos + qm[:, 1024 + h * 128:1024 + (h + 1) * 128] * sin
        mq_ref[:, h * 256 + 128:(h + 1) * 256] = q_rope.astype(BF16)
        mk_ref[:, h * 256:h * 256 + 128] = kv[:, h * 256:h * 256 + 128].astype(BF16)
        mk_ref[:, h * 256 + 128:(h + 1) * 256] = k_rope
        mv_ref[h * 128:(h + 1) * 128, :] = kv[:, h * 256 + 128:(h + 1) * 256].T.astype(BF16)


def _rotated(w):
    half = w.shape[1] // 2
    return jnp.concatenate([-w[:, half:], w[:, :half]], axis=1)


def _pad_cols(w, width):
    return jnp.pad(w, ((0, 0), (0, width - w.shape[1])))


def _even_proj(h, g, w_in, g_cq, w_uq, g_ckv, w_ukv, cos_pad, sin_pad, *, seq, tm):
    n, d = h.shape
    kr = w_in[:, 1920:1984]
    w_ext = jnp.concatenate(
        [w_in[:, :512] * (DIFF_DK ** -0.5), w_in[:, 512:1920], _pad_cols(kr, 128), _pad_cols(_rotated(kr), 128)],
        axis=1).astype(BF16)
    per_head = MLA_NOPE + MLA_ROPE
    nope = [w_uq[:, hh * per_head:hh * per_head + MLA_NOPE] for hh in range(MLA_HEADS)]
    rope = [w_uq[:, hh * per_head + MLA_NOPE:(hh + 1) * per_head] for hh in range(MLA_HEADS)]
    wq_ext = jnp.concatenate(
        nope + [_pad_cols(r, 128) for r in rope] + [_pad_cols(_rotated(r), 128) for r in rope], axis=1).astype(BF16)
    pos_blocks = seq // tm
    bf = lambda width: jax.ShapeDtypeStruct((n, width), BF16)
    row = lambda width: pl.BlockSpec((tm, width), lambda i: (i, 0))
    vt_shape = jax.ShapeDtypeStruct((n // seq, pos_blocks, 512, tm), BF16)
    vt_spec = pl.BlockSpec((None, None, 512, tm), lambda i: (i // pos_blocks, i % pos_blocks, 0, 0))
    return pl.pallas_call(
        _even_proj_kernel,
        grid=(n // tm,),
        in_specs=[
            row(d),
            _resident((1, d)),
            _resident(w_ext.shape),
            _resident((1, g_cq.shape[-1])),
            _resident(wq_ext.shape),
            _resident((1, g_ckv.shape[-1])),
            _resident(w_ukv.shape),
            pl.BlockSpec((tm, 128), lambda i: (i % pos_blocks, 0)),
            pl.BlockSpec((tm, 128), lambda i: (i % pos_blocks, 0)),
        ],
        out_specs=[row(1024), row(512), vt_spec, row(1024), row(1024), vt_spec],
        out_shape=[bf(1024), bf(512), vt_shape, bf(1024), bf(1024), vt_shape],
        compiler_params=_params(1),
        name="even_proj",
    )(h, g.reshape(1, -1), w_ext, g_cq.reshape(1, -1), wq_ext, g_ckv.reshape(1, -1), w_ukv.astype(BF16),
      cos_pad, sin_pad)


class _Softmax:
    def __init__(self, s0, s1, p, a, m, l, acc):
        self.s, self.p, self.a, self.m, self.l, self.acc = (s0, s1), p, a, m, l, acc

    def reset(self):
        self.m[...] = jnp.full(self.m.shape, NEG_BIG, F32)
        self.l[...] = jnp.zeros(self.l.shape, F32)
        self.acc[...] = jnp.zeros(self.acc.shape, F32)
        self.p[...] = jnp.zeros(self.p.shape, BF16)
        self.a[...] = jnp.ones(self.a.shape, F32)

    def weigh(self, cur, scale, bias, shift):
        logits = self.s[cur][...] * scale
        if bias is not None:
            logits = logits + bias
        m_prev = self.m[...]
        m_new = jnp.maximum(m_prev, jnp.max(logits, axis=0, keepdims=True) + shift)
        alpha = jnp.exp2(m_prev - m_new)
        p = jnp.exp2(logits - (m_new - shift))
        self.l[...] = alpha * self.l[...] + jnp.sum(p, axis=0, keepdims=True)
        self.m[...] = m_new
        self.p[...] = p.astype(BF16)
        self.a[...] = alpha

    def start_values(self, vt):
        return self.a[...], _dot(vt, self.p[...])

    def fold(self, alpha, pv):
        self.acc[...] = alpha * self.acc[...] + pv


def _run_key_blocks(i, sms, scores, vt_ref, off_diagonal, diagonal):
    for sm in sms:
        sm.reset()
    for sm, s in zip(sms, scores(0)):
        sm.s[0][...] = s

    def iteration(j, cur):
        vt = vt_ref[jnp.maximum(j - 1, 0)]
        pending = [sm.start_values(vt) for sm in sms]
        for sm, s in zip(sms, scores(j + 1)):
            sm.s[1 - cur][...] = s
        args = off_diagonal(j)
        for sm in sms:
            sm.weigh(cur, *args)
        for sm, pend in zip(sms, pending):
            sm.fold(*pend)

    def pair(jj, carry):
        iteration(2 * jj, 0)
        iteration(2 * jj + 1, 1)
        return carry

    lax.fori_loop(0, i >> 1, pair, 0)

    @pl.when((i & 1) == 1)
    def _():
        iteration(i - 1, 0)
        for sm in sms:
            sm.s[0][...] = sm.s[1][...]

    vt = vt_ref[jnp.maximum(i - 1, 0)]
    pending = [sm.start_values(vt) for sm in sms]
    for sm in sms:
        sm.weigh(0, *diagonal)
    for sm, pend in zip(sms, pending):
        sm.fold(*pend)
    for sm in sms:
        sm.fold(*sm.start_values(vt_ref[i]))


def _softmax_scratch(t):
    return [pltpu.VMEM((t, t), F32), pltpu.VMEM((t, t), F32), pltpu.VMEM((t, t), BF16), pltpu.VMEM((1, t), F32),
            pltpu.VMEM((1, t), F32), pltpu.VMEM((1, t), F32), pltpu.VMEM((128, t), F32)]


def _key_query_iotas(t):
    return lax.broadcasted_iota(jnp.int32, (t, t), 0), lax.broadcasted_iota(jnp.int32, (t, t), 1)


def _diff_attn_kernel(q_ref, k_ref, vt_ref, slope_ref, lq1_ref, lk1_ref, lq2_ref, lk2_ref, gsub_ref, o_ref,
                      bias_off, bias_diag, *scratch, t, lambda_init):
    i = pl.program_id(2)
    slope2 = slope_ref[:, :1] * LOG2E
    sm1 = _Softmax(*scratch[:7])
    sm2 = _Softmax(*scratch[7:])

    @pl.when(i == 0)
    def _():
        key, qry = _key_query_iotas(t)
        rel = (qry - key).astype(F32)
        bias_off[...] = -slope2 * rel
        allowed = (key >> CHUNK_SHIFT) <= (qry >> CHUNK_SHIFT)
        bias_diag[...] = jnp.where(allowed, -slope2 * jnp.abs(rel), NEG_BIG)

    def scores(j):
        k = k_ref[pl.ds(pl.multiple_of(j * t, t), t), :]
        return _dot_nt(k, q_ref[:, :128]), _dot_nt(k, q_ref[:, 128:])

    def off_diagonal(j):
        return LOG2E, bias_off[...], -slope2 * ((i - j) * t).astype(F32)

    _run_key_blocks(i, (sm1, sm2), scores, vt_ref, off_diagonal, (LOG2E, bias_diag[...], jnp.zeros((1, 1), F32)))

    lam = (jnp.exp(jnp.sum(lq1_ref[...] * lk1_ref[...], axis=-1, keepdims=True))
           - jnp.exp(jnp.sum(lq2_ref[...] * lk2_ref[...], axis=-1, keepdims=True)) + lambda_init)
    o = (sm1.acc[...] / sm1.l[...] - lam * (sm2.acc[...] / sm2.l[...])).T
    o_ref[...] = (_rms(o, gsub_ref[...]) * (1.0 - lambda_init)).astype(o_ref.dtype)


def _diff_attn(dq, dk, dv, lq1, lk1, lq2, lk2, g_sub, lambda_init, *, t):
    b, s, _ = dq.shape
    slopes = 2.0 ** (-8.0 * jnp.arange(1, DIFF_HEADS + 1, dtype=F32) / DIFF_HEADS)
    slopes = jnp.broadcast_to(slopes[:, None, None], (DIFF_HEADS, 1, LANES))
    vec = lambda a: a.reshape(1, -1)
    small = lambda width: pl.BlockSpec((1, width), lambda bi, h, i: (0, 0))
    kern = functools.partial(_diff_attn_kernel, t=t, lambda_init=lambda_init)
    return pl.pallas_call(
        kern,
        grid=(b, DIFF_HEADS, s // t),
        in_specs=[
            pl.BlockSpec((None, t, 256), lambda bi, h, i: (bi, i, h)),
            pl.BlockSpec((None, s, 128), lambda bi, h, i: (bi, 0, h)),
            pl.BlockSpec((None, s // t, 128, t), lambda bi, h, i: (bi, 0, h, 0)),
            pl.BlockSpec((None, 1, LANES), lambda bi, h, i: (h, 0, 0)),
            small(DIFF_DK), small(DIFF_DK), small(DIFF_DK), small(DIFF_DK),
            small(2 * DIFF_DK),
        ],
        out_specs=pl.BlockSpec((None, t, 128), lambda bi, h, i: (bi, i, h)),
        out_shape=jax.ShapeDtypeStruct((b, s, DIFF_HEADS * 128), BF16),
        scratch_shapes=[pltpu.VMEM((t, t), F32)] * 2 + _softmax_scratch(t) * 2,
        compiler_params=_params(3),
        name="diff_attn",
    )(dq, dk, dv, slopes, vec(lq1), vec(lk1), vec(lq2), vec(lk2), vec(g_sub))


def _mla_attn_kernel(q_ref, k_ref, vt_ref, o_ref, mask_diag, *scratch, t):
    i = pl.program_id(2)
    sm = _Softmax(*scratch)

    @pl.when(i == 0)
    def _():
        key, qry = _key_query_iotas(t)
        mask_diag[...] = jnp.where((key >> CHUNK_SHIFT) <= (qry >> CHUNK_SHIFT), 0.0, NEG_BIG)

    def scores(j):
        return (_dot_nt(k_ref[pl.ds(pl.multiple_of(j * t, t), t), :], q_ref[...]),)

    scale = MLA_SCALE * LOG2E
    no_shift = jnp.zeros((1, 1), F32)
    _run_key_blocks(i, (sm,), scores, vt_ref, lambda j: (scale, None, no_shift), (scale, mask_diag[...], no_shift))
    o_ref[...] = (sm.acc[...] / sm.l[...]).T.astype(o_ref.dtype)


def _mla_attn(mq, mk, mv, *, t):
    b, s, _ = mq.shape
    return pl.pallas_call(
        functools.partial(_mla_attn_kernel, t=t),
        grid=(b, MLA_HEADS, s // t),
        in_specs=[
            pl.BlockSpec((None, t, 256), lambda bi, h, i: (bi, i, h)),
            pl.BlockSpec((None, s, 256), lambda bi, h, i: (bi, 0, h)),
            pl.BlockSpec((None, s // t, 128, t), lambda bi, h, i: (bi, 0, h, 0)),
        ],
        out_specs=pl.BlockSpec((None, t, 128), lambda bi, h, i: (bi, i, h)),
        out_shape=jax.ShapeDtypeStruct((b, s, MLA_HEADS * 128), BF16),
        scratch_shapes=[pltpu.VMEM((t, t), F32)] + _softmax_scratch(t),
        compiler_params=_params(3),
        name="mla_attn",
    )(mq, mk, mv)


def _sb_attn_kernel(q_ref, k_ref, v_ref, o_ref, carry_ref, acc_ref, *, t, pairs):
    i = pl.program_id(2)
    heads = 2 * pairs
    low = lax.broadcasted_iota(jnp.int32, (t, LANES), 1) < SB_DH
    q_heads = []
    for p in range(pairs):
        q = q_ref[:, p * LANES:(p + 1) * LANES]
        zero = jnp.zeros_like(q)
        q_heads += [jnp.where(low, q, zero), jnp.where(low, zero, q)]
    row, col = _key_query_iotas(t)
    later = (row > col).astype(BF16)
    strict = col < row

    def block(ref, j, h):
        return ref[pl.ds(pl.multiple_of(j * t, t), t), (h // 2) * LANES:(h // 2 + 1) * LANES]

    def keep_sums(z, mask):
        neg_z = -z
        log_keep = jnp.minimum(neg_z, 0.0) - jnp.log(1.0 + jnp.exp(jnp.minimum(z, neg_z)))
        if mask is not None:
            log_keep = jnp.where(mask, log_keep, 0.0)
        hi = log_keep.astype(BF16)
        lo = (log_keep - hi.astype(F32)).astype(BF16)
        sums = _dot(jnp.concatenate([hi, lo], axis=0), later)
        return z + log_keep, jnp.sum(log_keep, axis=-1, keepdims=True), sums[:t] + sums[t:]

    def weights(log_beta, sums, carry, mask):
        between = sums if carry is None else sums + carry
        a = jnp.exp(log_beta + between)
        if mask is not None:
            a = jnp.where(mask, a, 0.0)
        return a.astype(BF16)

    def near(with_previous):
        z_diag = [_dot_nt(q_heads[h], block(k_ref, i, h)) for h in range(heads)]
        if with_previous:
            z_prev = [_dot_nt(q_heads[h], block(k_ref, i - 1, h)) for h in range(heads)]
        staged = []
        for h in range(heads):
            stage = [keep_sums(z_diag[h], strict)]
            if with_previous:
                stage.append(keep_sums(z_prev[h], None))
            staged.append(stage)
        for h in range(heads):
            log_beta, total, sums = staged[h][0]
            pv = _dot(weights(log_beta, sums, None, strict), block(v_ref, i, h))
            carry = total
            if with_previous:
                log_beta, total_prev, sums = staged[h][1]
                pv = pv + _dot(weights(log_beta, sums, carry, None), block(v_ref, i - 1, h))
                carry = carry + total_prev
            acc_ref[h] = pv
            carry_ref[h] = carry

    def alive():
        return jnp.max(carry_ref[...]) > EXP_UNDERFLOW

    def far(state):
        j = i - 2 - state[0]
        z = [_dot_nt(q_heads[h], block(k_ref, j, h)) for h in range(heads)]
        staged = [keep_sums(z[h], None) for h in range(heads)]
        for h in range(heads):
            log_beta, total, sums = staged[h]
            acc_ref[h] += _dot(weights(log_beta, sums, carry_ref[h], None), block(v_ref, j, h))
            carry_ref[h] += total
        return state[0] + 1, alive()

    @pl.when(i == 0)
    def _():
        near(False)

    @pl.when(i > 0)
    def _():
        near(True)
        lax.while_loop(lambda state: jnp.logical_and(state[0] < i - 1, state[1]), far, (jnp.int32(0), alive()))

    for p in range(pairs):
        o_ref[:, p * LANES:(p + 1) * LANES] = jnp.where(low, acc_ref[2 * p], acc_ref[2 * p + 1]).astype(o_ref.dtype)


def _sb_attn(qkv, *, t, pairs):
    b, s, _ = qkv.shape
    groups = SB_HEADS * SB_DH // (LANES * pairs)
    width = LANES * pairs
    return pl.pallas_call(
        functools.partial(_sb_attn_kernel, t=t, pairs=pairs),
        grid=(b, groups, s // t),
        in_specs=[
            pl.BlockSpec((None, t, width), lambda bi, g, i: (bi, i, g)),
            pl.BlockSpec((None, s, width), lambda bi, g, i: (bi, 0, groups + g)),
            pl.BlockSpec((None, s, width), lambda bi, g, i: (bi, 0, 2 * groups + g)),
        ],
        out_specs=pl.BlockSpec((None, t, width), lambda bi, g, i: (bi, i, g)),
        out_shape=jax.ShapeDtypeStruct((b, s, SB_HEADS * SB_DH), BF16),
        scratch_shapes=[pltpu.VMEM((2 * pairs, t, 1), F32), pltpu.VMEM((2 * pairs, t, LANES), F32)],
        compiler_params=_params(3),
        name="sb_attn",
    )(qkv, qkv, qkv)


def _cross_kernel(h_ref, mix_ref, wmix_ref, g_ref, wq_ref, kv_ref, wo_ref, out_ref, o_scr):
    d = h_ref.shape[1]
    dh = d // XA_HEADS
    h1 = h_ref[...] + _dot(mix_ref[...], wmix_ref[...])
    q = _dot(_rms(h1, g_ref[...]).astype(BF16), wq_ref[...]).astype(BF16)
    for hd in range(XA_HEADS):
        s = _dot_nt(q[:, hd * dh:(hd + 1) * dh], kv_ref[:, hd * dh:(hd + 1) * dh])
        p = jnp.exp(s - jnp.max(s, axis=-1, keepdims=True))
        o = _dot(p.astype(BF16), kv_ref[:, d + hd * dh:d + (hd + 1) * dh]) / jnp.sum(p, axis=-1, keepdims=True)
        o_scr[:, hd * dh:(hd + 1) * dh] = o.astype(BF16)
    out_ref[...] = h1 + _dot(o_scr[...], wo_ref[...])


def _cross(h, mix, w_mix, g, wq, mem_kv, wo, *, seq, tm):
    n, d = h.shape
    blocks = seq // tm
    mem_len = mem_kv.shape[1]
    dh = d // XA_HEADS
    return pl.pallas_call(
        _cross_kernel,
        grid=(n // tm,),
        in_specs=[
            pl.BlockSpec((tm, d), lambda i: (i, 0)),
            pl.BlockSpec((tm, mix.shape[1]), lambda i: (i, 0)),
            _resident(w_mix.shape),
            _resident((1, d)),
            _resident(wq.shape),
            pl.BlockSpec((None, mem_len, 2 * d), lambda i: (i // blocks, 0, 0)),
            _resident(wo.shape),
        ],
        out_specs=pl.BlockSpec((tm, d), lambda i: (i, 0)),
        out_shape=jax.ShapeDtypeStruct((n, d), F32),
        scratch_shapes=[pltpu.VMEM((tm, d), BF16)],
        compiler_params=_params(1),
        name="cross_attn",
    )(h, mix, w_mix.astype(BF16), g.reshape(1, -1), (wq * dh ** -0.5).astype(BF16), mem_kv, wo.astype(BF16))


def _mlp_kernel(h_ref, g_ref, w1_ref, w2_ref, gf_ref, out_ref, *, tf, final_norm):
    x = h_ref[...]
    xn = _rms(x, g_ref[...]).astype(BF16)
    acc = x
    for c in range(w1_ref.shape[1] // tf):
        a = jnp.maximum(_dot(xn, w1_ref[:, c * tf:(c + 1) * tf]), 0.0)
        acc = acc + _dot((a * a).astype(BF16), w2_ref[c * tf:(c + 1) * tf, :])
    if final_norm:
        acc = _rms(acc, gf_ref[...])
    out_ref[...] = acc


def _mlp(h, g, w1, w2, g_final, *, tm, tf, final_norm):
    n, d = h.shape
    return pl.pallas_call(
        functools.partial(_mlp_kernel, tf=tf, final_norm=final_norm),
        grid=(n // tm,),
        in_specs=[
            pl.BlockSpec((tm, d), lambda i: (i, 0)),
            _resident((1, d)),
            _resident(w1.shape),
            _resident(w2.shape),
            _resident((1, d)),
        ],
        out_specs=pl.BlockSpec((tm, d), lambda i: (i, 0)),
        out_shape=jax.ShapeDtypeStruct((n, d), F32),
        compiler_params=_params(1),
        name="mlp",
    )(h, g.reshape(1, -1), w1.astype(BF16), w2.astype(BF16), g_final.reshape(1, -1))


def _rope_tables(seq):
    half = MLA_ROPE // 2
    inv = ROPE_BASE ** (-jnp.arange(half, dtype=F32) / half)
    ang = jnp.arange(seq).astype(F32)[:, None] * inv[None, :]
    pad = lambda a: jnp.concatenate([a, a, jnp.zeros((seq, LANES - 2 * half), F32)], axis=1)
    return pad(jnp.cos(ang)), pad(jnp.sin(ang))


def kernel(x, mem, ev_norm, ev_w_in, diff_lq1, diff_lk1, diff_lq2, diff_lk2, diff_subln, mla_g_cq, mla_w_uq, mla_g_ckv, mla_w_ukv, ev_w_out, od_norm, sb_w_in, sb_w_out, xa_norm, xa_mem_norm, xa_wq, xa_wkv, xa_wo, mlp_norm, mlp_w1, mlp_w2, final_norm):
    b, seq, d = x.shape
    mem_len = mem.shape[1]
    depth = xa_norm.shape[0]
    tm = 512
    h = x.reshape(b * seq, d)
    mem2 = mem.reshape(b * mem_len, d)
    cos_pad, sin_pad = _rope_tables(seq)
    for i in range(depth):
        j = i // 2
        if i % 2 == 0:
            lambda_init = 0.8 - 0.6 * math.exp(-0.3 * i)
            dq, dk, dv, mq, mk, mv = _even_proj(
                h, ev_norm[j], ev_w_in[j], mla_g_cq[j], mla_w_uq[j], mla_g_ckv[j], mla_w_ukv[j],
                cos_pad, sin_pad, seq=seq, tm=tm)
            split = lambda a: a.reshape(b, seq, a.shape[-1])
            o_a = _diff_attn(split(dq), split(dk), dv, diff_lq1[j], diff_lk1[j], diff_lq2[j], diff_lk2[j],
                             diff_subln[j], lambda_init, t=tm)
            o_b = _mla_attn(split(mq), split(mk), mv, t=tm)
            mix = jnp.concatenate([o_a, o_b], axis=-1).reshape(b * seq, -1)
            w_mix = ev_w_out[j]
        else:
            w_in = sb_w_in[j]
            width = SB_HEADS * SB_DH
            w_in = jnp.concatenate([w_in[:, :width] * (SB_DH ** -0.5), w_in[:, width:]], axis=1).astype(BF16)
            qkv = _norm_matmul(h, od_norm[j].reshape(1, -1), w_in, tm=tm, name="sb_proj")
            mix = _sb_attn(qkv.reshape(b, seq, -1), t=256, pairs=2).reshape(b * seq, -1)
            w_mix = sb_w_out[j]
        mem_kv = _norm_matmul(mem2, xa_mem_norm[i].reshape(1, -1), xa_wkv[i].astype(BF16), tm=mem_len, name="mem_kv")
        h = _cross(h, mix, w_mix, xa_norm[i], xa_wq[i], mem_kv.reshape(b, mem_len, -1), xa_wo[i], seq=seq, tm=tm)
        h = _mlp(h, mlp_norm[i], mlp_w1[i], mlp_w2[i], final_norm, tm=tm, tf=1024, final_norm=(i == depth - 1))
    return h.reshape(b, seq, d)
```

```python
import functools
import math

import jax
import jax.numpy as jnp
from jax import lax
from jax.experimental import pallas as pl
from jax.experimental.pallas import tpu as pltpu

F32 = jnp.float32
BF16 = jnp.bfloat16

EPS = 1e-6
CHUNK_SHIFT = 6
DIFF_HEADS = 4
DIFF_DK = 64
MLA_HEADS = 4
MLA_NOPE = 128
MLA_ROPE = 64
MLA_SCALE = (MLA_NOPE + MLA_ROPE) ** -0.5
ROPE_BASE = 10000.0
SB_HEADS = 16
SB_DH = 64
XA_HEADS = 4
NEG_BIG = -1e30
LOG2E = math.log2(math.e)
EXP_UNDERFLOW = -104.0
LANES = 128
DV = 128
ONES_ROWS = 16

VMEM_LIMIT_BYTES = 48 * 1024 * 1024


def _params(n_grid):
    return pltpu.CompilerParams(
        dimension_semantics=("arbitrary",) * n_grid, vmem_limit_bytes=VMEM_LIMIT_BYTES
    )


def _resident(shape):
    zeros = (0,) * len(shape)
    return pl.BlockSpec(shape, lambda *_: zeros, pipeline_mode=pl.Buffered(1))


def _rms(x, g):
    ms = jnp.mean(x * x, axis=-1, keepdims=True)
    return x * lax.rsqrt(ms + EPS) * g


def _dot(a, b):
    return jnp.dot(a, b, preferred_element_type=F32)


def _dot_nt(a, b):
    return lax.dot_general(a, b, (((1,), (1,)), ((), ())), preferred_element_type=F32)


def _norm_matmul_kernel(x_ref, g_ref, w_ref, o_ref):
    xn = _rms(x_ref[...], g_ref[...]).astype(BF16)
    o_ref[...] = _dot(xn, w_ref[...]).astype(o_ref.dtype)


def _norm_matmul(x, g, w, *, tm, name):
    n, d = x.shape
    f = w.shape[1]
    return pl.pallas_call(
        _norm_matmul_kernel,
        grid=(n // tm,),
        in_specs=[
            pl.BlockSpec((tm, d), lambda i: (i, 0)),
            _resident((1, d)),
            _resident((d, f)),
        ],
        out_specs=pl.BlockSpec((tm, f), lambda i: (i, 0)),
        out_shape=jax.ShapeDtypeStruct((n, f), BF16),
        compiler_params=_params(1),
        name=name,
    )(x, g, w)


def _even_proj_kernel(h_ref, g_ref, w_ref, gcq_ref, wq_ref, gckv_ref, wkv_ref, cos_ref, sin_ref,
                      dq_ref, dk_ref, dv_ref, mq_ref, mk_ref, mv_ref):
    tm = h_ref.shape[0]
    xn = _rms(h_ref[...], g_ref[...]).astype(BF16)
    proj = _dot(xn, w_ref[...])
    lane = lax.broadcasted_iota(jnp.int32, (tm, LANES), 1)
    low = lane < DIFF_DK
    frame_in_block = lax.broadcasted_iota(jnp.int32, (tm, LANES), 0).astype(F32)
    ones = jnp.where(lane < DIFF_DK + 3, 1.0, 0.0)
    for h in range(DIFF_HEADS):
        b = frame_in_block * _alibi_slope(h)
        b_hi = b.astype(BF16).astype(F32)
        b_mid = (b - b_hi).astype(BF16).astype(F32)
        bias = jnp.where(lane == DIFF_DK, b_hi, jnp.where(lane == DIFF_DK + 1, b_mid, b - b_hi - b_mid))
        bias = jnp.where(lane < DIFF_DK + 3, bias, 0.0)
        q_pair = proj[:, h * 128:(h + 1) * 128]
        k_pair = proj[:, 512 + h * 128:512 + (h + 1) * 128]
        for half, (q, k) in enumerate(((q_pair, k_pair), (pltpu.roll(q_pair, DIFF_DK, 1), pltpu.roll(k_pair, DIFF_DK, 1)))):
            cols = slice(h * 256 + half * 128, h * 256 + (half + 1) * 128)
            dq_ref[:, cols] = jnp.where(low, q, ones).astype(BF16)
            dk_ref[:, cols] = jnp.where(low, k, bias).astype(BF16)
    dv_ref[...] = proj[:, 1024:1536].T.astype(BF16)

    cos = cos_ref[...]
    sin = sin_ref[...]
    k_rope = (proj[:, 1920:2048] * cos + proj[:, 2048:2176] * sin).astype(BF16)

    cqn = _rms(proj[:, 1536:1792], gcq_ref[...]).astype(BF16)
    qm = _dot(cqn, wq_ref[...])
    ckvn = _rms(proj[:, 1792:1920], gckv_ref[...]).astype(BF16)
    kv = _dot(ckvn, wkv_ref[...])
    for h in range(MLA_HEADS):
        mq_ref[:, h * 256:h * 256 + 128] = qm[:, h * 128:(h + 1) * 128].astype(BF16)
        q_rope = qm[:, 512 + h * 128:512 + (h + 1) * 128] * cos + qm[:, 1024 + h * 128:1024 + (h + 1) * 128] * sin
        mq_ref[:, h * 256 + 128:(h + 1) * 256] = q_rope.astype(BF16)
        mk_ref[:, h * 256:h * 256 + 128] = kv[:, h * 256:h * 256 + 128].astype(BF16)
        mk_ref[:, h * 256 + 128:(h + 1) * 256] = k_rope
        mv_ref[h * 128:(h + 1) * 128, :] = kv[:, h * 256 + 128:(h + 1) * 256].T.astype(BF16)


def _alibi_slope(h):
    return 2.0 ** (-8.0 * (h + 1) / DIFF_HEADS)


def _rotated(w):
    half = w.shape[1] // 2
    return jnp.concatenate([-w[:, half:], w[:, :half]], axis=1)


def _pad_cols(w, width):
    return jnp.pad(w, ((0, 0), (0, width - w.shape[1])))


def _even_proj(h, g, w_in, g_cq, w_uq, g_ckv, w_ukv, cos_pad, sin_pad, *, seq, tm):
    n, d = h.shape
    kr = w_in[:, 1920:1984]
    w_ext = jnp.concatenate(
        [w_in[:, :512] * (DIFF_DK ** -0.5), w_in[:, 512:1920], _pad_cols(kr, 128), _pad_cols(_rotated(kr), 128)],
        axis=1).astype(BF16)
    per_head = MLA_NOPE + MLA_ROPE
    nope = [w_uq[:, hh * per_head:hh * per_head + MLA_NOPE] for hh in range(MLA_HEADS)]
    rope = [w_uq[:, hh * per_head + MLA_NOPE:(hh + 1) * per_head] for hh in range(MLA_HEADS)]
    wq_ext = jnp.concatenate(
        nope + [_pad_cols(r, 128) for r in rope] + [_pad_cols(_rotated(r), 128) for r in rope], axis=1).astype(BF16)
    pos_blocks = seq // tm
    bf = lambda width: jax.ShapeDtypeStruct((n, width), BF16)
    row = lambda width: pl.BlockSpec((tm, width), lambda i: (i, 0))
    vt_shape = jax.ShapeDtypeStruct((n // seq, pos_blocks, 512, tm), BF16)
    vt_spec = pl.BlockSpec((None, None, 512, tm), lambda i: (i // pos_blocks, i % pos_blocks, 0, 0))
    return pl.pallas_call(
        _even_proj_kernel,
        grid=(n // tm,),
        in_specs=[
            row(d),
            _resident((1, d)),
            _resident(w_ext.shape),
            _resident((1, g_cq.shape[-1])),
            _resident(wq_ext.shape),
            _resident((1, g_ckv.shape[-1])),
            _resident(w_ukv.shape),
            pl.BlockSpec((tm, 128), lambda i: (i % pos_blocks, 0)),
            pl.BlockSpec((tm, 128), lambda i: (i % pos_blocks, 0)),
        ],
        out_specs=[row(1024), row(1024), vt_spec, row(1024), row(1024), vt_spec],
        out_shape=[bf(1024), bf(1024), vt_shape, bf(1024), bf(1024), vt_shape],
        compiler_params=_params(1),
        name="even_proj",
    )(h, g.reshape(1, -1), w_ext, g_cq.reshape(1, -1), wq_ext, g_ckv.reshape(1, -1), w_ukv.astype(BF16),
      cos_pad, sin_pad)


class _Softmax:
    def __init__(self, s0, s1, p, a, m, acc):
        self.s, self.p, self.a, self.m, self.acc = (s0, s1), p, a, m, acc

    def reset(self):
        self.m[...] = jnp.full(self.m.shape, NEG_BIG, F32)
        self.acc[...] = jnp.zeros(self.acc.shape, F32)
        self.p[...] = jnp.zeros(self.p.shape, BF16)
        self.a[...] = jnp.ones(self.a.shape, F32)

    def weigh(self, cur, scale, bias, shift):
        m_prev = self.m[...]
        if bias is None:
            m_block = jnp.max(self.s[cur][...], axis=0, keepdims=True) * scale
            m_new = jnp.maximum(m_prev, m_block if shift is None else m_block + shift)
            p = jnp.exp2(self.s[cur][...] * scale - (m_new if shift is None else m_new - shift))
        else:
            logits = self.s[cur][...] * scale + bias
            m_block = jnp.max(logits, axis=0, keepdims=True)
            m_new = jnp.maximum(m_prev, m_block if shift is None else m_block + shift)
            p = jnp.exp2(logits - (m_new if shift is None else m_new - shift))
        self.m[...] = m_new
        self.p[...] = p.astype(BF16)
        self.a[...] = jnp.exp2(m_prev - m_new)

    def start_values(self, vt):
        vt_ones = jnp.concatenate([vt, jnp.ones((ONES_ROWS, vt.shape[1]), BF16)], axis=0)
        return self.a[...], _dot(vt_ones, self.p[...])

    def fold(self, alpha, pv):
        self.acc[...] = alpha * self.acc[...] + pv

    def normalized(self):
        return self.acc[:DV, :] / self.acc[DV:DV + 1, :]


def _run_key_blocks(i, sms, scores, q_ref, q_next_ref, vt_ref, scale, diagonal_bias, block_shift):
    @pl.when(i == 0)
    def _():
        for sm, s in zip(sms, scores(q_ref, 0)):
            sm.s[0][...] = s

    for sm in sms:
        sm.reset()

    def iteration(j, cur):
        vt = vt_ref[jnp.maximum(j - 1, 0)]
        next_scores = scores(q_ref, j + 1)
        scores_first = cur == 1 and len(sms) == 1
        if scores_first:
            sms[0].s[1 - cur][...] = next(next_scores)
        pending = [sm.start_values(vt) for sm in sms]
        if not scores_first:
            for sm in sms:
                sm.s[1 - cur][...] = next(next_scores)
        for sm in sms:
            sm.weigh(cur, scale, None, block_shift(j))
        for sm, pend in zip(sms, pending):
            sm.fold(*pend)

    def pair(jj, carry):
        iteration(2 * jj, 0)
        iteration(2 * jj + 1, 1)
        return carry

    lax.fori_loop(0, i >> 1, pair, 0)

    @pl.when((i & 1) == 1)
    def _():
        iteration(i - 1, 0)
        for sm in sms:
            sm.s[0][...] = sm.s[1][...]

    vt = vt_ref[jnp.maximum(i - 1, 0)]
    pending = [sm.start_values(vt) for sm in sms]
    for sm in sms:
        sm.weigh(0, scale, diagonal_bias, block_shift(i))
    last = [sm.start_values(vt_ref[i]) for sm in sms]
    for sm, s in zip(sms, scores(q_next_ref, 0)):
        sm.s[0][...] = s
    for sm, pend in zip(sms, pending):
        sm.fold(*pend)
    for sm, pend in zip(sms, last):
        sm.fold(*pend)


def _next_query_spec(blocks, t):
    return pl.BlockSpec((None, t, 256), lambda bi, h, i: (bi, jnp.minimum(i + 1, blocks - 1), h))


def _softmax_scratch(t):
    return [pltpu.VMEM((t, t), F32), pltpu.VMEM((t, t), F32), pltpu.VMEM((t, t), BF16), pltpu.VMEM((1, t), F32),
            pltpu.VMEM((1, t), F32), pltpu.VMEM((DV + ONES_ROWS, t), F32)]


def _key_query_iotas(t):
    return lax.broadcasted_iota(jnp.int32, (t, t), 0), lax.broadcasted_iota(jnp.int32, (t, t), 1)


def _diff_attn_kernel(q_ref, q_next_ref, k_ref, vt_ref, slope_ref, lq1_ref, lk1_ref, lq2_ref, lk2_ref, gsub_ref,
                      o_ref, bias_diag, *scratch, t, lambda_init):
    i = pl.program_id(2)
    sm1 = _Softmax(*scratch[:6])
    sm2 = _Softmax(*scratch[6:])

    @pl.when(i == 0)
    def _():
        key, qry = _key_query_iotas(t)
        ahead = jnp.maximum(key - qry, 0).astype(F32)
        allowed = (key >> CHUNK_SHIFT) <= (qry >> CHUNK_SHIFT)
        bias_diag[...] = jnp.where(allowed, (-2.0 * LOG2E) * slope_ref[:, :1] * ahead, NEG_BIG)

    def scores(q, j):
        k = k_ref[pl.ds(pl.multiple_of(j * t, t), t), :]
        yield _dot_nt(k[:, :128], q[:, :128])
        yield _dot_nt(k[:, 128:], q[:, 128:])

    def block_shift(j):
        return (LOG2E * t) * slope_ref[:, :1] * j.astype(F32)

    _run_key_blocks(i, (sm1, sm2), scores, q_ref, q_next_ref, vt_ref, LOG2E, bias_diag[...], block_shift)

    lam = (jnp.exp(jnp.sum(lq1_ref[...] * lk1_ref[...], axis=-1, keepdims=True))
           - jnp.exp(jnp.sum(lq2_ref[...] * lk2_ref[...], axis=-1, keepdims=True)) + lambda_init)
    o = (sm1.normalized() - lam * sm2.normalized()).T
    o_ref[...] = (_rms(o, gsub_ref[...]) * (1.0 - lambda_init)).astype(o_ref.dtype)


def _diff_attn(dq, dk, dv, lq1, lk1, lq2, lk2, g_sub, lambda_init, *, t):
    b, s, _ = dq.shape
    slopes = 2.0 ** (-8.0 * jnp.arange(1, DIFF_HEADS + 1, dtype=F32) / DIFF_HEADS)
    slopes = jnp.broadcast_to(slopes[:, None, None], (DIFF_HEADS, 1, LANES))
    vec = lambda a: a.reshape(1, -1)
    small = lambda width: pl.BlockSpec((1, width), lambda bi, h, i: (0, 0))
    kern = functools.partial(_diff_attn_kernel, t=t, lambda_init=lambda_init)
    return pl.pallas_call(
        kern,
        grid=(b, DIFF_HEADS, s // t),
        in_specs=[
            pl.BlockSpec((None, t, 256), lambda bi, h, i: (bi, i, h)),
            _next_query_spec(s // t, t),
            pl.BlockSpec((None, s, 256), lambda bi, h, i: (bi, 0, h)),
            pl.BlockSpec((None, s // t, 128, t), lambda bi, h, i: (bi, 0, h, 0)),
            pl.BlockSpec((None, 1, LANES), lambda bi, h, i: (h, 0, 0)),
            small(DIFF_DK), small(DIFF_DK), small(DIFF_DK), small(DIFF_DK),
            small(2 * DIFF_DK),
        ],
        out_specs=pl.BlockSpec((None, t, 128), lambda bi, h, i: (bi, i, h)),
        out_shape=jax.ShapeDtypeStruct((b, s, DIFF_HEADS * 128), BF16),
        scratch_shapes=[pltpu.VMEM((t, t), F32)] + _softmax_scratch(t) * 2,
        compiler_params=_params(3),
        name="diff_attn",
    )(dq, dq, dk, dv, slopes, vec(lq1), vec(lk1), vec(lq2), vec(lk2), vec(g_sub))


def _mla_attn_kernel(q_ref, q_next_ref, k_ref, vt_ref, o_ref, mask_diag, *scratch, t):
    i = pl.program_id(2)
    sm = _Softmax(*scratch)

    @pl.when(i == 0)
    def _():
        key, qry = _key_query_iotas(t)
        mask_diag[...] = jnp.where((key >> CHUNK_SHIFT) <= (qry >> CHUNK_SHIFT), 0.0, NEG_BIG)

    def scores(q, j):
        yield _dot_nt(k_ref[pl.ds(pl.multiple_of(j * t, t), t), :], q[...])

    _run_key_blocks(i, (sm,), scores, q_ref, q_next_ref, vt_ref, MLA_SCALE * LOG2E, mask_diag[...], lambda j: None)
    o_ref[...] = sm.normalized().T.astype(o_ref.dtype)


def _mla_attn(mq, mk, mv, *, t):
    b, s, _ = mq.shape
    return pl.pallas_call(
        functools.partial(_mla_attn_kernel, t=t),
        grid=(b, MLA_HEADS, s // t),
        in_specs=[
            pl.BlockSpec((None, t, 256), lambda bi, h, i: (bi, i, h)),
            _next_query_spec(s // t, t),
            pl.BlockSpec((None, s, 256), lambda bi, h, i: (bi, 0, h)),
            pl.BlockSpec((None, s // t, 128, t), lambda bi, h, i: (bi, 0, h, 0)),
        ],
        out_specs=pl.BlockSpec((None, t, 128), lambda bi, h, i: (bi, i, h)),
        out_shape=jax.ShapeDtypeStruct((b, s, MLA_HEADS * 128), BF16),
        scratch_shapes=[pltpu.VMEM((t, t), F32)] + _softmax_scratch(t),
        compiler_params=_params(3),
        name="mla_attn",
    )(mq, mq, mk, mv)


def _sb_attn_kernel(q_ref, k_ref, v_ref, o_ref, carry_ref, acc_ref, *, t, pairs):
    i = pl.program_id(2)
    heads = 2 * pairs
    low = lax.broadcasted_iota(jnp.int32, (t, LANES), 1) < SB_DH
    q_heads = []
    for p in range(pairs):
        q = q_ref[:, p * LANES:(p + 1) * LANES]
        zero = jnp.zeros_like(q)
        q_heads += [jnp.where(low, q, zero), jnp.where(low, zero, q)]
    row, col = _key_query_iotas(t)
    later = (row > col).astype(BF16)
    strict = col < row

    def block(ref, j, h):
        return ref[pl.ds(pl.multiple_of(j * t, t), t), (h // 2) * LANES:(h // 2 + 1) * LANES]

    def keep_sums(z, mask):
        neg_z = -z
        log_keep = jnp.minimum(neg_z, 0.0) - jnp.log(1.0 + jnp.exp(jnp.minimum(z, neg_z)))
        if mask is not None:
            log_keep = jnp.where(mask, log_keep, 0.0)
        hi = log_keep.astype(BF16)
        lo = (log_keep - hi.astype(F32)).astype(BF16)
        sums = _dot(jnp.concatenate([hi, lo], axis=0), later)
        return z + log_keep, jnp.sum(log_keep, axis=-1, keepdims=True), sums[:t] + sums[t:]

    def weights(log_beta, sums, carry, mask):
        between = sums if carry is None else sums + carry
        a = jnp.exp(log_beta + between)
        if mask is not None:
            a = jnp.where(mask, a, 0.0)
        return a.astype(BF16)

    def near(with_previous):
        z_diag = [_dot_nt(q_heads[h], block(k_ref, i, h)) for h in range(heads)]
        if with_previous:
            z_prev = [_dot_nt(q_heads[h], block(k_ref, i - 1, h)) for h in range(heads)]
        staged = []
        for h in range(heads):
            stage = [keep_sums(z_diag[h], strict)]
            if with_previous:
                stage.append(keep_sums(z_prev[h], None))
            staged.append(stage)
        for h in range(heads):
            log_beta, total, sums = staged[h][0]
            pv = _dot(weights(log_beta, sums, None, strict), block(v_ref, i, h))
            carry = total
            if with_previous:
                log_beta, total_prev, sums = staged[h][1]
                pv = pv + _dot(weights(log_beta, sums, carry, None), block(v_ref, i - 1, h))
                carry = carry + total_prev
            acc_ref[h] = pv
            carry_ref[h] = carry

    def alive():
        return jnp.max(carry_ref[...]) > EXP_UNDERFLOW

    def far(state):
        j = i - 2 - state[0]
        z = [_dot_nt(q_heads[h], block(k_ref, j, h)) for h in range(heads)]
        staged = [keep_sums(z[h], None) for h in range(heads)]
        for h in range(heads):
            log_beta, total, sums = staged[h]
            acc_ref[h] += _dot(weights(log_beta, sums, carry_ref[h], None), block(v_ref, j, h))
            carry_ref[h] += total
        return state[0] + 1, alive()

    @pl.when(i == 0)
    def _():
        near(False)

    @pl.when(i > 0)
    def _():
        near(True)
        lax.while_loop(lambda state: jnp.logical_and(state[0] < i - 1, state[1]), far, (jnp.int32(0), alive()))

    for p in range(pairs):
        o_ref[:, p * LANES:(p + 1) * LANES] = jnp.where(low, acc_ref[2 * p], acc_ref[2 * p + 1]).astype(o_ref.dtype)


def _sb_attn(qkv, *, t, pairs):
    b, s, _ = qkv.shape
    groups = SB_HEADS * SB_DH // (LANES * pairs)
    width = LANES * pairs
    return pl.pallas_call(
        functools.partial(_sb_attn_kernel, t=t, pairs=pairs),
        grid=(b, groups, s // t),
        in_specs=[
            pl.BlockSpec((None, t, width), lambda bi, g, i: (bi, i, g)),
            pl.BlockSpec((None, s, width), lambda bi, g, i: (bi, 0, groups + g)),
            pl.BlockSpec((None, s, width), lambda bi, g, i: (bi, 0, 2 * groups + g)),
        ],
        out_specs=pl.BlockSpec((None, t, width), lambda bi, g, i: (bi, i, g)),
        out_shape=jax.ShapeDtypeStruct((b, s, SB_HEADS * SB_DH), BF16),
        scratch_shapes=[pltpu.VMEM((2 * pairs, t, 1), F32), pltpu.VMEM((2 * pairs, t, LANES), F32)],
        compiler_params=_params(3),
        name="sb_attn",
    )(qkv, qkv, qkv)


def _cross_kernel(h_ref, mix_ref, wmix_ref, g_ref, wq_ref, kv_ref, wo_ref, out_ref, o_scr):
    d = h_ref.shape[1]
    dh = d // XA_HEADS
    h1 = h_ref[...] + _dot(mix_ref[...], wmix_ref[...])
    q = _dot(_rms(h1, g_ref[...]).astype(BF16), wq_ref[...]).astype(BF16)
    for hd in range(XA_HEADS):
        s = _dot_nt(q[:, hd * dh:(hd + 1) * dh], kv_ref[:, hd * dh:(hd + 1) * dh])
        p = jnp.exp(s - jnp.max(s, axis=-1, keepdims=True))
        o = _dot(p.astype(BF16), kv_ref[:, d + hd * dh:d + (hd + 1) * dh]) / jnp.sum(p, axis=-1, keepdims=True)
        o_scr[:, hd * dh:(hd + 1) * dh] = o.astype(BF16)
    out_ref[...] = h1 + _dot(o_scr[...], wo_ref[...])


def _cross(h, mix, w_mix, g, wq, mem_kv, wo, *, seq, tm):
    n, d = h.shape
    blocks = seq // tm
    mem_len = mem_kv.shape[1]
    dh = d // XA_HEADS
    return pl.pallas_call(
        _cross_kernel,
        grid=(n // tm,),
        in_specs=[
            pl.BlockSpec((tm, d), lambda i: (i, 0)),
            pl.BlockSpec((tm, mix.shape[1]), lambda i: (i, 0)),
            _resident(w_mix.shape),
            _resident((1, d)),
            _resident(wq.shape),
            pl.BlockSpec((None, mem_len, 2 * d), lambda i: (i // blocks, 0, 0)),
            _resident(wo.shape),
        ],
        out_specs=pl.BlockSpec((tm, d), lambda i: (i, 0)),
        out_shape=jax.ShapeDtypeStruct((n, d), F32),
        scratch_shapes=[pltpu.VMEM((tm, d), BF16)],
        compiler_params=_params(1),
        name="cross_attn",
    )(h, mix, w_mix.astype(BF16), g.reshape(1, -1), (wq * dh ** -0.5).astype(BF16), mem_kv, wo.astype(BF16))


def _mlp_kernel(h_ref, g_ref, w1_ref, w2_ref, gf_ref, out_ref, *, tf, final_norm):
    x = h_ref[...]
    xn = _rms(x, g_ref[...]).astype(BF16)
    acc = x
    for c in range(w1_ref.shape[1] // tf):
        a = jnp.maximum(_dot(xn, w1_ref[:, c * tf:(c + 1) * tf]), 0.0)
        acc = acc + _dot((a * a).astype(BF16), w2_ref[c * tf:(c + 1) * tf, :])
    if final_norm:
        acc = _rms(acc, gf_ref[...])
    out_ref[...] = acc


def _mlp(h, g, w1, w2, g_final, *, tm, tf, final_norm):
    n, d = h.shape
    return pl.pallas_call(
        functools.partial(_mlp_kernel, tf=tf, final_norm=final_norm),
        grid=(n // tm,),
        in_specs=[
            pl.BlockSpec((tm, d), lambda i: (i, 0)),
            _resident((1, d)),
            _resident(w1.shape),
            _resident(w2.shape),
            _resident((1, d)),
        ],
        out_specs=pl.BlockSpec((tm, d), lambda i: (i, 0)),
        out_shape=jax.ShapeDtypeStruct((n, d), F32),
        compiler_params=_params(1),
        name="mlp",
    )(h, g.reshape(1, -1), w1.astype(BF16), w2.astype(BF16), g_final.reshape(1, -1))


def _rope_tables(seq):
    half = MLA_ROPE // 2
    inv = ROPE_BASE ** (-jnp.arange(half, dtype=F32) / half)
    ang = jnp.arange(seq).astype(F32)[:, None] * inv[None, :]
    pad = lambda a: jnp.concatenate([a, a, jnp.zeros((seq, LANES - 2 * half), F32)], axis=1)
    return pad(jnp.cos(ang)), pad(jnp.sin(ang))


def kernel(x, mem, ev_norm, ev_w_in, diff_lq1, diff_lk1, diff_lq2, diff_lk2, diff_subln, mla_g_cq, mla_w_uq, mla_g_ckv, mla_w_ukv, ev_w_out, od_norm, sb_w_in, sb_w_out, xa_norm, xa_mem_norm, xa_wq, xa_wkv, xa_wo, mlp_norm, mlp_w1, mlp_w2, final_norm):
    b, seq, d = x.shape
    mem_len = mem.shape[1]
    depth = xa_norm.shape[0]
    tm = 512
    h = x.reshape(b * seq, d)
    mem2 = mem.reshape(b * mem_len, d)
    cos_pad, sin_pad = _rope_tables(seq)
    for i in range(depth):
        j = i // 2
        if i % 2 == 0:
            lambda_init = 0.8 - 0.6 * math.exp(-0.3 * i)
            dq, dk, dv, mq, mk, mv = _even_proj(
                h, ev_norm[j], ev_w_in[j], mla_g_cq[j], mla_w_uq[j], mla_g_ckv[j], mla_w_ukv[j],
                cos_pad, sin_pad, seq=seq, tm=tm)
            split = lambda a: a.reshape(b, seq, a.shape[-1])
            o_a = _diff_attn(split(dq), split(dk), dv, diff_lq1[j], diff_lk1[j], diff_lq2[j], diff_lk2[j],
                             diff_subln[j], lambda_init, t=tm)
            o_b = _mla_attn(split(mq), split(mk), mv, t=tm)
            mix = jnp.concatenate([o_a, o_b], axis=-1).reshape(b * seq, -1)
            w_mix = ev_w_out[j]
        else:
            w_in = sb_w_in[j]
            width = SB_HEADS * SB_DH
            w_in = jnp.concatenate([w_in[:, :width] * (SB_DH ** -0.5), w_in[:, width:]], axis=1).astype(BF16)
            qkv = _norm_matmul(h, od_norm[j].reshape(1, -1), w_in, tm=tm, name="sb_proj")
            mix = _sb_attn(qkv.reshape(b, seq, -1), t=256, pairs=2).reshape(b * seq, -1)
            w_mix = sb_w_out[j]
        mem_kv = _norm_matmul(mem2, xa_mem_norm[i].reshape(1, -1), xa_wkv[i].astype(BF16), tm=mem_len, name="mem_kv")
        h = _cross(h, mix, w_mix, xa_norm[i], xa_wq[i], mem_kv.reshape(b, mem_len, -1), xa_wo[i], seq=seq, tm=tm)
        h = _mlp(h, mlp_norm[i], mlp_w1[i], mlp_w2[i], final_norm, tm=tm, tf=1024, final_norm=(i == depth - 1))
    return h.reshape(b, seq, d)
```

```python
import functools
import math

import jax
import jax.numpy as jnp
from jax import lax
from jax.experimental import pallas as pl
from jax.experimental.pallas import tpu as pltpu

F32 = jnp.float32
BF16 = jnp.bfloat16

EPS = 1e-6
CHUNK_SHIFT = 6
DIFF_HEADS = 4
DIFF_DK = 64
MLA_HEADS = 4
MLA_NOPE = 128
MLA_ROPE = 64
MLA_SCALE = (MLA_NOPE + MLA_ROPE) ** -0.5
ROPE_BASE = 10000.0
SB_HEADS = 16
SB_DH = 64
XA_HEADS = 4
NEG_BIG = -1e30
LOG2E = math.log2(math.e)
EXP_UNDERFLOW = -104.0
LANES = 128
DV = 128
ONES_ROWS = 16

VMEM_LIMIT_BYTES = 48 * 1024 * 1024


def _params(n_grid):
    return pltpu.CompilerParams(
        dimension_semantics=("arbitrary",) * n_grid, vmem_limit_bytes=VMEM_LIMIT_BYTES
    )


def _resident(shape):
    zeros = (0,) * len(shape)
    return pl.BlockSpec(shape, lambda *_: zeros, pipeline_mode=pl.Buffered(1))


def _rms(x, g):
    ms = jnp.mean(x * x, axis=-1, keepdims=True)
    return x * lax.rsqrt(ms + EPS) * g


def _dot(a, b):
    return jnp.dot(a, b, preferred_element_type=F32)


def _dot_nt(a, b):
    return lax.dot_general(a, b, (((1,), (1,)), ((), ())), preferred_element_type=F32)


def _norm_matmul_kernel(x_ref, g_ref, w_ref, o_ref):
    xn = _rms(x_ref[...], g_ref[...]).astype(BF16)
    o_ref[...] = _dot(xn, w_ref[...]).astype(o_ref.dtype)


def _norm_matmul(x, g, w, *, tm, name):
    n, d = x.shape
    f = w.shape[1]
    return pl.pallas_call(
        _norm_matmul_kernel,
        grid=(n // tm,),
        in_specs=[
            pl.BlockSpec((tm, d), lambda i: (i, 0)),
            _resident((1, d)),
            _resident((d, f)),
        ],
        out_specs=pl.BlockSpec((tm, f), lambda i: (i, 0)),
        out_shape=jax.ShapeDtypeStruct((n, f), BF16),
        compiler_params=_params(1),
        name=name,
    )(x, g, w)


def _even_proj_kernel(h_ref, g_ref, w_ref, gcq_ref, wq_ref, gckv_ref, wkv_ref, cos_ref, sin_ref,
                      dq_ref, dk_ref, dv_ref, mq_ref, mk_ref, mv_ref):
    tm = h_ref.shape[0]
    xn = _rms(h_ref[...], g_ref[...]).astype(BF16)
    proj = _dot(xn, w_ref[...])
    lane = lax.broadcasted_iota(jnp.int32, (tm, LANES), 1)
    low = lane < DIFF_DK
    frame_in_block = lax.broadcasted_iota(jnp.int32, (tm, LANES), 0).astype(F32)
    ones = jnp.where(lane < DIFF_DK + 3, 1.0, 0.0)
    for h in range(DIFF_HEADS):
        b = frame_in_block * _alibi_slope(h)
        b_hi = b.astype(BF16).astype(F32)
        b_mid = (b - b_hi).astype(BF16).astype(F32)
        bias = jnp.where(lane == DIFF_DK, b_hi, jnp.where(lane == DIFF_DK + 1, b_mid, b - b_hi - b_mid))
        bias = jnp.where(lane < DIFF_DK + 3, bias, 0.0)
        q_pair = proj[:, h * 128:(h + 1) * 128]
        k_pair = proj[:, 512 + h * 128:512 + (h + 1) * 128]
        for half, (q, k) in enumerate(((q_pair, k_pair), (pltpu.roll(q_pair, DIFF_DK, 1), pltpu.roll(k_pair, DIFF_DK, 1)))):
            cols = slice(h * 256 + half * 128, h * 256 + (half + 1) * 128)
            dq_ref[:, cols] = jnp.where(low, q, ones).astype(BF16)
            dk_ref[:, cols] = jnp.where(low, k, bias).astype(BF16)
    dv_ref[...] = proj[:, 1024:1536].T.astype(BF16)

    cos = cos_ref[...]
    sin = sin_ref[...]
    k_rope = (proj[:, 1920:2048] * cos + proj[:, 2048:2176] * sin).astype(BF16)

    cqn = _rms(proj[:, 1536:1792], gcq_ref[...]).astype(BF16)
    qm = _dot(cqn, wq_ref[...])
    ckvn = _rms(proj[:, 1792:1920], gckv_ref[...]).astype(BF16)
    kv = _dot(ckvn, wkv_ref[...])
    for h in range(MLA_HEADS):
        mq_ref[:, h * 256:h * 256 + 128] = qm[:, h * 128:(h + 1) * 128].astype(BF16)
        q_rope = qm[:, 512 + h * 128:512 + (h + 1) * 128] * cos + qm[:, 1024 + h * 128:1024 + (h + 1) * 128] * sin
        mq_ref[:, h * 256 + 128:(h + 1) * 256] = q_rope.astype(BF16)
        mk_ref[:, h * 256:h * 256 + 128] = kv[:, h * 256:h * 256 + 128].astype(BF16)
        mk_ref[:, h * 256 + 128:(h + 1) * 256] = k_rope
        mv_ref[h * 128:(h + 1) * 128, :] = kv[:, h * 256 + 128:(h + 1) * 256].T.astype(BF16)


def _alibi_slope(h):
    return 2.0 ** (-8.0 * (h + 1) / DIFF_HEADS)


def _rotated(w):
    half = w.shape[1] // 2
    return jnp.concatenate([-w[:, half:], w[:, :half]], axis=1)


def _pad_cols(w, width):
    return jnp.pad(w, ((0, 0), (0, width - w.shape[1])))


def _even_proj(h, g, w_in, g_cq, w_uq, g_ckv, w_ukv, cos_pad, sin_pad, *, seq, tm):
    n, d = h.shape
    kr = w_in[:, 1920:1984]
    w_ext = jnp.concatenate(
        [w_in[:, :512] * (DIFF_DK ** -0.5), w_in[:, 512:1920], _pad_cols(kr, 128), _pad_cols(_rotated(kr), 128)],
        axis=1).astype(BF16)
    per_head = MLA_NOPE + MLA_ROPE
    nope = [w_uq[:, hh * per_head:hh * per_head + MLA_NOPE] for hh in range(MLA_HEADS)]
    rope = [w_uq[:, hh * per_head + MLA_NOPE:(hh + 1) * per_head] for hh in range(MLA_HEADS)]
    wq_ext = jnp.concatenate(
        nope + [_pad_cols(r, 128) for r in rope] + [_pad_cols(_rotated(r), 128) for r in rope], axis=1).astype(BF16)
    pos_blocks = seq // tm
    bf = lambda width: jax.ShapeDtypeStruct((n, width), BF16)
    row = lambda width: pl.BlockSpec((tm, width), lambda i: (i, 0))
    vt_shape = jax.ShapeDtypeStruct((n // seq, pos_blocks, 512, tm), BF16)
    vt_spec = pl.BlockSpec((None, None, 512, tm), lambda i: (i // pos_blocks, i % pos_blocks, 0, 0))
    return pl.pallas_call(
        _even_proj_kernel,
        grid=(n // tm,),
        in_specs=[
            row(d),
            _resident((1, d)),
            _resident(w_ext.shape),
            _resident((1, g_cq.shape[-1])),
            _resident(wq_ext.shape),
            _resident((1, g_ckv.shape[-1])),
            _resident(w_ukv.shape),
            pl.BlockSpec((tm, 128), lambda i: (i % pos_blocks, 0)),
            pl.BlockSpec((tm, 128), lambda i: (i % pos_blocks, 0)),
        ],
        out_specs=[row(1024), row(1024), vt_spec, row(1024), row(1024), vt_spec],
        out_shape=[bf(1024), bf(1024), vt_shape, bf(1024), bf(1024), vt_shape],
        compiler_params=_params(1),
        name="even_proj",
    )(h, g.reshape(1, -1), w_ext, g_cq.reshape(1, -1), wq_ext, g_ckv.reshape(1, -1), w_ukv.astype(BF16),
      cos_pad, sin_pad)


class _Softmax:
    def __init__(self, s0, s1, p, a, m, acc):
        self.s, self.p, self.a, self.m, self.acc = (s0, s1), p, a, m, acc

    def reset(self):
        self.m[...] = jnp.full(self.m.shape, NEG_BIG, F32)
        self.acc[...] = jnp.zeros(self.acc.shape, F32)
        self.p[...] = jnp.zeros(self.p.shape, BF16)
        self.a[...] = jnp.ones(self.a.shape, F32)

    def weigh(self, cur, scale, bias, shift):
        m_prev = self.m[...]
        if bias is None:
            m_block = jnp.max(self.s[cur][...], axis=0, keepdims=True) * scale
            m_new = jnp.maximum(m_prev, m_block if shift is None else m_block + shift)
            p = jnp.exp2(self.s[cur][...] * scale - (m_new if shift is None else m_new - shift))
        else:
            logits = self.s[cur][...] * scale + bias
            m_block = jnp.max(logits, axis=0, keepdims=True)
            m_new = jnp.maximum(m_prev, m_block if shift is None else m_block + shift)
            p = jnp.exp2(logits - (m_new if shift is None else m_new - shift))
        self.m[...] = m_new
        self.p[...] = p.astype(BF16)
        self.a[...] = jnp.exp2(m_prev - m_new)

    def start_values(self, vt):
        vt_ones = jnp.concatenate([vt, jnp.ones((ONES_ROWS, vt.shape[1]), BF16)], axis=0)
        return self.a[...], _dot(vt_ones, self.p[...])

    def fold(self, alpha, pv):
        self.acc[...] = alpha * self.acc[...] + pv

    def normalized(self):
        return self.acc[:DV, :] / self.acc[DV:DV + 1, :]


def _run_key_blocks(i, sms, scores, q_ref, q_next_ref, vt_ref, scale, diagonal_bias, block_shift):
    @pl.when(i == 0)
    def _():
        for sm, s in zip(sms, scores(q_ref, 0)):
            sm.s[0][...] = s

    for sm in sms:
        sm.reset()

    def iteration(j, cur):
        vt = vt_ref[jnp.maximum(j - 1, 0)]
        next_scores = scores(q_ref, j + 1)
        scores_first = cur == 1 and len(sms) == 1
        if scores_first:
            sms[0].s[1 - cur][...] = next(next_scores)
        pending = [sm.start_values(vt) for sm in sms]
        if not scores_first:
            for sm in sms:
                sm.s[1 - cur][...] = next(next_scores)
        for sm in sms:
            sm.weigh(cur, scale, None, block_shift(j))
        for sm, pend in zip(sms, pending):
            sm.fold(*pend)

    def pair(jj, carry):
        iteration(2 * jj, 0)
        iteration(2 * jj + 1, 1)
        return carry

    lax.fori_loop(0, i >> 1, pair, 0)

    @pl.when((i & 1) == 1)
    def _():
        iteration(i - 1, 0)
        for sm in sms:
            sm.s[0][...] = sm.s[1][...]

    vt = vt_ref[jnp.maximum(i - 1, 0)]
    pending = [sm.start_values(vt) for sm in sms]
    for sm in sms:
        sm.weigh(0, scale, diagonal_bias, block_shift(i))
    last = [sm.start_values(vt_ref[i]) for sm in sms]
    for sm, s in zip(sms, scores(q_next_ref, 0)):
        sm.s[0][...] = s
    for sm, pend in zip(sms, pending):
        sm.fold(*pend)
    for sm, pend in zip(sms, last):
        sm.fold(*pend)


def _next_query_spec(blocks, t):
    return pl.BlockSpec((None, t, 256), lambda bi, h, i: (bi, jnp.minimum(i + 1, blocks - 1), h))


def _softmax_scratch(t):
    return [pltpu.VMEM((t, t), F32), pltpu.VMEM((t, t), F32), pltpu.VMEM((t, t), BF16), pltpu.VMEM((1, t), F32),
            pltpu.VMEM((1, t), F32), pltpu.VMEM((DV + ONES_ROWS, t), F32)]


def _key_query_iotas(t):
    return lax.broadcasted_iota(jnp.int32, (t, t), 0), lax.broadcasted_iota(jnp.int32, (t, t), 1)


def _diff_attn_kernel(q_ref, q_next_ref, k_ref, vt_ref, slope_ref, lq1_ref, lk1_ref, lq2_ref, lk2_ref, gsub_ref,
                      o_ref, bias_diag, *scratch, t, lambda_init):
    i = pl.program_id(2)
    sm1 = _Softmax(*scratch[:6])
    sm2 = _Softmax(*scratch[6:])

    @pl.when(i == 0)
    def _():
        key, qry = _key_query_iotas(t)
        ahead = jnp.maximum(key - qry, 0).astype(F32)
        allowed = (key >> CHUNK_SHIFT) <= (qry >> CHUNK_SHIFT)
        bias_diag[...] = jnp.where(allowed, (-2.0 * LOG2E) * slope_ref[:, :1] * ahead, NEG_BIG)

    def scores(q, j):
        k = k_ref[pl.ds(pl.multiple_of(j * t, t), t), :]
        yield _dot_nt(k[:, :128], q[:, :128])
        yield _dot_nt(k[:, 128:], q[:, 128:])

    def block_shift(j):
        return (LOG2E * t) * slope_ref[:, :1] * j.astype(F32)

    _run_key_blocks(i, (sm1, sm2), scores, q_ref, q_next_ref, vt_ref, LOG2E, bias_diag[...], block_shift)

    lam = (jnp.exp(jnp.sum(lq1_ref[...] * lk1_ref[...], axis=-1, keepdims=True))
           - jnp.exp(jnp.sum(lq2_ref[...] * lk2_ref[...], axis=-1, keepdims=True)) + lambda_init)
    o = (sm1.normalized() - lam * sm2.normalized()).T
    o_ref[...] = (_rms(o, gsub_ref[...]) * (1.0 - lambda_init)).astype(o_ref.dtype)


def _diff_attn(dq, dk, dv, lq1, lk1, lq2, lk2, g_sub, lambda_init, *, t):
    b, s, _ = dq.shape
    slopes = 2.0 ** (-8.0 * jnp.arange(1, DIFF_HEADS + 1, dtype=F32) / DIFF_HEADS)
    slopes = jnp.broadcast_to(slopes[:, None, None], (DIFF_HEADS, 1, LANES))
    vec = lambda a: a.reshape(1, -1)
    small = lambda width: pl.BlockSpec((1, width), lambda bi, h, i: (0, 0))
    kern = functools.partial(_diff_attn_kernel, t=t, lambda_init=lambda_init)
    return pl.pallas_call(
        kern,
        grid=(b, DIFF_HEADS, s // t),
        in_specs=[
            pl.BlockSpec((None, t, 256), lambda bi, h, i: (bi, i, h)),
            _next_query_spec(s // t, t),
            pl.BlockSpec((None, s, 256), lambda bi, h, i: (bi, 0, h)),
            pl.BlockSpec((None, s // t, 128, t), lambda bi, h, i: (bi, 0, h, 0)),
            pl.BlockSpec((None, 1, LANES), lambda bi, h, i: (h, 0, 0)),
            small(DIFF_DK), small(DIFF_DK), small(DIFF_DK), small(DIFF_DK),
            small(2 * DIFF_DK),
        ],
        out_specs=pl.BlockSpec((None, t, 128), lambda bi, h, i: (bi, i, h)),
        out_shape=jax.ShapeDtypeStruct((b, s, DIFF_HEADS * 128), BF16),
        scratch_shapes=[pltpu.VMEM((t, t), F32)] + _softmax_scratch(t) * 2,
        compiler_params=_params(3),
        name="diff_attn",
    )(dq, dq, dk, dv, slopes, vec(lq1), vec(lk1), vec(lq2), vec(lk2), vec(g_sub))


def _mla_attn_kernel(q_ref, q_next_ref, k_ref, vt_ref, o_ref, mask_diag, *scratch, t):
    i = pl.program_id(2)
    sm = _Softmax(*scratch)

    @pl.when(i == 0)
    def _():
        key, qry = _key_query_iotas(t)
        mask_diag[...] = jnp.where((key >> CHUNK_SHIFT) <= (qry >> CHUNK_SHIFT), 0.0, NEG_BIG)

    def scores(q, j):
        yield _dot_nt(k_ref[pl.ds(pl.multiple_of(j * t, t), t), :], q[...])

    _run_key_blocks(i, (sm,), scores, q_ref, q_next_ref, vt_ref, MLA_SCALE * LOG2E, mask_diag[...], lambda j: None)
    o_ref[...] = sm.normalized().T.astype(o_ref.dtype)


def _mla_attn(mq, mk, mv, *, t):
    b, s, _ = mq.shape
    return pl.pallas_call(
        functools.partial(_mla_attn_kernel, t=t),
        grid=(b, MLA_HEADS, s // t),
        in_specs=[
            pl.BlockSpec((None, t, 256), lambda bi, h, i: (bi, i, h)),
            _next_query_spec(s // t, t),
            pl.BlockSpec((None, s, 256), lambda bi, h, i: (bi, 0, h)),
            pl.BlockSpec((None, s // t, 128, t), lambda bi, h, i: (bi, 0, h, 0)),
        ],
        out_specs=pl.BlockSpec((None, t, 128), lambda bi, h, i: (bi, i, h)),
        out_shape=jax.ShapeDtypeStruct((b, s, MLA_HEADS * 128), BF16),
        scratch_shapes=[pltpu.VMEM((t, t), F32)] + _softmax_scratch(t),
        compiler_params=_params(3),
        name="mla_attn",
    )(mq, mq, mk, mv)


def _sb_attn_kernel(q_ref, k_ref, v_ref, o_ref, carry_ref, acc_ref, *, t, pairs):
    i = pl.program_id(2)
    heads = 2 * pairs
    low = lax.broadcasted_iota(jnp.int32, (t, LANES), 1) < SB_DH
    q_heads = []
    for p in range(pairs):
        q = q_ref[:, p * LANES:(p + 1) * LANES]
        zero = jnp.zeros_like(q)
        q_heads += [jnp.where(low, q, zero), jnp.where(low, zero, q)]
    row, col = _key_query_iotas(t)
    later = (row > col).astype(BF16)
    strict = col < row

    def block(ref, j, h):
        return ref[pl.ds(pl.multiple_of(j * t, t), t), (h // 2) * LANES:(h // 2 + 1) * LANES]

    def keep_sums(z, mask):
        neg_z = -z
        log_keep = jnp.minimum(neg_z, 0.0) - jnp.log(1.0 + jnp.exp(jnp.minimum(z, neg_z)))
        if mask is not None:
            log_keep = jnp.where(mask, log_keep, 0.0)
        hi = log_keep.astype(BF16)
        lo = (log_keep - hi.astype(F32)).astype(BF16)
        sums = _dot(jnp.concatenate([hi, lo], axis=0), later)
        return z + log_keep, jnp.sum(log_keep, axis=-1, keepdims=True), sums[:t] + sums[t:]

    def weights(log_beta, sums, carry, mask):
        between = sums if carry is None else sums + carry
        a = jnp.exp(log_beta + between)
        if mask is not None:
            a = jnp.where(mask, a, 0.0)
        return a.astype(BF16)

    def near(with_previous):
        back = (0, 1) if with_previous else (0,)
        units = [(h, d) for h in range(heads) for d in back]

        def scores(u):
            h, d = units[u]
            return _dot_nt(q_heads[h], block(k_ref, i - d, h))

        def finish(u, staged, pending):
            h, d = units[u]
            log_beta, total, sums = staged[u]
            if d == 0:
                pv = _dot(weights(log_beta, sums, None, strict), block(v_ref, i, h))
                carry = total
            else:
                carry_diag, pv_diag = pending.pop(h)
                pv = pv_diag + _dot(weights(log_beta, sums, carry_diag, None), block(v_ref, i - 1, h))
                carry = carry_diag + total
            if d == back[-1]:
                acc_ref[h] = pv
                carry_ref[h] = carry
            else:
                pending[h] = (carry, pv)

        z = {u: scores(u) for u in range(min(2, len(units)))}
        staged, pending = {}, {}
        for u in range(len(units)):
            staged[u] = keep_sums(z.pop(u), strict if units[u][1] == 0 else None)
            if u + 2 < len(units):
                z[u + 2] = scores(u + 2)
            if u >= 2:
                finish(u - 2, staged, pending)
        for u in range(max(len(units) - 2, 0), len(units)):
            finish(u, staged, pending)

    def alive():
        return jnp.max(carry_ref[...]) > EXP_UNDERFLOW

    def far(state):
        j = i - 2 - state[0]
        z = [_dot_nt(q_heads[h], block(k_ref, j, h)) for h in range(heads)]
        staged = [keep_sums(z[h], None) for h in range(heads)]
        for h in range(heads):
            log_beta, total, sums = staged[h]
            acc_ref[h] += _dot(weights(log_beta, sums, carry_ref[h], None), block(v_ref, j, h))
            carry_ref[h] += total
        return state[0] + 1, alive()

    @pl.when(i == 0)
    def _():
        near(False)

    @pl.when(i > 0)
    def _():
        near(True)
        lax.while_loop(lambda state: jnp.logical_and(state[0] < i - 1, state[1]), far, (jnp.int32(0), alive()))

    for p in range(pairs):
        o_ref[:, p * LANES:(p + 1) * LANES] = jnp.where(low, acc_ref[2 * p], acc_ref[2 * p + 1]).astype(o_ref.dtype)


def _sb_attn(qkv, *, t, pairs):
    b, s, _ = qkv.shape
    groups = SB_HEADS * SB_DH // (LANES * pairs)
    width = LANES * pairs
    return pl.pallas_call(
        functools.partial(_sb_attn_kernel, t=t, pairs=pairs),
        grid=(b, groups, s // t),
        in_specs=[
            pl.BlockSpec((None, t, width), lambda bi, g, i: (bi, i, g)),
            pl.BlockSpec((None, s, width), lambda bi, g, i: (bi, 0, groups + g)),
            pl.BlockSpec((None, s, width), lambda bi, g, i: (bi, 0, 2 * groups + g)),
        ],
        out_specs=pl.BlockSpec((None, t, width), lambda bi, g, i: (bi, i, g)),
        out_shape=jax.ShapeDtypeStruct((b, s, SB_HEADS * SB_DH), BF16),
        scratch_shapes=[pltpu.VMEM((2 * pairs, t, 1), F32), pltpu.VMEM((2 * pairs, t, LANES), F32)],
        compiler_params=_params(3),
        name="sb_attn",
    )(qkv, qkv, qkv)


def _cross_kernel(h_ref, mix_ref, wmix_ref, g_ref, wq_ref, kv_ref, wo_ref, out_ref, o_scr):
    d = h_ref.shape[1]
    dh = d // XA_HEADS
    h1 = h_ref[...] + _dot(mix_ref[...], wmix_ref[...])
    q = _dot(_rms(h1, g_ref[...]).astype(BF16), wq_ref[...]).astype(BF16)
    for hd in range(XA_HEADS):
        s = _dot_nt(q[:, hd * dh:(hd + 1) * dh], kv_ref[:, hd * dh:(hd + 1) * dh])
        p = jnp.exp(s - jnp.max(s, axis=-1, keepdims=True))
        o = _dot(p.astype(BF16), kv_ref[:, d + hd * dh:d + (hd + 1) * dh]) / jnp.sum(p, axis=-1, keepdims=True)
        o_scr[:, hd * dh:(hd + 1) * dh] = o.astype(BF16)
    out_ref[...] = h1 + _dot(o_scr[...], wo_ref[...])


def _cross(h, mix, w_mix, g, wq, mem_kv, wo, *, seq, tm):
    n, d = h.shape
    blocks = seq // tm
    mem_len = mem_kv.shape[1]
    dh = d // XA_HEADS
    return pl.pallas_call(
        _cross_kernel,
        grid=(n // tm,),
        in_specs=[
            pl.BlockSpec((tm, d), lambda i: (i, 0)),
            pl.BlockSpec((tm, mix.shape[1]), lambda i: (i, 0)),
            _resident(w_mix.shape),
            _resident((1, d)),
            _resident(wq.shape),
            pl.BlockSpec((None, mem_len, 2 * d), lambda i: (i // blocks, 0, 0)),
            _resident(wo.shape),
        ],
        out_specs=pl.BlockSpec((tm, d), lambda i: (i, 0)),
        out_shape=jax.ShapeDtypeStruct((n, d), F32),
        scratch_shapes=[pltpu.VMEM((tm, d), BF16)],
        compiler_params=_params(1),
        name="cross_attn",
    )(h, mix, w_mix.astype(BF16), g.reshape(1, -1), (wq * dh ** -0.5).astype(BF16), mem_kv, wo.astype(BF16))


def _mlp_kernel(h_ref, g_ref, w1_ref, w2_ref, gf_ref, out_ref, *, tf, final_norm):
    x = h_ref[...]
    xn = _rms(x, g_ref[...]).astype(BF16)
    acc = x
    for c in range(w1_ref.shape[1] // tf):
        a = jnp.maximum(_dot(xn, w1_ref[:, c * tf:(c + 1) * tf]), 0.0)
        acc = acc + _dot((a * a).astype(BF16), w2_ref[c * tf:(c + 1) * tf, :])
    if final_norm:
        acc = _rms(acc, gf_ref[...])
    out_ref[...] = acc


def _mlp(h, g, w1, w2, g_final, *, tm, tf, final_norm):
    n, d = h.shape
    return pl.pallas_call(
        functools.partial(_mlp_kernel, tf=tf, final_norm=final_norm),
        grid=(n // tm,),
        in_specs=[
            pl.BlockSpec((tm, d), lambda i: (i, 0)),
            _resident((1, d)),
            _resident(w1.shape),
            _resident(w2.shape),
            _resident((1, d)),
        ],
        out_specs=pl.BlockSpec((tm, d), lambda i: (i, 0)),
        out_shape=jax.ShapeDtypeStruct((n, d), F32),
        compiler_params=_params(1),
        name="mlp",
    )(h, g.reshape(1, -1), w1.astype(BF16), w2.astype(BF16), g_final.reshape(1, -1))


def _rope_tables(seq):
    half = MLA_ROPE // 2
    inv = ROPE_BASE ** (-jnp.arange(half, dtype=F32) / half)
    ang = jnp.arange(seq).astype(F32)[:, None] * inv[None, :]
    pad = lambda a: jnp.concatenate([a, a, jnp.zeros((seq, LANES - 2 * half), F32)], axis=1)
    return pad(jnp.cos(ang)), pad(jnp.sin(ang))


def kernel(x, mem, ev_norm, ev_w_in, diff_lq1, diff_lk1, diff_lq2, diff_lk2, diff_subln, mla_g_cq, mla_w_uq, mla_g_ckv, mla_w_ukv, ev_w_out, od_norm, sb_w_in, sb_w_out, xa_norm, xa_mem_norm, xa_wq, xa_wkv, xa_wo, mlp_norm, mlp_w1, mlp_w2, final_norm):
    b, seq, d = x.shape
    mem_len = mem.shape[1]
    depth = xa_norm.shape[0]
    tm = 512
    h = x.reshape(b * seq, d)
    mem2 = mem.reshape(b * mem_len, d)
    cos_pad, sin_pad = _rope_tables(seq)
    for i in range(depth):
        j = i // 2
        if i % 2 == 0:
            lambda_init = 0.8 - 0.6 * math.exp(-0.3 * i)
            dq, dk, dv, mq, mk, mv = _even_proj(
                h, ev_norm[j], ev_w_in[j], mla_g_cq[j], mla_w_uq[j], mla_g_ckv[j], mla_w_ukv[j],
                cos_pad, sin_pad, seq=seq, tm=tm)
            split = lambda a: a.reshape(b, seq, a.shape[-1])
            o_a = _diff_attn(split(dq), split(dk), dv, diff_lq1[j], diff_lk1[j], diff_lq2[j], diff_lk2[j],
                             diff_subln[j], lambda_init, t=tm)
            o_b = _mla_attn(split(mq), split(mk), mv, t=tm)
            mix = jnp.concatenate([o_a, o_b], axis=-1).reshape(b * seq, -1)
            w_mix = ev_w_out[j]
        else:
            w_in = sb_w_in[j]
            width = SB_HEADS * SB_DH
            w_in = jnp.concatenate([w_in[:, :width] * (SB_DH ** -0.5), w_in[:, width:]], axis=1).astype(BF16)
            qkv = _norm_matmul(h, od_norm[j].reshape(1, -1), w_in, tm=tm, name="sb_proj")
            mix = _sb_attn(qkv.reshape(b, seq, -1), t=256, pairs=4).reshape(b * seq, -1)
            w_mix = sb_w_out[j]
        mem_kv = _norm_matmul(mem2, xa_mem_norm[i].reshape(1, -1), xa_wkv[i].astype(BF16), tm=mem_len, name="mem_kv")
        h = _cross(h, mix, w_mix, xa_norm[i], xa_wq[i], mem_kv.reshape(b, mem_len, -1), xa_wo[i], seq=seq, tm=tm)
        h = _mlp(h, mlp_norm[i], mlp_w1[i], mlp_w2[i], final_norm, tm=tm, tf=1024, final_norm=(i == depth - 1))
    return h.reshape(b, seq, d)
```

```python
import functools
import math

import jax
import jax.numpy as jnp
from jax import lax
from jax.experimental import pallas as pl
from jax.experimental.pallas import tpu as pltpu

F32 = jnp.float32
BF16 = jnp.bfloat16

EPS = 1e-6
CHUNK_SHIFT = 6
DIFF_HEADS = 4
DIFF_DK = 64
MLA_HEADS = 4
MLA_NOPE = 128
MLA_ROPE = 64
MLA_SCALE = (MLA_NOPE + MLA_ROPE) ** -0.5
ROPE_BASE = 10000.0
SB_HEADS = 16
SB_DH = 64
XA_HEADS = 4
NEG_BIG = -1e30
LOG2E = math.log2(math.e)
EXP_UNDERFLOW = -104.0
LANES = 128
DV = 128
ONES_ROWS = 16

VMEM_LIMIT_BYTES = 48 * 1024 * 1024


def _params(n_grid):
    return pltpu.CompilerParams(
        dimension_semantics=("arbitrary",) * n_grid, vmem_limit_bytes=VMEM_LIMIT_BYTES
    )


def _resident(shape):
    zeros = (0,) * len(shape)
    return pl.BlockSpec(shape, lambda *_: zeros, pipeline_mode=pl.Buffered(1))


def _rms(x, g):
    ms = jnp.mean(x * x, axis=-1, keepdims=True)
    return x * lax.rsqrt(ms + EPS) * g


def _dot(a, b):
    return jnp.dot(a, b, preferred_element_type=F32)


def _dot_nt(a, b):
    return lax.dot_general(a, b, (((1,), (1,)), ((), ())), preferred_element_type=F32)


def _norm_matmul_kernel(x_ref, g_ref, w_ref, o_ref):
    xn = _rms(x_ref[...], g_ref[...]).astype(BF16)
    o_ref[...] = _dot(xn, w_ref[...]).astype(o_ref.dtype)


def _norm_matmul(x, g, w, *, tm, name):
    n, d = x.shape
    f = w.shape[1]
    return pl.pallas_call(
        _norm_matmul_kernel,
        grid=(n // tm,),
        in_specs=[
            pl.BlockSpec((tm, d), lambda i: (i, 0)),
            _resident((1, d)),
            _resident((d, f)),
        ],
        out_specs=pl.BlockSpec((tm, f), lambda i: (i, 0)),
        out_shape=jax.ShapeDtypeStruct((n, f), BF16),
        compiler_params=_params(1),
        name=name,
    )(x, g, w)


def _even_proj_kernel(h_ref, g_ref, w_ref, gcq_ref, wq_ref, gckv_ref, wkv_ref, cos_ref, sin_ref,
                      dq_ref, dk_ref, dv_ref, mq_ref, mk_ref, mv_ref):
    tm = h_ref.shape[0]
    xn = _rms(h_ref[...], g_ref[...]).astype(BF16)
    proj = _dot(xn, w_ref[...])
    lane = lax.broadcasted_iota(jnp.int32, (tm, LANES), 1)
    low = lane < DIFF_DK
    frame_in_block = lax.broadcasted_iota(jnp.int32, (tm, LANES), 0).astype(F32)
    ones = jnp.where(lane < DIFF_DK + 3, 1.0, 0.0)
    for h in range(DIFF_HEADS):
        b = frame_in_block * _alibi_slope(h)
        b_hi = b.astype(BF16).astype(F32)
        b_mid = (b - b_hi).astype(BF16).astype(F32)
        bias = jnp.where(lane == DIFF_DK, b_hi, jnp.where(lane == DIFF_DK + 1, b_mid, b - b_hi - b_mid))
        bias = jnp.where(lane < DIFF_DK + 3, bias, 0.0)
        q_pair = proj[:, h * 128:(h + 1) * 128]
        k_pair = proj[:, 512 + h * 128:512 + (h + 1) * 128]
        for half, (q, k) in enumerate(((q_pair, k_pair), (pltpu.roll(q_pair, DIFF_DK, 1), pltpu.roll(k_pair, DIFF_DK, 1)))):
            cols = slice(h * 256 + half * 128, h * 256 + (half + 1) * 128)
            dq_ref[:, cols] = jnp.where(low, q, ones).astype(BF16)
            dk_ref[:, cols] = jnp.where(low, k, bias).astype(BF16)
    dv_ref[...] = proj[:, 1024:1536].T.astype(BF16)

    cos = cos_ref[...]
    sin = sin_ref[...]
    k_rope = (proj[:, 1920:2048] * cos + proj[:, 2048:2176] * sin).astype(BF16)

    cqn = _rms(proj[:, 1536:1792], gcq_ref[...]).astype(BF16)
    qm = _dot(cqn, wq_ref[...])
    ckvn = _rms(proj[:, 1792:1920], gckv_ref[...]).astype(BF16)
    kv = _dot(ckvn, wkv_ref[...])
    for h in range(MLA_HEADS):
        mq_ref[:, h * 256:h * 256 + 128] = qm[:, h * 128:(h + 1) * 128].astype(BF16)
        q_rope = qm[:, 512 + h * 128:512 + (h + 1) * 128] * cos + qm[:, 1024 + h * 128:1024 + (h + 1) * 128] * sin
        mq_ref[:, h * 256 + 128:(h + 1) * 256] = q_rope.astype(BF16)
        mk_ref[:, h * 256:h * 256 + 128] = kv[:, h * 256:h * 256 + 128].astype(BF16)
        mk_ref[:, h * 256 + 128:(h + 1) * 256] = k_rope
        mv_ref[h * 128:(h + 1) * 128, :] = kv[:, h * 256 + 128:(h + 1) * 256].T.astype(BF16)


def _alibi_slope(h):
    return 2.0 ** (-8.0 * (h + 1) / DIFF_HEADS)


def _rotated(w):
    half = w.shape[1] // 2
    return jnp.concatenate([-w[:, half:], w[:, :half]], axis=1)


def _pad_cols(w, width):
    return jnp.pad(w, ((0, 0), (0, width - w.shape[1])))


def _even_proj(h, g, w_in, g_cq, w_uq, g_ckv, w_ukv, cos_pad, sin_pad, *, seq, tm):
    n, d = h.shape
    kr = w_in[:, 1920:1984]
    w_ext = jnp.concatenate(
        [w_in[:, :512] * (DIFF_DK ** -0.5), w_in[:, 512:1920], _pad_cols(kr, 128), _pad_cols(_rotated(kr), 128)],
        axis=1).astype(BF16)
    per_head = MLA_NOPE + MLA_ROPE
    nope = [w_uq[:, hh * per_head:hh * per_head + MLA_NOPE] for hh in range(MLA_HEADS)]
    rope = [w_uq[:, hh * per_head + MLA_NOPE:(hh + 1) * per_head] for hh in range(MLA_HEADS)]
    wq_ext = jnp.concatenate(
        nope + [_pad_cols(r, 128) for r in rope] + [_pad_cols(_rotated(r), 128) for r in rope], axis=1).astype(BF16)
    pos_blocks = seq // tm
    bf = lambda width: jax.ShapeDtypeStruct((n, width), BF16)
    row = lambda width: pl.BlockSpec((tm, width), lambda i: (i, 0))
    vt_shape = jax.ShapeDtypeStruct((n // seq, pos_blocks, 512, tm), BF16)
    vt_spec = pl.BlockSpec((None, None, 512, tm), lambda i: (i // pos_blocks, i % pos_blocks, 0, 0))
    return pl.pallas_call(
        _even_proj_kernel,
        grid=(n // tm,),
        in_specs=[
            row(d),
            _resident((1, d)),
            _resident(w_ext.shape),
            _resident((1, g_cq.shape[-1])),
            _resident(wq_ext.shape),
            _resident((1, g_ckv.shape[-1])),
            _resident(w_ukv.shape),
            pl.BlockSpec((tm, 128), lambda i: (i % pos_blocks, 0)),
            pl.BlockSpec((tm, 128), lambda i: (i % pos_blocks, 0)),
        ],
        out_specs=[row(1024), row(1024), vt_spec, row(1024), row(1024), vt_spec],
        out_shape=[bf(1024), bf(1024), vt_shape, bf(1024), bf(1024), vt_shape],
        compiler_params=_params(1),
        name="even_proj",
    )(h, g.reshape(1, -1), w_ext, g_cq.reshape(1, -1), wq_ext, g_ckv.reshape(1, -1), w_ukv.astype(BF16),
      cos_pad, sin_pad)


class _Softmax:
    def __init__(self, s0, s1, p, a, m, acc):
        self.s, self.p, self.a, self.m, self.acc = (s0, s1), p, a, m, acc

    def reset(self):
        self.m[...] = jnp.full(self.m.shape, NEG_BIG, F32)
        self.acc[...] = jnp.zeros(self.acc.shape, F32)
        self.p[...] = jnp.zeros(self.p.shape, BF16)
        self.a[...] = jnp.ones(self.a.shape, F32)

    def weigh(self, cur, scale, bias, shift):
        m_prev = self.m[...]
        if bias is None:
            m_block = jnp.max(self.s[cur][...], axis=0, keepdims=True) * scale
            m_new = jnp.maximum(m_prev, m_block if shift is None else m_block + shift)
            p = jnp.exp2(self.s[cur][...] * scale - (m_new if shift is None else m_new - shift))
        else:
            logits = self.s[cur][...] * scale + bias
            m_block = jnp.max(logits, axis=0, keepdims=True)
            m_new = jnp.maximum(m_prev, m_block if shift is None else m_block + shift)
            p = jnp.exp2(logits - (m_new if shift is None else m_new - shift))
        self.m[...] = m_new
        self.p[...] = p.astype(BF16)
        self.a[...] = jnp.exp2(m_prev - m_new)

    def start_values(self, vt):
        vt_ones = jnp.concatenate([vt, jnp.ones((ONES_ROWS, vt.shape[1]), BF16)], axis=0)
        return self.a[...], _dot(vt_ones, self.p[...])

    def fold(self, alpha, pv):
        self.acc[...] = alpha * self.acc[...] + pv

    def normalized(self):
        return self.acc[:DV, :] / self.acc[DV:DV + 1, :]


def _run_key_blocks(i, sms, scores, q_ref, q_next_ref, vt_ref, scale, diagonal_bias, block_shift):
    @pl.when(i == 0)
    def _():
        for sm, s in zip(sms, scores(q_ref, 0)):
            sm.s[0][...] = s

    for sm in sms:
        sm.reset()

    def iteration(j, cur):
        vt = vt_ref[jnp.maximum(j - 1, 0)]
        next_scores = scores(q_ref, j + 1)
        scores_first = cur == 1 and len(sms) == 1
        if scores_first:
            sms[0].s[1 - cur][...] = next(next_scores)
        pending = [sm.start_values(vt) for sm in sms]
        if not scores_first:
            for sm in sms:
                sm.s[1 - cur][...] = next(next_scores)
        for sm in sms:
            sm.weigh(cur, scale, None, block_shift(j))
        for sm, pend in zip(sms, pending):
            sm.fold(*pend)

    def pair(jj, carry):
        iteration(2 * jj, 0)
        iteration(2 * jj + 1, 1)
        return carry

    lax.fori_loop(0, i >> 1, pair, 0)

    @pl.when((i & 1) == 1)
    def _():
        iteration(i - 1, 0)
        for sm in sms:
            sm.s[0][...] = sm.s[1][...]

    vt = vt_ref[jnp.maximum(i - 1, 0)]
    pending = [sm.start_values(vt) for sm in sms]
    for sm in sms:
        sm.weigh(0, scale, diagonal_bias, block_shift(i))
    last = [sm.start_values(vt_ref[i]) for sm in sms]
    for sm, s in zip(sms, scores(q_next_ref, 0)):
        sm.s[0][...] = s
    for sm, pend in zip(sms, pending):
        sm.fold(*pend)
    for sm, pend in zip(sms, last):
        sm.fold(*pend)


def _next_query_spec(blocks, t):
    return pl.BlockSpec((None, t, 256), lambda bi, h, i: (bi, jnp.minimum(i + 1, blocks - 1), h))


def _softmax_scratch(t):
    return [pltpu.VMEM((t, t), F32), pltpu.VMEM((t, t), F32), pltpu.VMEM((t, t), BF16), pltpu.VMEM((1, t), F32),
            pltpu.VMEM((1, t), F32), pltpu.VMEM((DV + ONES_ROWS, t), F32)]


def _key_query_iotas(t):
    return lax.broadcasted_iota(jnp.int32, (t, t), 0), lax.broadcasted_iota(jnp.int32, (t, t), 1)


def _diff_attn_kernel(q_ref, q_next_ref, k_ref, vt_ref, slope_ref, lq1_ref, lk1_ref, lq2_ref, lk2_ref, gsub_ref,
                      o_ref, bias_diag, *scratch, t, lambda_init):
    i = pl.program_id(2)
    sm1 = _Softmax(*scratch[:6])
    sm2 = _Softmax(*scratch[6:])

    @pl.when(i == 0)
    def _():
        key, qry = _key_query_iotas(t)
        ahead = jnp.maximum(key - qry, 0).astype(F32)
        allowed = (key >> CHUNK_SHIFT) <= (qry >> CHUNK_SHIFT)
        bias_diag[...] = jnp.where(allowed, (-2.0 * LOG2E) * slope_ref[:, :1] * ahead, NEG_BIG)

    def scores(q, j):
        k = k_ref[pl.ds(pl.multiple_of(j * t, t), t), :]
        yield _dot_nt(k[:, :128], q[:, :128])
        yield _dot_nt(k[:, 128:], q[:, 128:])

    def block_shift(j):
        return (LOG2E * t) * slope_ref[:, :1] * j.astype(F32)

    _run_key_blocks(i, (sm1, sm2), scores, q_ref, q_next_ref, vt_ref, LOG2E, bias_diag[...], block_shift)

    lam = (jnp.exp(jnp.sum(lq1_ref[...] * lk1_ref[...], axis=-1, keepdims=True))
           - jnp.exp(jnp.sum(lq2_ref[...] * lk2_ref[...], axis=-1, keepdims=True)) + lambda_init)
    o = (sm1.normalized() - lam * sm2.normalized()).T
    o_ref[...] = (_rms(o, gsub_ref[...]) * (1.0 - lambda_init)).astype(o_ref.dtype)


def _diff_attn(dq, dk, dv, lq1, lk1, lq2, lk2, g_sub, lambda_init, *, t):
    b, s, _ = dq.shape
    slopes = 2.0 ** (-8.0 * jnp.arange(1, DIFF_HEADS + 1, dtype=F32) / DIFF_HEADS)
    slopes = jnp.broadcast_to(slopes[:, None, None], (DIFF_HEADS, 1, LANES))
    vec = lambda a: a.reshape(1, -1)
    small = lambda width: pl.BlockSpec((1, width), lambda bi, h, i: (0, 0))
    kern = functools.partial(_diff_attn_kernel, t=t, lambda_init=lambda_init)
    return pl.pallas_call(
        kern,
        grid=(b, DIFF_HEADS, s // t),
        in_specs=[
            pl.BlockSpec((None, t, 256), lambda bi, h, i: (bi, i, h)),
            _next_query_spec(s // t, t),
            pl.BlockSpec((None, s, 256), lambda bi, h, i: (bi, 0, h)),
            pl.BlockSpec((None, s // t, 128, t), lambda bi, h, i: (bi, 0, h, 0)),
            pl.BlockSpec((None, 1, LANES), lambda bi, h, i: (h, 0, 0)),
            small(DIFF_DK), small(DIFF_DK), small(DIFF_DK), small(DIFF_DK),
            small(2 * DIFF_DK),
        ],
        out_specs=pl.BlockSpec((None, t, 128), lambda bi, h, i: (bi, i, h)),
        out_shape=jax.ShapeDtypeStruct((b, s, DIFF_HEADS * 128), BF16),
        scratch_shapes=[pltpu.VMEM((t, t), F32)] + _softmax_scratch(t) * 2,
        compiler_params=_params(3),
        name="diff_attn",
    )(dq, dq, dk, dv, slopes, vec(lq1), vec(lk1), vec(lq2), vec(lk2), vec(g_sub))


def _mla_attn_kernel(q_ref, q_next_ref, k_ref, vt_ref, o_ref, mask_diag, *scratch, t):
    i = pl.program_id(2)
    sm = _Softmax(*scratch)

    @pl.when(i == 0)
    def _():
        key, qry = _key_query_iotas(t)
        mask_diag[...] = jnp.where((key >> CHUNK_SHIFT) <= (qry >> CHUNK_SHIFT), 0.0, NEG_BIG)

    def scores(q, j):
        yield _dot_nt(k_ref[pl.ds(pl.multiple_of(j * t, t), t), :], q[...])

    _run_key_blocks(i, (sm,), scores, q_ref, q_next_ref, vt_ref, MLA_SCALE * LOG2E, mask_diag[...], lambda j: None)
    o_ref[...] = sm.normalized().T.astype(o_ref.dtype)


def _mla_attn(mq, mk, mv, *, t):
    b, s, _ = mq.shape
    return pl.pallas_call(
        functools.partial(_mla_attn_kernel, t=t),
        grid=(b, MLA_HEADS, s // t),
        in_specs=[
            pl.BlockSpec((None, t, 256), lambda bi, h, i: (bi, i, h)),
            _next_query_spec(s // t, t),
            pl.BlockSpec((None, s, 256), lambda bi, h, i: (bi, 0, h)),
            pl.BlockSpec((None, s // t, 128, t), lambda bi, h, i: (bi, 0, h, 0)),
        ],
        out_specs=pl.BlockSpec((None, t, 128), lambda bi, h, i: (bi, i, h)),
        out_shape=jax.ShapeDtypeStruct((b, s, MLA_HEADS * 128), BF16),
        scratch_shapes=[pltpu.VMEM((t, t), F32)] + _softmax_scratch(t),
        compiler_params=_params(3),
        name="mla_attn",
    )(mq, mq, mk, mv)


def _sb_attn_kernel(q_ref, k_ref, v_ref, o_ref, carry_ref, acc_ref, *, t, pairs):
    i = pl.program_id(2)
    heads = 2 * pairs
    low = lax.broadcasted_iota(jnp.int32, (t, LANES), 1) < SB_DH
    q_heads = []
    for p in range(pairs):
        q = q_ref[:, p * LANES:(p + 1) * LANES]
        zero = jnp.zeros_like(q)
        q_heads += [jnp.where(low, q, zero), jnp.where(low, zero, q)]
    row, col = _key_query_iotas(t)
    later = (row > col).astype(BF16)
    strict = col < row

    def block(ref, j, h):
        return ref[pl.ds(pl.multiple_of(j * t, t), t), (h // 2) * LANES:(h // 2 + 1) * LANES]

    def keep_sums(z, mask):
        neg_z = -z
        log_keep = jnp.minimum(neg_z, 0.0) - jnp.log(1.0 + jnp.exp(jnp.minimum(z, neg_z)))
        if mask is not None:
            log_keep = jnp.where(mask, log_keep, 0.0)
        sums = _dot(log_keep.astype(BF16), later)
        return z + log_keep, jnp.sum(log_keep, axis=-1, keepdims=True), sums

    def weights(log_beta, sums, carry, mask):
        between = sums if carry is None else sums + carry
        a = jnp.exp(log_beta + between)
        if mask is not None:
            a = jnp.where(mask, a, 0.0)
        return a.astype(BF16)

    def near(with_previous):
        back = (0, 1) if with_previous else (0,)
        units = [(h, d) for h in range(heads) for d in back]

        def scores(u):
            h, d = units[u]
            return _dot_nt(q_heads[h], block(k_ref, i - d, h))

        def finish(u, staged, pending):
            h, d = units[u]
            log_beta, total, sums = staged[u]
            if d == 0:
                pv = _dot(weights(log_beta, sums, None, strict), block(v_ref, i, h))
                carry = total
            else:
                carry_diag, pv_diag = pending.pop(h)
                pv = pv_diag + _dot(weights(log_beta, sums, carry_diag, None), block(v_ref, i - 1, h))
                carry = carry_diag + total
            if d == back[-1]:
                acc_ref[h] = pv
                carry_ref[h] = carry
            else:
                pending[h] = (carry, pv)

        z = {u: scores(u) for u in range(min(2, len(units)))}
        staged, pending = {}, {}
        for u in range(len(units)):
            staged[u] = keep_sums(z.pop(u), strict if units[u][1] == 0 else None)
            if u + 2 < len(units):
                z[u + 2] = scores(u + 2)
            if u >= 2:
                finish(u - 2, staged, pending)
        for u in range(max(len(units) - 2, 0), len(units)):
            finish(u, staged, pending)

    def alive():
        return jnp.max(carry_ref[...]) > EXP_UNDERFLOW

    def far(state):
        j = i - 2 - state[0]
        z = [_dot_nt(q_heads[h], block(k_ref, j, h)) for h in range(heads)]
        staged = [keep_sums(z[h], None) for h in range(heads)]
        for h in range(heads):
            log_beta, total, sums = staged[h]
            acc_ref[h] += _dot(weights(log_beta, sums, carry_ref[h], None), block(v_ref, j, h))
            carry_ref[h] += total
        return state[0] + 1, alive()

    @pl.when(i == 0)
    def _():
        near(False)

    @pl.when(i > 0)
    def _():
        near(True)
        lax.while_loop(lambda state: jnp.logical_and(state[0] < i - 1, state[1]), far, (jnp.int32(0), alive()))

    for p in range(pairs):
        o_ref[:, p * LANES:(p + 1) * LANES] = jnp.where(low, acc_ref[2 * p], acc_ref[2 * p + 1]).astype(o_ref.dtype)


def _sb_attn(qkv, *, t, pairs):
    b, s, _ = qkv.shape
    groups = SB_HEADS * SB_DH // (LANES * pairs)
    width = LANES * pairs
    return pl.pallas_call(
        functools.partial(_sb_attn_kernel, t=t, pairs=pairs),
        grid=(b, groups, s // t),
        in_specs=[
            pl.BlockSpec((None, t, width), lambda bi, g, i: (bi, i, g)),
            pl.BlockSpec((None, s, width), lambda bi, g, i: (bi, 0, groups + g)),
            pl.BlockSpec((None, s, width), lambda bi, g, i: (bi, 0, 2 * groups + g)),
        ],
        out_specs=pl.BlockSpec((None, t, width), lambda bi, g, i: (bi, i, g)),
        out_shape=jax.ShapeDtypeStruct((b, s, SB_HEADS * SB_DH), BF16),
        scratch_shapes=[pltpu.VMEM((2 * pairs, t, 1), F32), pltpu.VMEM((2 * pairs, t, LANES), F32)],
        compiler_params=_params(3),
        name="sb_attn",
    )(qkv, qkv, qkv)


def _cross_kernel(h_ref, mix_ref, wmix_ref, g_ref, wq_ref, kv_ref, wo_ref, out_ref, o_scr):
    d = h_ref.shape[1]
    dh = d // XA_HEADS
    h1 = h_ref[...] + _dot(mix_ref[...], wmix_ref[...])
    q = _dot(_rms(h1, g_ref[...]).astype(BF16), wq_ref[...]).astype(BF16)
    for hd in range(XA_HEADS):
        s = _dot_nt(q[:, hd * dh:(hd + 1) * dh], kv_ref[:, hd * dh:(hd + 1) * dh])
        p = jnp.exp(s - jnp.max(s, axis=-1, keepdims=True))
        o = _dot(p.astype(BF16), kv_ref[:, d + hd * dh:d + (hd + 1) * dh]) / jnp.sum(p, axis=-1, keepdims=True)
        o_scr[:, hd * dh:(hd + 1) * dh] = o.astype(BF16)
    out_ref[...] = h1 + _dot(o_scr[...], wo_ref[...])


def _cross(h, mix, w_mix, g, wq, mem_kv, wo, *, seq, tm):
    n, d = h.shape
    blocks = seq // tm
    mem_len = mem_kv.shape[1]
    dh = d // XA_HEADS
    return pl.pallas_call(
        _cross_kernel,
        grid=(n // tm,),
        in_specs=[
            pl.BlockSpec((tm, d), lambda i: (i, 0)),
            pl.BlockSpec((tm, mix.shape[1]), lambda i: (i, 0)),
            _resident(w_mix.shape),
            _resident((1, d)),
            _resident(wq.shape),
            pl.BlockSpec((None, mem_len, 2 * d), lambda i: (i // blocks, 0, 0)),
            _resident(wo.shape),
        ],
        out_specs=pl.BlockSpec((tm, d), lambda i: (i, 0)),
        out_shape=jax.ShapeDtypeStruct((n, d), F32),
        scratch_shapes=[pltpu.VMEM((tm, d), BF16)],
        compiler_params=_params(1),
        name="cross_attn",
    )(h, mix, w_mix.astype(BF16), g.reshape(1, -1), (wq * dh ** -0.5).astype(BF16), mem_kv, wo.astype(BF16))


def _mlp_kernel(h_ref, g_ref, w1_ref, w2_ref, gf_ref, out_ref, *, tf, final_norm):
    x = h_ref[...]
    xn = _rms(x, g_ref[...]).astype(BF16)
    acc = x
    for c in range(w1_ref.shape[1] // tf):
        a = jnp.maximum(_dot(xn, w1_ref[:, c * tf:(c + 1) * tf]), 0.0)
        acc = acc + _dot((a * a).astype(BF16), w2_ref[c * tf:(c + 1) * tf, :])
    if final_norm:
        acc = _rms(acc, gf_ref[...])
    out_ref[...] = acc


def _mlp(h, g, w1, w2, g_final, *, tm, tf, final_norm):
    n, d = h.shape
    return pl.pallas_call(
        functools.partial(_mlp_kernel, tf=tf, final_norm=final_norm),
        grid=(n // tm,),
        in_specs=[
            pl.BlockSpec((tm, d), lambda i: (i, 0)),
            _resident((1, d)),
            _resident(w1.shape),
            _resident(w2.shape),
            _resident((1, d)),
        ],
        out_specs=pl.BlockSpec((tm, d), lambda i: (i, 0)),
        out_shape=jax.ShapeDtypeStruct((n, d), F32),
        compiler_params=_params(1),
        name="mlp",
    )(h, g.reshape(1, -1), w1.astype(BF16), w2.astype(BF16), g_final.reshape(1, -1))


def _rope_tables(seq):
    half = MLA_ROPE // 2
    inv = ROPE_BASE ** (-jnp.arange(half, dtype=F32) / half)
    ang = jnp.arange(seq).astype(F32)[:, None] * inv[None, :]
    pad = lambda a: jnp.concatenate([a, a, jnp.zeros((seq, LANES - 2 * half), F32)], axis=1)
    return pad(jnp.cos(ang)), pad(jnp.sin(ang))


def kernel(x, mem, ev_norm, ev_w_in, diff_lq1, diff_lk1, diff_lq2, diff_lk2, diff_subln, mla_g_cq, mla_w_uq, mla_g_ckv, mla_w_ukv, ev_w_out, od_norm, sb_w_in, sb_w_out, xa_norm, xa_mem_norm, xa_wq, xa_wkv, xa_wo, mlp_norm, mlp_w1, mlp_w2, final_norm):
    b, seq, d = x.shape
    mem_len = mem.shape[1]
    depth = xa_norm.shape[0]
    tm = 512
    h = x.reshape(b * seq, d)
    mem2 = mem.reshape(b * mem_len, d)
    cos_pad, sin_pad = _rope_tables(seq)
    for i in range(depth):
        j = i // 2
        if i % 2 == 0:
            lambda_init = 0.8 - 0.6 * math.exp(-0.3 * i)
            dq, dk, dv, mq, mk, mv = _even_proj(
                h, ev_norm[j], ev_w_in[j], mla_g_cq[j], mla_w_uq[j], mla_g_ckv[j], mla_w_ukv[j],
                cos_pad, sin_pad, seq=seq, tm=tm)
            split = lambda a: a.reshape(b, seq, a.shape[-1])
            o_a = _diff_attn(split(dq), split(dk), dv, diff_lq1[j], diff_lk1[j], diff_lq2[j], diff_lk2[j],
                             diff_subln[j], lambda_init, t=tm)
            o_b = _mla_attn(split(mq), split(mk), mv, t=tm)
            mix = jnp.concatenate([o_a, o_b], axis=-1).reshape(b * seq, -1)
            w_mix = ev_w_out[j]
        else:
            w_in = sb_w_in[j]
            width = SB_HEADS * SB_DH
            w_in = jnp.concatenate([w_in[:, :width] * (SB_DH ** -0.5), w_in[:, width:]], axis=1).astype(BF16)
            qkv = _norm_matmul(h, od_norm[j].reshape(1, -1), w_in, tm=tm, name="sb_proj")
            mix = _sb_attn(qkv.reshape(b, seq, -1), t=256, pairs=4).reshape(b * seq, -1)
            w_mix = sb_w_out[j]
        mem_kv = _norm_matmul(mem2, xa_mem_norm[i].reshape(1, -1), xa_wkv[i].astype(BF16), tm=mem_len, name="mem_kv")
        h = _cross(h, mix, w_mix, xa_norm[i], xa_wq[i], mem_kv.reshape(b, mem_len, -1), xa_wo[i], seq=seq, tm=tm)
        h = _mlp(h, mlp_norm[i], mlp_w1[i], mlp_w2[i], final_norm, tm=tm, tf=1024, final_norm=(i == depth - 1))
    return h.reshape(b, seq, d)
```

```python
import functools
import math

import jax
import jax.numpy as jnp
from jax import lax
from jax.experimental import pallas as pl
from jax.experimental.pallas import tpu as pltpu

F32 = jnp.float32
BF16 = jnp.bfloat16

EPS = 1e-6
CHUNK_SHIFT = 6
DIFF_HEADS = 4
DIFF_DK = 64
MLA_HEADS = 4
MLA_NOPE = 128
MLA_ROPE = 64
MLA_SCALE = (MLA_NOPE + MLA_ROPE) ** -0.5
ROPE_BASE = 10000.0
SB_HEADS = 16
SB_DH = 64
XA_HEADS = 4
NEG_BIG = -1e30
LOG2E = math.log2(math.e)
EXP_UNDERFLOW = -104.0
LANES = 128
DV = 128
ONES_ROWS = 16

VMEM_LIMIT_BYTES = 48 * 1024 * 1024


def _params(n_grid):
    return pltpu.CompilerParams(
        dimension_semantics=("arbitrary",) * n_grid, vmem_limit_bytes=VMEM_LIMIT_BYTES
    )


def _resident(shape):
    zeros = (0,) * len(shape)
    return pl.BlockSpec(shape, lambda *_: zeros, pipeline_mode=pl.Buffered(1))


def _rms(x, g):
    ms = jnp.mean(x * x, axis=-1, keepdims=True)
    return x * lax.rsqrt(ms + EPS) * g


def _dot(a, b):
    return jnp.dot(a, b, preferred_element_type=F32)


def _dot_nt(a, b):
    return lax.dot_general(a, b, (((1,), (1,)), ((), ())), preferred_element_type=F32)


def _norm_matmul_kernel(x_ref, g_ref, w_ref, o_ref):
    xn = _rms(x_ref[...], g_ref[...]).astype(BF16)
    o_ref[...] = _dot(xn, w_ref[...]).astype(o_ref.dtype)


def _norm_matmul(x, g, w, *, tm, name):
    n, d = x.shape
    f = w.shape[1]
    return pl.pallas_call(
        _norm_matmul_kernel,
        grid=(n // tm,),
        in_specs=[
            pl.BlockSpec((tm, d), lambda i: (i, 0)),
            _resident((1, d)),
            _resident((d, f)),
        ],
        out_specs=pl.BlockSpec((tm, f), lambda i: (i, 0)),
        out_shape=jax.ShapeDtypeStruct((n, f), BF16),
        compiler_params=_params(1),
        name=name,
    )(x, g, w)


def _even_proj_kernel(h_ref, g_ref, w_ref, gcq_ref, wq_ref, gckv_ref, wkv_ref, cos_ref, sin_ref,
                      dq_ref, dk_ref, dv_ref, mq_ref, mk_ref, mv_ref):
    tm = h_ref.shape[0]
    xn = _rms(h_ref[...], g_ref[...]).astype(BF16)
    proj = _dot(xn, w_ref[...])
    lane = lax.broadcasted_iota(jnp.int32, (tm, LANES), 1)
    low = lane < DIFF_DK
    frame_in_block = lax.broadcasted_iota(jnp.int32, (tm, LANES), 0).astype(F32)
    ones = jnp.where(lane < DIFF_DK + 3, 1.0, 0.0)
    for h in range(DIFF_HEADS):
        b = frame_in_block * _alibi_slope(h)
        b_hi = b.astype(BF16).astype(F32)
        b_mid = (b - b_hi).astype(BF16).astype(F32)
        bias = jnp.where(lane == DIFF_DK, b_hi, jnp.where(lane == DIFF_DK + 1, b_mid, b - b_hi - b_mid))
        bias = jnp.where(lane < DIFF_DK + 3, bias, 0.0)
        q_pair = proj[:, h * 128:(h + 1) * 128]
        k_pair = proj[:, 512 + h * 128:512 + (h + 1) * 128]
        for half, (q, k) in enumerate(((q_pair, k_pair), (pltpu.roll(q_pair, DIFF_DK, 1), pltpu.roll(k_pair, DIFF_DK, 1)))):
            cols = slice(h * 256 + half * 128, h * 256 + (half + 1) * 128)
            dq_ref[:, cols] = jnp.where(low, q, ones).astype(BF16)
            dk_ref[:, cols] = jnp.where(low, k, bias).astype(BF16)
    dv_ref[...] = proj[:, 1024:1536].T.astype(BF16)

    cos = cos_ref[...]
    sin = sin_ref[...]
    k_rope = (proj[:, 1920:2048] * cos + proj[:, 2048:2176] * sin).astype(BF16)

    cqn = _rms(proj[:, 1536:1792], gcq_ref[...]).astype(BF16)
    qm = _dot(cqn, wq_ref[...])
    ckvn = _rms(proj[:, 1792:1920], gckv_ref[...]).astype(BF16)
    kv = _dot(ckvn, wkv_ref[...])
    for h in range(MLA_HEADS):
        mq_ref[:, h * 256:h * 256 + 128] = qm[:, h * 128:(h + 1) * 128].astype(BF16)
        q_rope = qm[:, 512 + h * 128:512 + (h + 1) * 128] * cos + qm[:, 1024 + h * 128:1024 + (h + 1) * 128] * sin
        mq_ref[:, h * 256 + 128:(h + 1) * 256] = q_rope.astype(BF16)
        mk_ref[:, h * 256:h * 256 + 128] = kv[:, h * 256:h * 256 + 128].astype(BF16)
        mk_ref[:, h * 256 + 128:(h + 1) * 256] = k_rope
        mv_ref[h * 128:(h + 1) * 128, :] = kv[:, h * 256 + 128:(h + 1) * 256].T.astype(BF16)


def _alibi_slope(h):
    return 2.0 ** (-8.0 * (h + 1) / DIFF_HEADS)


def _rotated(w):
    half = w.shape[1] // 2
    return jnp.concatenate([-w[:, half:], w[:, :half]], axis=1)


def _pad_cols(w, width):
    return jnp.pad(w, ((0, 0), (0, width - w.shape[1])))


def _even_proj(h, g, w_in, g_cq, w_uq, g_ckv, w_ukv, cos_pad, sin_pad, *, seq, tm):
    n, d = h.shape
    kr = w_in[:, 1920:1984]
    w_ext = jnp.concatenate(
        [w_in[:, :512] * (DIFF_DK ** -0.5), w_in[:, 512:1920], _pad_cols(kr, 128), _pad_cols(_rotated(kr), 128)],
        axis=1).astype(BF16)
    per_head = MLA_NOPE + MLA_ROPE
    nope = [w_uq[:, hh * per_head:hh * per_head + MLA_NOPE] for hh in range(MLA_HEADS)]
    rope = [w_uq[:, hh * per_head + MLA_NOPE:(hh + 1) * per_head] for hh in range(MLA_HEADS)]
    wq_ext = jnp.concatenate(
        nope + [_pad_cols(r, 128) for r in rope] + [_pad_cols(_rotated(r), 128) for r in rope], axis=1).astype(BF16)
    pos_blocks = seq // tm
    bf = lambda width: jax.ShapeDtypeStruct((n, width), BF16)
    row = lambda width: pl.BlockSpec((tm, width), lambda i: (i, 0))
    vt_shape = jax.ShapeDtypeStruct((n // seq, pos_blocks, 512, tm), BF16)
    vt_spec = pl.BlockSpec((None, None, 512, tm), lambda i: (i // pos_blocks, i % pos_blocks, 0, 0))
    return pl.pallas_call(
        _even_proj_kernel,
        grid=(n // tm,),
        in_specs=[
            row(d),
            _resident((1, d)),
            _resident(w_ext.shape),
            _resident((1, g_cq.shape[-1])),
            _resident(wq_ext.shape),
            _resident((1, g_ckv.shape[-1])),
            _resident(w_ukv.shape),
            pl.BlockSpec((tm, 128), lambda i: (i % pos_blocks, 0)),
            pl.BlockSpec((tm, 128), lambda i: (i % pos_blocks, 0)),
        ],
        out_specs=[row(1024), row(1024), vt_spec, row(1024), row(1024), vt_spec],
        out_shape=[bf(1024), bf(1024), vt_shape, bf(1024), bf(1024), vt_shape],
        compiler_params=_params(1),
        name="even_proj",
    )(h, g.reshape(1, -1), w_ext, g_cq.reshape(1, -1), wq_ext, g_ckv.reshape(1, -1), w_ukv.astype(BF16),
      cos_pad, sin_pad)


class _Softmax:
    def __init__(self, scratch, values, block_shift, diagonal_bias):
        s0, s1, self.p, self.a, self.m, self.acc = scratch
        self.s = (s0, s1)
        self.values, self.block_shift, self.diagonal_bias = values, block_shift, diagonal_bias

    def reset(self):
        self.m[...] = jnp.full(self.m.shape, NEG_BIG, F32)
        self.acc[...] = jnp.zeros(self.acc.shape, F32)
        self.p[...] = jnp.zeros(self.p.shape, BF16)
        self.a[...] = jnp.ones(self.a.shape, F32)

    def weigh(self, cur, scale, bias, shift):
        m_prev = self.m[...]
        if bias is None:
            m_block = jnp.max(self.s[cur][...], axis=0, keepdims=True) * scale
            m_new = jnp.maximum(m_prev, m_block if shift is None else m_block + shift)
            p = jnp.exp2(self.s[cur][...] * scale - (m_new if shift is None else m_new - shift))
        else:
            logits = self.s[cur][...] * scale + bias
            m_block = jnp.max(logits, axis=0, keepdims=True)
            m_new = jnp.maximum(m_prev, m_block if shift is None else m_block + shift)
            p = jnp.exp2(logits - (m_new if shift is None else m_new - shift))
        self.m[...] = m_new
        self.p[...] = p.astype(BF16)
        self.a[...] = jnp.exp2(m_prev - m_new)

    def start_values(self, j):
        vt = self.values(j)
        vt_ones = jnp.concatenate([vt, jnp.ones((ONES_ROWS, vt.shape[1]), BF16)], axis=0)
        return self.a[...], _dot(vt_ones, self.p[...])

    def fold(self, alpha, pv):
        self.acc[...] = alpha * self.acc[...] + pv

    def normalized(self):
        return self.acc[:DV, :] / self.acc[DV:DV + 1, :]


def _run_key_blocks(i, sms, scores, q_ref, q_next_ref, scale):
    @pl.when(i == 0)
    def _():
        for sm, s in zip(sms, scores(q_ref, 0)):
            sm.s[0][...] = s

    for sm in sms:
        sm.reset()

    def iteration(j, cur):
        previous = jnp.maximum(j - 1, 0)
        next_scores = scores(q_ref, j + 1)
        scores_first = cur == 1 and len(sms) == 1
        if scores_first:
            sms[0].s[1 - cur][...] = next(next_scores)
        pending = [sm.start_values(previous) for sm in sms]
        if not scores_first:
            for sm in sms:
                sm.s[1 - cur][...] = next(next_scores)
        for sm in sms:
            sm.weigh(cur, scale, None, sm.block_shift(j))
        for sm, pend in zip(sms, pending):
            sm.fold(*pend)

    def pair(jj, carry):
        iteration(2 * jj, 0)
        iteration(2 * jj + 1, 1)
        return carry

    lax.fori_loop(0, i >> 1, pair, 0)

    @pl.when((i & 1) == 1)
    def _():
        iteration(i - 1, 0)
        for sm in sms:
            sm.s[0][...] = sm.s[1][...]

    pending = [sm.start_values(jnp.maximum(i - 1, 0)) for sm in sms]
    for sm in sms:
        sm.weigh(0, scale, sm.diagonal_bias(), sm.block_shift(i))
    last = [sm.start_values(i) for sm in sms]
    for sm, s in zip(sms, scores(q_next_ref, 0)):
        sm.s[0][...] = s
    for sm, pend in zip(sms, pending):
        sm.fold(*pend)
    for sm, pend in zip(sms, last):
        sm.fold(*pend)


def _next_query_spec(blocks, t, width):
    return pl.BlockSpec((None, t, width), lambda bi, g, i: (bi, jnp.minimum(i + 1, blocks - 1), g))


def _softmax_scratch(t):
    return [pltpu.VMEM((t, t), F32), pltpu.VMEM((t, t), F32), pltpu.VMEM((t, t), BF16), pltpu.VMEM((1, t), F32),
            pltpu.VMEM((1, t), F32), pltpu.VMEM((DV + ONES_ROWS, t), F32)]


def _key_query_iotas(t):
    return lax.broadcasted_iota(jnp.int32, (t, t), 0), lax.broadcasted_iota(jnp.int32, (t, t), 1)


def _diff_attn_kernel(q_ref, q_next_ref, k_ref, vt_ref, slope_ref, lq1_ref, lk1_ref, lq2_ref, lk2_ref, gsub_ref,
                      o_ref, bias_diag, *scratch, t, heads, lambda_init):
    i = pl.program_id(2)

    @pl.when(i == 0)
    def _():
        key, qry = _key_query_iotas(t)
        ahead = jnp.maximum(key - qry, 0).astype(F32)
        allowed = (key >> CHUNK_SHIFT) <= (qry >> CHUNK_SHIFT)
        for hh in range(heads):
            bias_diag[hh] = jnp.where(allowed, (-2.0 * LOG2E) * slope_ref[hh][:, :1] * ahead, NEG_BIG)

    sms = []
    for hh in range(heads):
        for half in range(2):
            n = 2 * hh + half
            sms.append(_Softmax(
                scratch[6 * n:6 * n + 6],
                values=lambda j, hh=hh: vt_ref[j, hh * DV:(hh + 1) * DV, :],
                block_shift=lambda j, hh=hh: (LOG2E * t) * slope_ref[hh][:, :1] * j.astype(F32),
                diagonal_bias=lambda hh=hh: bias_diag[hh]))

    def scores(q, j):
        rows = pl.ds(pl.multiple_of(j * t, t), t)
        for n in range(2 * heads):
            cols = slice(n * 128, (n + 1) * 128)
            yield _dot_nt(k_ref[rows, cols], q[:, cols])

    _run_key_blocks(i, sms, scores, q_ref, q_next_ref, LOG2E)

    lam = (jnp.exp(jnp.sum(lq1_ref[...] * lk1_ref[...], axis=-1, keepdims=True))
           - jnp.exp(jnp.sum(lq2_ref[...] * lk2_ref[...], axis=-1, keepdims=True)) + lambda_init)
    for hh in range(heads):
        o = (sms[2 * hh].normalized() - lam * sms[2 * hh + 1].normalized()).T
        o_ref[:, hh * DV:(hh + 1) * DV] = (_rms(o, gsub_ref[...]) * (1.0 - lambda_init)).astype(o_ref.dtype)


def _diff_attn(dq, dk, dv, lq1, lk1, lq2, lk2, g_sub, lambda_init, *, t, heads):
    b, s, _ = dq.shape
    slopes = 2.0 ** (-8.0 * jnp.arange(1, DIFF_HEADS + 1, dtype=F32) / DIFF_HEADS)
    slopes = jnp.broadcast_to(slopes[:, None, None], (DIFF_HEADS, 1, LANES))
    vec = lambda a: a.reshape(1, -1)
    small = lambda width: pl.BlockSpec((1, width), lambda bi, g, i: (0, 0))
    kern = functools.partial(_diff_attn_kernel, t=t, heads=heads, lambda_init=lambda_init)
    return pl.pallas_call(
        kern,
        grid=(b, DIFF_HEADS // heads, s // t),
        in_specs=[
            pl.BlockSpec((None, t, 256 * heads), lambda bi, g, i: (bi, i, g)),
            _next_query_spec(s // t, t, 256 * heads),
            pl.BlockSpec((None, s, 256 * heads), lambda bi, g, i: (bi, 0, g)),
            pl.BlockSpec((None, s // t, DV * heads, t), lambda bi, g, i: (bi, 0, g, 0)),
            pl.BlockSpec((heads, 1, LANES), lambda bi, g, i: (g, 0, 0)),
            small(DIFF_DK), small(DIFF_DK), small(DIFF_DK), small(DIFF_DK),
            small(2 * DIFF_DK),
        ],
        out_specs=pl.BlockSpec((None, t, DV * heads), lambda bi, g, i: (bi, i, g)),
        out_shape=jax.ShapeDtypeStruct((b, s, DIFF_HEADS * DV), BF16),
        scratch_shapes=[pltpu.VMEM((heads, t, t), F32)] + _softmax_scratch(t) * (2 * heads),
        compiler_params=_params(3),
        name="diff_attn",
    )(dq, dq, dk, dv, slopes, vec(lq1), vec(lk1), vec(lq2), vec(lk2), vec(g_sub))


def _mla_attn_kernel(q_ref, q_next_ref, k_ref, vt_ref, o_ref, mask_diag, *scratch, t, heads):
    i = pl.program_id(2)

    @pl.when(i == 0)
    def _():
        key, qry = _key_query_iotas(t)
        mask_diag[...] = jnp.where((key >> CHUNK_SHIFT) <= (qry >> CHUNK_SHIFT), 0.0, NEG_BIG)

    sms = [_Softmax(scratch[6 * hh:6 * hh + 6],
                    values=lambda j, hh=hh: vt_ref[j, hh * DV:(hh + 1) * DV, :],
                    block_shift=lambda j: None,
                    diagonal_bias=lambda: mask_diag[...]) for hh in range(heads)]

    def scores(q, j):
        rows = pl.ds(pl.multiple_of(j * t, t), t)
        for hh in range(heads):
            cols = slice(hh * 256, (hh + 1) * 256)
            yield _dot_nt(k_ref[rows, cols], q[:, cols])

    _run_key_blocks(i, sms, scores, q_ref, q_next_ref, MLA_SCALE * LOG2E)
    for hh in range(heads):
        o_ref[:, hh * DV:(hh + 1) * DV] = sms[hh].normalized().T.astype(o_ref.dtype)


def _mla_attn(mq, mk, mv, *, t, heads):
    b, s, _ = mq.shape
    return pl.pallas_call(
        functools.partial(_mla_attn_kernel, t=t, heads=heads),
        grid=(b, MLA_HEADS // heads, s // t),
        in_specs=[
            pl.BlockSpec((None, t, 256 * heads), lambda bi, g, i: (bi, i, g)),
            _next_query_spec(s // t, t, 256 * heads),
            pl.BlockSpec((None, s, 256 * heads), lambda bi, g, i: (bi, 0, g)),
            pl.BlockSpec((None, s // t, DV * heads, t), lambda bi, g, i: (bi, 0, g, 0)),
        ],
        out_specs=pl.BlockSpec((None, t, DV * heads), lambda bi, g, i: (bi, i, g)),
        out_shape=jax.ShapeDtypeStruct((b, s, MLA_HEADS * DV), BF16),
        scratch_shapes=[pltpu.VMEM((t, t), F32)] + _softmax_scratch(t) * heads,
        compiler_params=_params(3),
        name="mla_attn",
    )(mq, mq, mk, mv)


def _sb_attn_kernel(q_ref, k_ref, v_ref, o_ref, carry_ref, acc_ref, *, t, pairs):
    i = pl.program_id(2)
    heads = 2 * pairs
    low = lax.broadcasted_iota(jnp.int32, (t, LANES), 1) < SB_DH
    q_heads = []
    for p in range(pairs):
        q = q_ref[:, p * LANES:(p + 1) * LANES]
        zero = jnp.zeros_like(q)
        q_heads += [jnp.where(low, q, zero), jnp.where(low, zero, q)]
    row, col = _key_query_iotas(t)
    later = (row > col).astype(BF16)
    strict = col < row

    def block(ref, j, h):
        return ref[pl.ds(pl.multiple_of(j * t, t), t), (h // 2) * LANES:(h // 2 + 1) * LANES]

    def keep_sums(z, mask):
        neg_z = -z
        log_keep = jnp.minimum(neg_z, 0.0) - jnp.log(1.0 + jnp.exp(jnp.minimum(z, neg_z)))
        if mask is not None:
            log_keep = jnp.where(mask, log_keep, 0.0)
        sums = _dot(log_keep.astype(BF16), later)
        return z + log_keep, jnp.sum(log_keep, axis=-1, keepdims=True), sums

    def weights(log_beta, sums, carry, mask):
        between = sums if carry is None else sums + carry
        a = jnp.exp(log_beta + between)
        if mask is not None:
            a = jnp.where(mask, a, 0.0)
        return a.astype(BF16)

    def near(with_previous):
        back = (0, 1) if with_previous else (0,)
        units = [(h, d) for h in range(heads) for d in back]

        def scores(u):
            h, d = units[u]
            return _dot_nt(q_heads[h], block(k_ref, i - d, h))

        def finish(u, staged, pending):
            h, d = units[u]
            log_beta, total, sums = staged[u]
            if d == 0:
                pv = _dot(weights(log_beta, sums, None, strict), block(v_ref, i, h))
                carry = total
            else:
                carry_diag, pv_diag = pending.pop(h)
                pv = pv_diag + _dot(weights(log_beta, sums, carry_diag, None), block(v_ref, i - 1, h))
                carry = carry_diag + total
            if d == back[-1]:
                acc_ref[h] = pv
                carry_ref[h] = carry
            else:
                pending[h] = (carry, pv)

        z = {u: scores(u) for u in range(min(2, len(units)))}
        staged, pending = {}, {}
        for u in range(len(units)):
            staged[u] = keep_sums(z.pop(u), strict if units[u][1] == 0 else None)
            if u + 2 < len(units):
                z[u + 2] = scores(u + 2)
            if u >= 2:
                finish(u - 2, staged, pending)
        for u in range(max(len(units) - 2, 0), len(units)):
            finish(u, staged, pending)

    def alive():
        return jnp.max(carry_ref[...]) > EXP_UNDERFLOW

    def far(state):
        j = i - 2 - state[0]
        z = [_dot_nt(q_heads[h], block(k_ref, j, h)) for h in range(heads)]
        staged = [keep_sums(z[h], None) for h in range(heads)]
        for h in range(heads):
            log_beta, total, sums = staged[h]
            acc_ref[h] += _dot(weights(log_beta, sums, carry_ref[h], None), block(v_ref, j, h))
            carry_ref[h] += total
        return state[0] + 1, alive()

    @pl.when(i == 0)
    def _():
        near(False)

    @pl.when(i > 0)
    def _():
        near(True)
        lax.while_loop(lambda state: jnp.logical_and(state[0] < i - 1, state[1]), far, (jnp.int32(0), alive()))

    for p in range(pairs):
        o_ref[:, p * LANES:(p + 1) * LANES] = jnp.where(low, acc_ref[2 * p], acc_ref[2 * p + 1]).astype(o_ref.dtype)


def _sb_attn(qkv, *, t, pairs):
    b, s, _ = qkv.shape
    groups = SB_HEADS * SB_DH // (LANES * pairs)
    width = LANES * pairs
    return pl.pallas_call(
        functools.partial(_sb_attn_kernel, t=t, pairs=pairs),
        grid=(b, groups, s // t),
        in_specs=[
            pl.BlockSpec((None, t, width), lambda bi, g, i: (bi, i, g)),
            pl.BlockSpec((None, s, width), lambda bi, g, i: (bi, 0, groups + g)),
            pl.BlockSpec((None, s, width), lambda bi, g, i: (bi, 0, 2 * groups + g)),
        ],
        out_specs=pl.BlockSpec((None, t, width), lambda bi, g, i: (bi, i, g)),
        out_shape=jax.ShapeDtypeStruct((b, s, SB_HEADS * SB_DH), BF16),
        scratch_shapes=[pltpu.VMEM((2 * pairs, t, 1), F32), pltpu.VMEM((2 * pairs, t, LANES), F32)],
        compiler_params=_params(3),
        name="sb_attn",
    )(qkv, qkv, qkv)


def _cross_kernel(h_ref, mix_ref, wmix_ref, g_ref, wq_ref, kv_ref, wo_ref, out_ref, o_scr):
    d = h_ref.shape[1]
    dh = d // XA_HEADS
    h1 = h_ref[...] + _dot(mix_ref[...], wmix_ref[...])
    q = _dot(_rms(h1, g_ref[...]).astype(BF16), wq_ref[...]).astype(BF16)
    for hd in range(XA_HEADS):
        s = _dot_nt(q[:, hd * dh:(hd + 1) * dh], kv_ref[:, hd * dh:(hd + 1) * dh])
        p = jnp.exp(s - jnp.max(s, axis=-1, keepdims=True))
        o = _dot(p.astype(BF16), kv_ref[:, d + hd * dh:d + (hd + 1) * dh]) / jnp.sum(p, axis=-1, keepdims=True)
        o_scr[:, hd * dh:(hd + 1) * dh] = o.astype(BF16)
    out_ref[...] = h1 + _dot(o_scr[...], wo_ref[...])


def _cross(h, mix, w_mix, g, wq, mem_kv, wo, *, seq, tm):
    n, d = h.shape
    blocks = seq // tm
    mem_len = mem_kv.shape[1]
    dh = d // XA_HEADS
    return pl.pallas_call(
        _cross_kernel,
        grid=(n // tm,),
        in_specs=[
            pl.BlockSpec((tm, d), lambda i: (i, 0)),
            pl.BlockSpec((tm, mix.shape[1]), lambda i: (i, 0)),
            _resident(w_mix.shape),
            _resident((1, d)),
            _resident(wq.shape),
            pl.BlockSpec((None, mem_len, 2 * d), lambda i: (i // blocks, 0, 0)),
            _resident(wo.shape),
        ],
        out_specs=pl.BlockSpec((tm, d), lambda i: (i, 0)),
        out_shape=jax.ShapeDtypeStruct((n, d), F32),
        scratch_shapes=[pltpu.VMEM((tm, d), BF16)],
        compiler_params=_params(1),
        name="cross_attn",
    )(h, mix, w_mix.astype(BF16), g.reshape(1, -1), (wq * dh ** -0.5).astype(BF16), mem_kv, wo.astype(BF16))


def _mlp_kernel(h_ref, g_ref, w1_ref, w2_ref, gf_ref, out_ref, *, tf, final_norm):
    x = h_ref[...]
    xn = _rms(x, g_ref[...]).astype(BF16)
    acc = x
    for c in range(w1_ref.shape[1] // tf):
        a = jnp.maximum(_dot(xn, w1_ref[:, c * tf:(c + 1) * tf]), 0.0)
        acc = acc + _dot((a * a).astype(BF16), w2_ref[c * tf:(c + 1) * tf, :])
    if final_norm:
        acc = _rms(acc, gf_ref[...])
    out_ref[...] = acc


def _mlp(h, g, w1, w2, g_final, *, tm, tf, final_norm):
    n, d = h.shape
    return pl.pallas_call(
        functools.partial(_mlp_kernel, tf=tf, final_norm=final_norm),
        grid=(n // tm,),
        in_specs=[
            pl.BlockSpec((tm, d), lambda i: (i, 0)),
            _resident((1, d)),
            _resident(w1.shape),
            _resident(w2.shape),
            _resident((1, d)),
        ],
        out_specs=pl.BlockSpec((tm, d), lambda i: (i, 0)),
        out_shape=jax.ShapeDtypeStruct((n, d), F32),
        compiler_params=_params(1),
        name="mlp",
    )(h, g.reshape(1, -1), w1.astype(BF16), w2.astype(BF16), g_final.reshape(1, -1))


def _rope_tables(seq):
    half = MLA_ROPE // 2
    inv = ROPE_BASE ** (-jnp.arange(half, dtype=F32) / half)
    ang = jnp.arange(seq).astype(F32)[:, None] * inv[None, :]
    pad = lambda a: jnp.concatenate([a, a, jnp.zeros((seq, LANES - 2 * half), F32)], axis=1)
    return pad(jnp.cos(ang)), pad(jnp.sin(ang))


def kernel(x, mem, ev_norm, ev_w_in, diff_lq1, diff_lk1, diff_lq2, diff_lk2, diff_subln, mla_g_cq, mla_w_uq, mla_g_ckv, mla_w_ukv, ev_w_out, od_norm, sb_w_in, sb_w_out, xa_norm, xa_mem_norm, xa_wq, xa_wkv, xa_wo, mlp_norm, mlp_w1, mlp_w2, final_norm):
    b, seq, d = x.shape
    mem_len = mem.shape[1]
    depth = xa_norm.shape[0]
    tm = 512
    h = x.reshape(b * seq, d)
    mem2 = mem.reshape(b * mem_len, d)
    cos_pad, sin_pad = _rope_tables(seq)
    for i in range(depth):
        j = i // 2
        if i % 2 == 0:
            lambda_init = 0.8 - 0.6 * math.exp(-0.3 * i)
            dq, dk, dv, mq, mk, mv = _even_proj(
                h, ev_norm[j], ev_w_in[j], mla_g_cq[j], mla_w_uq[j], mla_g_ckv[j], mla_w_ukv[j],
                cos_pad, sin_pad, seq=seq, tm=tm)
            split = lambda a: a.reshape(b, seq, a.shape[-1])
            o_a = _diff_attn(split(dq), split(dk), dv, diff_lq1[j], diff_lk1[j], diff_lq2[j], diff_lk2[j],
                             diff_subln[j], lambda_init, t=tm, heads=2)
            o_b = _mla_attn(split(mq), split(mk), mv, t=tm, heads=2)
            mix = jnp.concatenate([o_a, o_b], axis=-1).reshape(b * seq, -1)
            w_mix = ev_w_out[j]
        else:
            w_in = sb_w_in[j]
            width = SB_HEADS * SB_DH
            w_in = jnp.concatenate([w_in[:, :width] * (SB_DH ** -0.5), w_in[:, width:]], axis=1).astype(BF16)
            qkv = _norm_matmul(h, od_norm[j].reshape(1, -1), w_in, tm=tm, name="sb_proj")
            mix = _sb_attn(qkv.reshape(b, seq, -1), t=256, pairs=4).reshape(b * seq, -1)
            w_mix = sb_w_out[j]
        mem_kv = _norm_matmul(mem2, xa_mem_norm[i].reshape(1, -1), xa_wkv[i].astype(BF16), tm=mem_len, name="mem_kv")
        h = _cross(h, mix, w_mix, xa_norm[i], xa_wq[i], mem_kv.reshape(b, mem_len, -1), xa_wo[i], seq=seq, tm=tm)
        h = _mlp(h, mlp_norm[i], mlp_w1[i], mlp_w2[i], final_norm, tm=tm, tf=1024, final_norm=(i == depth - 1))
    return h.reshape(b, seq, d)
```

```python
import functools
import math

import jax
import jax.numpy as jnp
import numpy as np
from jax import lax
from jax.experimental import pallas as pl
from jax.experimental.pallas import tpu as pltpu

F32 = jnp.float32
BF16 = jnp.bfloat16

EPS = 1e-6
CHUNK_SHIFT = 6
DIFF_HEADS = 4
DIFF_DK = 64
MLA_HEADS = 4
MLA_NOPE = 128
MLA_ROPE = 64
MLA_SCALE = (MLA_NOPE + MLA_ROPE) ** -0.5
ROPE_BASE = 10000.0
SB_HEADS = 16
SB_DH = 64
XA_HEADS = 4
NEG_BIG = -1e30
LOG2E = math.log2(math.e)
EXP_UNDERFLOW = -104.0
LANES = 128
DV = 128
ONES_ROWS = 16

VMEM_LIMIT_BYTES = 48 * 1024 * 1024

ROW_TILE = 512
MLP_FF_TILE = 1024
DIFF_HEADS_PER_STEP = 2
MLA_HEADS_PER_STEP = 4
SB_BLOCK = 256
SB_PAIRS_PER_STEP = 4


def _params(n_grid):
    return pltpu.CompilerParams(
        dimension_semantics=("arbitrary",) * n_grid, vmem_limit_bytes=VMEM_LIMIT_BYTES
    )


def _resident(shape):
    zeros = (0,) * len(shape)
    return pl.BlockSpec(shape, lambda *_: zeros, pipeline_mode=pl.Buffered(1))


def _rms(x, g):
    ms = jnp.mean(x * x, axis=-1, keepdims=True)
    return x * lax.rsqrt(ms + EPS) * g


def _dot(a, b):
    return jnp.dot(a, b, preferred_element_type=F32)


def _dot_nt(a, b):
    return lax.dot_general(a, b, (((1,), (1,)), ((), ())), preferred_element_type=F32)


def _norm_matmul_kernel(x_ref, g_ref, w_ref, o_ref, *, scaled_cols, scale):
    xn = _rms(x_ref[...], g_ref[...]).astype(BF16)
    if scaled_cols:
        o_ref[:, :scaled_cols] = (_dot(xn, w_ref[:, :scaled_cols]) * scale).astype(o_ref.dtype)
    o_ref[:, scaled_cols:] = _dot(xn, w_ref[:, scaled_cols:]).astype(o_ref.dtype)


def _norm_matmul(x, g, w, *, tm, name, scaled_cols=0, scale=1.0):
    n, d = x.shape
    f = w.shape[1]
    return pl.pallas_call(
        functools.partial(_norm_matmul_kernel, scaled_cols=scaled_cols, scale=scale),
        grid=(n // tm,),
        in_specs=[
            pl.BlockSpec((tm, d), lambda i: (i, 0)),
            _resident((1, d)),
            _resident((d, f)),
        ],
        out_specs=pl.BlockSpec((tm, f), lambda i: (i, 0)),
        out_shape=jax.ShapeDtypeStruct((n, f), BF16),
        compiler_params=_params(1),
        name=name,
    )(x, g, w)


def _even_proj_kernel(h_ref, g_ref, w_ref, wkr_ref, gcq_ref, wq_ref, gckv_ref, wkv_ref, cos_ref, sin_ref,
                      dq_ref, dk_ref, dv_ref, mq_ref, mk_ref, mv_ref):
    tm = h_ref.shape[0]
    xn = _rms(h_ref[...], g_ref[...]).astype(BF16)
    proj = _dot(xn, w_ref[...])
    key_rope = _dot(xn, wkr_ref[...])
    lane = lax.broadcasted_iota(jnp.int32, (tm, LANES), 1)
    low = lane < DIFF_DK
    frame_in_block = lax.broadcasted_iota(jnp.int32, (tm, LANES), 0).astype(F32)
    ones = jnp.where(lane < DIFF_DK + 3, 1.0, 0.0)
    for h in range(DIFF_HEADS):
        b = frame_in_block * _alibi_slope(h)
        b_hi = b.astype(BF16).astype(F32)
        b_mid = (b - b_hi).astype(BF16).astype(F32)
        bias = jnp.where(lane == DIFF_DK, b_hi, jnp.where(lane == DIFF_DK + 1, b_mid, b - b_hi - b_mid))
        bias = jnp.where(lane < DIFF_DK + 3, bias, 0.0)
        q_pair = proj[:, h * 128:(h + 1) * 128] * DIFF_DK ** -0.5
        k_pair = proj[:, 512 + h * 128:512 + (h + 1) * 128]
        for half, (q, k) in enumerate(((q_pair, k_pair), (pltpu.roll(q_pair, DIFF_DK, 1), pltpu.roll(k_pair, DIFF_DK, 1)))):
            cols = slice(h * 256 + half * 128, h * 256 + (half + 1) * 128)
            dq_ref[:, cols] = jnp.where(low, q, ones).astype(BF16)
            dk_ref[:, cols] = jnp.where(low, k, bias).astype(BF16)
    dv_ref[...] = proj[:, 1024:1536].T.astype(BF16)

    cos = cos_ref[...]
    sin = sin_ref[...]
    k_rope = (key_rope[:, :128] * cos + key_rope[:, 128:] * sin).astype(BF16)

    cqn = _rms(proj[:, 1536:1792], gcq_ref[...]).astype(BF16)
    qm = _dot(cqn, wq_ref[...])
    ckvn = _rms(proj[:, 1792:1920], gckv_ref[...]).astype(BF16)
    kv = _dot(ckvn, wkv_ref[...])
    for h in range(MLA_HEADS):
        mq_ref[:, h * 256:h * 256 + 128] = qm[:, h * 128:(h + 1) * 128].astype(BF16)
        q_rope = qm[:, 512 + h * 128:512 + (h + 1) * 128] * cos + qm[:, 1024 + h * 128:1024 + (h + 1) * 128] * sin
        mq_ref[:, h * 256 + 128:(h + 1) * 256] = q_rope.astype(BF16)
        mk_ref[:, h * 256:h * 256 + 128] = kv[:, h * 256:h * 256 + 128].astype(BF16)
        mk_ref[:, h * 256 + 128:(h + 1) * 256] = k_rope
        mv_ref[h * 128:(h + 1) * 128, :] = kv[:, h * 256 + 128:(h + 1) * 256].T.astype(BF16)


def _alibi_slope(h):
    return 2.0 ** (-8.0 * (h + 1) / DIFF_HEADS)


def _rotated(w):
    half = w.shape[1] // 2
    return jnp.concatenate([-w[:, half:], w[:, :half]], axis=1)


def _pad_cols(w, width):
    return jnp.pad(w, ((0, 0), (0, width - w.shape[1])))


def _even_proj(h, g, w_in, g_cq, w_uq, g_ckv, w_ukv, cos_pad, sin_pad, *, seq, tm):
    n, d = h.shape
    kr = w_in[:, 1920:1984]
    w_main = w_in[:, :1920].astype(BF16)
    w_kr = jnp.concatenate([_pad_cols(kr, 128), _pad_cols(_rotated(kr), 128)], axis=1).astype(BF16)
    per_head = MLA_NOPE + MLA_ROPE
    nope = [w_uq[:, hh * per_head:hh * per_head + MLA_NOPE] for hh in range(MLA_HEADS)]
    rope = [w_uq[:, hh * per_head + MLA_NOPE:(hh + 1) * per_head] for hh in range(MLA_HEADS)]
    wq_ext = jnp.concatenate(
        nope + [_pad_cols(r, 128) for r in rope] + [_pad_cols(_rotated(r), 128) for r in rope], axis=1).astype(BF16)
    pos_blocks = seq // tm
    bf = lambda width: jax.ShapeDtypeStruct((n, width), BF16)
    row = lambda width: pl.BlockSpec((tm, width), lambda i: (i, 0))
    vt_shape = jax.ShapeDtypeStruct((n // seq, pos_blocks, 512, tm), BF16)
    vt_spec = pl.BlockSpec((None, None, 512, tm), lambda i: (i // pos_blocks, i % pos_blocks, 0, 0))
    return pl.pallas_call(
        _even_proj_kernel,
        grid=(n // tm,),
        in_specs=[
            row(d),
            _resident((1, d)),
            _resident(w_main.shape),
            _resident(w_kr.shape),
            _resident((1, g_cq.shape[-1])),
            _resident(wq_ext.shape),
            _resident((1, g_ckv.shape[-1])),
            _resident(w_ukv.shape),
            pl.BlockSpec((tm, 128), lambda i: (i % pos_blocks, 0)),
            pl.BlockSpec((tm, 128), lambda i: (i % pos_blocks, 0)),
        ],
        out_specs=[row(1024), row(1024), vt_spec, row(1024), row(1024), vt_spec],
        out_shape=[bf(1024), bf(1024), vt_shape, bf(1024), bf(1024), vt_shape],
        compiler_params=_params(1),
        name="even_proj",
    )(h, g.reshape(1, -1), w_main, w_kr, g_cq.reshape(1, -1), wq_ext, g_ckv.reshape(1, -1), w_ukv.astype(BF16),
      cos_pad, sin_pad)


class _Softmax:
    def __init__(self, scratch, values, block_shift, diagonal_bias):
        s0, s1, self.p, self.a, self.m, self.acc = scratch
        self.s = (s0, s1)
        self.values, self.block_shift, self.diagonal_bias = values, block_shift, diagonal_bias

    def reset(self):
        self.m[...] = jnp.full(self.m.shape, NEG_BIG, F32)
        self.acc[...] = jnp.zeros(self.acc.shape, F32)
        self.p[...] = jnp.zeros(self.p.shape, BF16)
        self.a[...] = jnp.ones(self.a.shape, F32)

    def weigh(self, cur, scale, bias, shift):
        m_prev = self.m[...]
        if bias is None:
            m_block = jnp.max(self.s[cur][...], axis=0, keepdims=True) * scale
            m_new = jnp.maximum(m_prev, m_block if shift is None else m_block + shift)
            p = jnp.exp2(self.s[cur][...] * scale - (m_new if shift is None else m_new - shift))
        else:
            logits = self.s[cur][...] * scale + bias
            m_block = jnp.max(logits, axis=0, keepdims=True)
            m_new = jnp.maximum(m_prev, m_block if shift is None else m_block + shift)
            p = jnp.exp2(logits - (m_new if shift is None else m_new - shift))
        self.m[...] = m_new
        self.p[...] = p.astype(BF16)
        self.a[...] = jnp.exp2(m_prev - m_new)

    def start_values(self, j):
        vt = self.values(j)
        vt_ones = jnp.concatenate([vt, jnp.ones((ONES_ROWS, vt.shape[1]), BF16)], axis=0)
        return self.a[...], _dot(vt_ones, self.p[...])

    def fold(self, alpha, pv):
        self.acc[...] = alpha * self.acc[...] + pv

    def normalized(self):
        return self.acc[:DV, :] / self.acc[DV:DV + 1, :]


def _run_key_blocks(i, sms, scores, q_ref, q_next_ref, scale):
    @pl.when(i == 0)
    def _():
        for sm, s in zip(sms, scores(q_ref, 0)):
            sm.s[0][...] = s

    for sm in sms:
        sm.reset()

    def iteration(j, cur):
        previous = jnp.maximum(j - 1, 0)
        next_scores = scores(q_ref, j + 1)
        scores_first = cur == 1 and len(sms) == 1
        if scores_first:
            sms[0].s[1 - cur][...] = next(next_scores)
        pending = [sm.start_values(previous) for sm in sms]
        if not scores_first:
            for sm in sms:
                sm.s[1 - cur][...] = next(next_scores)
        for sm in sms:
            sm.weigh(cur, scale, None, sm.block_shift(j))
        for sm, pend in zip(sms, pending):
            sm.fold(*pend)

    def pair(jj, carry):
        iteration(2 * jj, 0)
        iteration(2 * jj + 1, 1)
        return carry

    lax.fori_loop(0, i >> 1, pair, 0)

    @pl.when((i & 1) == 1)
    def _():
        iteration(i - 1, 0)
        for sm in sms:
            sm.s[0][...] = sm.s[1][...]

    pending = [sm.start_values(jnp.maximum(i - 1, 0)) for sm in sms]
    for sm in sms:
        sm.weigh(0, scale, sm.diagonal_bias(), sm.block_shift(i))
    last = [sm.start_values(i) for sm in sms]
    for sm, s in zip(sms, scores(q_next_ref, 0)):
        sm.s[0][...] = s
    for sm, pend in zip(sms, pending):
        sm.fold(*pend)
    for sm, pend in zip(sms, last):
        sm.fold(*pend)


def _next_query_spec(blocks, t, width):
    return pl.BlockSpec((None, t, width), lambda bi, g, i: (bi, jnp.minimum(i + 1, blocks - 1), g))


def _softmax_scratch(t):
    return [pltpu.VMEM((t, t), F32), pltpu.VMEM((t, t), F32), pltpu.VMEM((t, t), BF16), pltpu.VMEM((1, t), F32),
            pltpu.VMEM((1, t), F32), pltpu.VMEM((DV + ONES_ROWS, t), F32)]


def _key_query_iotas(t):
    return lax.broadcasted_iota(jnp.int32, (t, t), 0), lax.broadcasted_iota(jnp.int32, (t, t), 1)


def _diff_attn_kernel(q_ref, q_next_ref, k_ref, vt_ref, slope_ref, lq1_ref, lk1_ref, lq2_ref, lk2_ref, gsub_ref,
                      o_ref, bias_diag, *scratch, t, heads, lambda_init):
    i = pl.program_id(2)

    @pl.when(i == 0)
    def _():
        key, qry = _key_query_iotas(t)
        ahead = jnp.maximum(key - qry, 0).astype(F32)
        allowed = (key >> CHUNK_SHIFT) <= (qry >> CHUNK_SHIFT)
        for hh in range(heads):
            bias_diag[hh] = jnp.where(allowed, (-2.0 * LOG2E) * slope_ref[hh][:, :1] * ahead, NEG_BIG)

    sms = []
    for hh in range(heads):
        for half in range(2):
            n = 2 * hh + half
            sms.append(_Softmax(
                scratch[6 * n:6 * n + 6],
                values=lambda j, hh=hh: vt_ref[j, hh * DV:(hh + 1) * DV, :],
                block_shift=lambda j, hh=hh: (LOG2E * t) * slope_ref[hh][:, :1] * j.astype(F32),
                diagonal_bias=lambda hh=hh: bias_diag[hh]))

    def scores(q, j):
        rows = pl.ds(pl.multiple_of(j * t, t), t)
        for n in range(2 * heads):
            cols = slice(n * 128, (n + 1) * 128)
            yield _dot_nt(k_ref[rows, cols], q[:, cols])

    _run_key_blocks(i, sms, scores, q_ref, q_next_ref, LOG2E)

    lam = (jnp.exp(jnp.sum(lq1_ref[...] * lk1_ref[...], axis=-1, keepdims=True))
           - jnp.exp(jnp.sum(lq2_ref[...] * lk2_ref[...], axis=-1, keepdims=True)) + lambda_init)
    for hh in range(heads):
        o = (sms[2 * hh].normalized() - lam * sms[2 * hh + 1].normalized()).T
        o_ref[:, hh * DV:(hh + 1) * DV] = (_rms(o, gsub_ref[...]) * (1.0 - lambda_init)).astype(o_ref.dtype)


def _diff_attn(dq, dk, dv, lq1, lk1, lq2, lk2, g_sub, lambda_init, *, t, heads):
    b, s, _ = dq.shape
    slopes = 2.0 ** (-8.0 * jnp.arange(1, DIFF_HEADS + 1, dtype=F32) / DIFF_HEADS)
    slopes = jnp.broadcast_to(slopes[:, None, None], (DIFF_HEADS, 1, LANES))
    vec = lambda a: a.reshape(1, -1)
    small = lambda width: pl.BlockSpec((1, width), lambda bi, g, i: (0, 0))
    kern = functools.partial(_diff_attn_kernel, t=t, heads=heads, lambda_init=lambda_init)
    return pl.pallas_call(
        kern,
        grid=(b, DIFF_HEADS // heads, s // t),
        in_specs=[
            pl.BlockSpec((None, t, 256 * heads), lambda bi, g, i: (bi, i, g)),
            _next_query_spec(s // t, t, 256 * heads),
            pl.BlockSpec((None, s, 256 * heads), lambda bi, g, i: (bi, 0, g)),
            pl.BlockSpec((None, s // t, DV * heads, t), lambda bi, g, i: (bi, 0, g, 0)),
            pl.BlockSpec((heads, 1, LANES), lambda bi, g, i: (g, 0, 0)),
            small(DIFF_DK), small(DIFF_DK), small(DIFF_DK), small(DIFF_DK),
            small(2 * DIFF_DK),
        ],
        out_specs=pl.BlockSpec((None, t, DV * heads), lambda bi, g, i: (bi, i, g)),
        out_shape=jax.ShapeDtypeStruct((b, s, DIFF_HEADS * DV), BF16),
        scratch_shapes=[pltpu.VMEM((heads, t, t), F32)] + _softmax_scratch(t) * (2 * heads),
        compiler_params=_params(3),
        name="diff_attn",
    )(dq, dq, dk, dv, slopes, vec(lq1), vec(lk1), vec(lq2), vec(lk2), vec(g_sub))


def _mla_attn_kernel(q_ref, q_next_ref, k_ref, vt_ref, o_ref, mask_diag, *scratch, t, heads):
    i = pl.program_id(2)

    @pl.when(i == 0)
    def _():
        key, qry = _key_query_iotas(t)
        mask_diag[...] = jnp.where((key >> CHUNK_SHIFT) <= (qry >> CHUNK_SHIFT), 0.0, NEG_BIG)

    sms = [_Softmax(scratch[6 * hh:6 * hh + 6],
                    values=lambda j, hh=hh: vt_ref[j, hh * DV:(hh + 1) * DV, :],
                    block_shift=lambda j: None,
                    diagonal_bias=lambda: mask_diag[...]) for hh in range(heads)]

    def scores(q, j):
        rows = pl.ds(pl.multiple_of(j * t, t), t)
        for hh in range(heads):
            cols = slice(hh * 256, (hh + 1) * 256)
            yield _dot_nt(k_ref[rows, cols], q[:, cols])

    _run_key_blocks(i, sms, scores, q_ref, q_next_ref, MLA_SCALE * LOG2E)
    for hh in range(heads):
        o_ref[:, hh * DV:(hh + 1) * DV] = sms[hh].normalized().T.astype(o_ref.dtype)


def _mla_attn(mq, mk, mv, *, t, heads):
    b, s, _ = mq.shape
    return pl.pallas_call(
        functools.partial(_mla_attn_kernel, t=t, heads=heads),
        grid=(b, MLA_HEADS // heads, s // t),
        in_specs=[
            pl.BlockSpec((None, t, 256 * heads), lambda bi, g, i: (bi, i, g)),
            _next_query_spec(s // t, t, 256 * heads),
            pl.BlockSpec((None, s, 256 * heads), lambda bi, g, i: (bi, 0, g)),
            pl.BlockSpec((None, s // t, DV * heads, t), lambda bi, g, i: (bi, 0, g, 0)),
        ],
        out_specs=pl.BlockSpec((None, t, DV * heads), lambda bi, g, i: (bi, i, g)),
        out_shape=jax.ShapeDtypeStruct((b, s, MLA_HEADS * DV), BF16),
        scratch_shapes=[pltpu.VMEM((t, t), F32)] + _softmax_scratch(t) * heads,
        compiler_params=_params(3),
        name="mla_attn",
    )(mq, mq, mk, mv)


def _sb_attn_kernel(q_ref, k_ref, v_ref, o_ref, carry_ref, acc_ref, *, t, pairs):
    i = pl.program_id(2)
    heads = 2 * pairs
    low = lax.broadcasted_iota(jnp.int32, (t, LANES), 1) < SB_DH
    q_heads = []
    for p in range(pairs):
        q = q_ref[:, p * LANES:(p + 1) * LANES]
        zero = jnp.zeros_like(q)
        q_heads += [jnp.where(low, q, zero), jnp.where(low, zero, q)]
    row, col = _key_query_iotas(t)
    later = (row > col).astype(BF16)
    strict = col < row

    def block(ref, j, h):
        return ref[pl.ds(pl.multiple_of(j * t, t), t), (h // 2) * LANES:(h // 2 + 1) * LANES]

    def keep_sums(z, mask):
        neg_z = -z
        log_keep = jnp.minimum(neg_z, 0.0) - jnp.log(1.0 + jnp.exp(jnp.minimum(z, neg_z)))
        if mask is not None:
            log_keep = jnp.where(mask, log_keep, 0.0)
        sums = _dot(log_keep.astype(BF16), later)
        return z + log_keep, jnp.sum(log_keep, axis=-1, keepdims=True), sums

    def weights(log_beta, sums, carry, mask):
        between = sums if carry is None else sums + carry
        a = jnp.exp(log_beta + between)
        if mask is not None:
            a = jnp.where(mask, a, 0.0)
        return a.astype(BF16)

    def near(with_previous):
        back = (0, 1) if with_previous else (0,)
        units = [(h, d) for h in range(heads) for d in back]

        def scores(u):
            h, d = units[u]
            return _dot_nt(q_heads[h], block(k_ref, i - d, h))

        def finish(u, staged, pending):
            h, d = units[u]
            log_beta, total, sums = staged[u]
            if d == 0:
                pv = _dot(weights(log_beta, sums, None, strict), block(v_ref, i, h))
                carry = total
            else:
                carry_diag, pv_diag = pending.pop(h)
                pv = pv_diag + _dot(weights(log_beta, sums, carry_diag, None), block(v_ref, i - 1, h))
                carry = carry_diag + total
            if d == back[-1]:
                acc_ref[h] = pv
                carry_ref[h] = carry
            else:
                pending[h] = (carry, pv)

        z = {u: scores(u) for u in range(min(2, len(units)))}
        staged, pending = {}, {}
        for u in range(len(units)):
            staged[u] = keep_sums(z.pop(u), strict if units[u][1] == 0 else None)
            if u + 2 < len(units):
                z[u + 2] = scores(u + 2)
            if u >= 2:
                finish(u - 2, staged, pending)
        for u in range(max(len(units) - 2, 0), len(units)):
            finish(u, staged, pending)

    def alive():
        return jnp.max(carry_ref[...]) > EXP_UNDERFLOW

    def far(state):
        j = i - 2 - state[0]
        z = [_dot_nt(q_heads[h], block(k_ref, j, h)) for h in range(heads)]
        staged = [keep_sums(z[h], None) for h in range(heads)]
        for h in range(heads):
            log_beta, total, sums = staged[h]
            acc_ref[h] += _dot(weights(log_beta, sums, carry_ref[h], None), block(v_ref, j, h))
            carry_ref[h] += total
        return state[0] + 1, alive()

    @pl.when(i == 0)
    def _():
        near(False)

    @pl.when(i > 0)
    def _():
        near(True)
        lax.while_loop(lambda state: jnp.logical_and(state[0] < i - 1, state[1]), far, (jnp.int32(0), alive()))

    for p in range(pairs):
        o_ref[:, p * LANES:(p + 1) * LANES] = jnp.where(low, acc_ref[2 * p], acc_ref[2 * p + 1]).astype(o_ref.dtype)


def _sb_attn(qkv, *, t, pairs):
    b, s, _ = qkv.shape
    groups = SB_HEADS * SB_DH // (LANES * pairs)
    width = LANES * pairs
    return pl.pallas_call(
        functools.partial(_sb_attn_kernel, t=t, pairs=pairs),
        grid=(b, groups, s // t),
        in_specs=[
            pl.BlockSpec((None, t, width), lambda bi, g, i: (bi, i, g)),
            pl.BlockSpec((None, s, width), lambda bi, g, i: (bi, 0, groups + g)),
            pl.BlockSpec((None, s, width), lambda bi, g, i: (bi, 0, 2 * groups + g)),
        ],
        out_specs=pl.BlockSpec((None, t, width), lambda bi, g, i: (bi, i, g)),
        out_shape=jax.ShapeDtypeStruct((b, s, SB_HEADS * SB_DH), BF16),
        scratch_shapes=[pltpu.VMEM((2 * pairs, t, 1), F32), pltpu.VMEM((2 * pairs, t, LANES), F32)],
        compiler_params=_params(3),
        name="sb_attn",
    )(qkv, qkv, qkv)


def _cross_kernel(h_ref, mix_a_ref, mix_b_ref, wmix_ref, g_ref, wq_ref, kv_ref, wo_ref, out_ref, o_scr):
    d = h_ref.shape[1]
    dh = d // XA_HEADS
    half = mix_a_ref.shape[1]
    h1 = h_ref[...] + (_dot(mix_a_ref[...], wmix_ref[:half, :]) + _dot(mix_b_ref[...], wmix_ref[half:, :]))
    q = _dot(_rms(h1, g_ref[...]).astype(BF16), wq_ref[...]).astype(BF16)
    for hd in range(XA_HEADS):
        s = _dot_nt(q[:, hd * dh:(hd + 1) * dh], kv_ref[:, hd * dh:(hd + 1) * dh])
        p = jnp.exp(s - jnp.max(s, axis=-1, keepdims=True))
        o = _dot(p.astype(BF16), kv_ref[:, d + hd * dh:d + (hd + 1) * dh]) / jnp.sum(p, axis=-1, keepdims=True)
        o_scr[:, hd * dh:(hd + 1) * dh] = o.astype(BF16)
    out_ref[...] = h1 + _dot(o_scr[...], wo_ref[...])


def _cross(h, mix_a, mix_b, w_mix, g, wq, mem_kv, wo, *, seq, tm):
    n, d = h.shape
    blocks = seq // tm
    mem_len = mem_kv.shape[1]
    dh = d // XA_HEADS
    half = w_mix.shape[0] // 2
    (mix_a, col_a), (mix_b, col_b) = mix_a, mix_b
    return pl.pallas_call(
        _cross_kernel,
        grid=(n // tm,),
        in_specs=[
            pl.BlockSpec((tm, d), lambda i: (i, 0)),
            pl.BlockSpec((tm, half), lambda i: (i, col_a)),
            pl.BlockSpec((tm, half), lambda i: (i, col_b)),
            _resident(w_mix.shape),
            _resident((1, d)),
            _resident(wq.shape),
            pl.BlockSpec((None, mem_len, 2 * d), lambda i: (i // blocks, 0, 0)),
            _resident(wo.shape),
        ],
        out_specs=pl.BlockSpec((tm, d), lambda i: (i, 0)),
        out_shape=jax.ShapeDtypeStruct((n, d), F32),
        scratch_shapes=[pltpu.VMEM((tm, d), BF16)],
        compiler_params=_params(1),
        name="cross_attn",
    )(h, mix_a, mix_b, w_mix.astype(BF16), g.reshape(1, -1), (wq * dh ** -0.5).astype(BF16), mem_kv, wo.astype(BF16))


def _mlp_kernel(h_ref, g_ref, w1_ref, w2_ref, gf_ref, out_ref, *, tf, final_norm):
    x = h_ref[...]
    xn = _rms(x, g_ref[...]).astype(BF16)
    acc = x
    for c in range(w1_ref.shape[1] // tf):
        a = jnp.maximum(_dot(xn, w1_ref[:, c * tf:(c + 1) * tf]), 0.0)
        acc = acc + _dot((a * a).astype(BF16), w2_ref[c * tf:(c + 1) * tf, :])
    if final_norm:
        acc = _rms(acc, gf_ref[...])
    out_ref[...] = acc


def _mlp(h, g, w1, w2, g_final, *, tm, tf, final_norm):
    n, d = h.shape
    return pl.pallas_call(
        functools.partial(_mlp_kernel, tf=tf, final_norm=final_norm),
        grid=(n // tm,),
        in_specs=[
            pl.BlockSpec((tm, d), lambda i: (i, 0)),
            _resident((1, d)),
            _resident(w1.shape),
            _resident(w2.shape),
            _resident((1, d)),
        ],
        out_specs=pl.BlockSpec((tm, d), lambda i: (i, 0)),
        out_shape=jax.ShapeDtypeStruct((n, d), F32),
        compiler_params=_params(1),
        name="mlp",
    )(h, g.reshape(1, -1), w1.astype(BF16), w2.astype(BF16), g_final.reshape(1, -1))


def _rope_tables(seq):
    half = MLA_ROPE // 2
    inv = (ROPE_BASE ** (-np.arange(half, dtype=np.float32) / half)).astype(np.float32)
    ang = np.arange(seq, dtype=np.float32)[:, None] * inv[None, :]
    pad = lambda a: np.concatenate([a, a, np.zeros((seq, LANES - 2 * half), np.float32)], axis=1).astype(np.float32)
    return jnp.asarray(pad(np.cos(ang))), jnp.asarray(pad(np.sin(ang)))


def kernel(x, mem, ev_norm, ev_w_in, diff_lq1, diff_lk1, diff_lq2, diff_lk2, diff_subln, mla_g_cq, mla_w_uq, mla_g_ckv, mla_w_ukv, ev_w_out, od_norm, sb_w_in, sb_w_out, xa_norm, xa_mem_norm, xa_wq, xa_wkv, xa_wo, mlp_norm, mlp_w1, mlp_w2, final_norm):
    b, seq, d = x.shape
    mem_len = mem.shape[1]
    depth = xa_norm.shape[0]
    tm = ROW_TILE
    h = x.reshape(b * seq, d)
    mem2 = mem.reshape(b * mem_len, d)
    cos_pad, sin_pad = _rope_tables(seq)
    for i in range(depth):
        j = i // 2
        if i % 2 == 0:
            lambda_init = 0.8 - 0.6 * math.exp(-0.3 * i)
            dq, dk, dv, mq, mk, mv = _even_proj(
                h, ev_norm[j], ev_w_in[j], mla_g_cq[j], mla_w_uq[j], mla_g_ckv[j], mla_w_ukv[j],
                cos_pad, sin_pad, seq=seq, tm=tm)
            split = lambda a: a.reshape(b, seq, a.shape[-1])
            o_a = _diff_attn(split(dq), split(dk), dv, diff_lq1[j], diff_lk1[j], diff_lq2[j], diff_lk2[j],
                             diff_subln[j], lambda_init, t=tm, heads=DIFF_HEADS_PER_STEP)
            o_b = _mla_attn(split(mq), split(mk), mv, t=tm, heads=MLA_HEADS_PER_STEP)
            mix_a, mix_b = (o_a.reshape(b * seq, -1), 0), (o_b.reshape(b * seq, -1), 0)
            w_mix = ev_w_out[j]
        else:
            qkv = _norm_matmul(h, od_norm[j].reshape(1, -1), sb_w_in[j].astype(BF16), tm=tm, name="sb_proj",
                               scaled_cols=SB_HEADS * SB_DH, scale=SB_DH ** -0.5)
            mix = _sb_attn(qkv.reshape(b, seq, -1), t=SB_BLOCK, pairs=SB_PAIRS_PER_STEP).reshape(b * seq, -1)
            mix_a, mix_b = (mix, 0), (mix, 1)
            w_mix = sb_w_out[j]
        mem_kv = _norm_matmul(mem2, xa_mem_norm[i].reshape(1, -1), xa_wkv[i].astype(BF16), tm=mem_len, name="mem_kv")
        h = _cross(h, mix_a, mix_b, w_mix, xa_norm[i], xa_wq[i], mem_kv.reshape(b, mem_len, -1), xa_wo[i], seq=seq, tm=tm)
        h = _mlp(h, mlp_norm[i], mlp_w1[i], mlp_w2[i], final_norm, tm=tm, tf=MLP_FF_TILE, final_norm=(i == depth - 1))
    return h.reshape(b, seq, d)
```

```python
import functools
import math

import jax
import jax.numpy as jnp
import numpy as np
from jax import lax
from jax.experimental import pallas as pl
from jax.experimental.pallas import tpu as pltpu

F32 = jnp.float32
BF16 = jnp.bfloat16

EPS = 1e-6
CHUNK_SHIFT = 6
DIFF_HEADS = 4
DIFF_DK = 64
MLA_HEADS = 4
MLA_NOPE = 128
MLA_ROPE = 64
MLA_SCALE = (MLA_NOPE + MLA_ROPE) ** -0.5
ROPE_BASE = 10000.0
SB_HEADS = 16
SB_DH = 64
XA_HEADS = 4
NEG_BIG = -1e30
LOG2E = math.log2(math.e)
EXP_UNDERFLOW = -104.0
LANES = 128
DV = 128
ONES_ROWS = 16

VMEM_LIMIT_BYTES = 48 * 1024 * 1024

ROW_TILE = 512
MLP_FF_TILE = 1024
DIFF_HEADS_PER_STEP = 2
MLA_HEADS_PER_STEP = 2
SB_BLOCK = 256
SB_PAIRS_PER_STEP = 4


def _params(n_grid):
    return pltpu.CompilerParams(
        dimension_semantics=("arbitrary",) * n_grid, vmem_limit_bytes=VMEM_LIMIT_BYTES
    )


def _resident(shape):
    zeros = (0,) * len(shape)
    return pl.BlockSpec(shape, lambda *_: zeros, pipeline_mode=pl.Buffered(1))


def _rms(x, g):
    ms = jnp.mean(x * x, axis=-1, keepdims=True)
    return x * lax.rsqrt(ms + EPS) * g


def _dot(a, b):
    return jnp.dot(a, b, preferred_element_type=F32)


def _dot_nt(a, b):
    return lax.dot_general(a, b, (((1,), (1,)), ((), ())), preferred_element_type=F32)


def _norm_matmul_kernel(x_ref, g_ref, w_ref, o_ref, *, scaled_cols, scale):
    xn = _rms(x_ref[...], g_ref[...]).astype(BF16)
    if scaled_cols:
        o_ref[:, :scaled_cols] = (_dot(xn, w_ref[:, :scaled_cols]) * scale).astype(o_ref.dtype)
    o_ref[:, scaled_cols:] = _dot(xn, w_ref[:, scaled_cols:]).astype(o_ref.dtype)


def _norm_matmul(x, g, w, *, tm, name, scaled_cols=0, scale=1.0):
    n, d = x.shape
    f = w.shape[1]
    return pl.pallas_call(
        functools.partial(_norm_matmul_kernel, scaled_cols=scaled_cols, scale=scale),
        grid=(n // tm,),
        in_specs=[
            pl.BlockSpec((tm, d), lambda i: (i, 0)),
            _resident((1, d)),
            _resident((d, f)),
        ],
        out_specs=pl.BlockSpec((tm, f), lambda i: (i, 0)),
        out_shape=jax.ShapeDtypeStruct((n, f), BF16),
        compiler_params=_params(1),
        name=name,
    )(x, g, w)


def _even_proj_kernel(h_ref, g_ref, w_ref, wkr_ref, gcq_ref, wq_ref, gckv_ref, wkv_ref, cos_ref, sin_ref,
                      dq_ref, dk_ref, dv_ref, mq_ref, mk_ref, mv_ref):
    tm = h_ref.shape[0]
    xn = _rms(h_ref[...], g_ref[...]).astype(BF16)
    proj = _dot(xn, w_ref[...])
    key_rope = _dot(xn, wkr_ref[...])
    lane = lax.broadcasted_iota(jnp.int32, (tm, LANES), 1)
    low = lane < DIFF_DK
    frame_in_block = lax.broadcasted_iota(jnp.int32, (tm, LANES), 0).astype(F32)
    ones = jnp.where(lane < DIFF_DK + 3, 1.0, 0.0)
    for h in range(DIFF_HEADS):
        b = frame_in_block * _alibi_slope(h)
        b_hi = b.astype(BF16).astype(F32)
        b_mid = (b - b_hi).astype(BF16).astype(F32)
        bias = jnp.where(lane == DIFF_DK, b_hi, jnp.where(lane == DIFF_DK + 1, b_mid, b - b_hi - b_mid))
        bias = jnp.where(lane < DIFF_DK + 3, bias, 0.0)
        q_pair = proj[:, h * 128:(h + 1) * 128] * DIFF_DK ** -0.5
        k_pair = proj[:, 512 + h * 128:512 + (h + 1) * 128]
        for half, (q, k) in enumerate(((q_pair, k_pair), (pltpu.roll(q_pair, DIFF_DK, 1), pltpu.roll(k_pair, DIFF_DK, 1)))):
            cols = slice(h * 256 + half * 128, h * 256 + (half + 1) * 128)
            dq_ref[:, cols] = jnp.where(low, q, ones).astype(BF16)
            dk_ref[:, cols] = jnp.where(low, k, bias).astype(BF16)
    dv_ref[...] = proj[:, 1024:1536].T.astype(BF16)

    cos = cos_ref[...]
    sin = sin_ref[...]
    k_rope = (key_rope[:, :128] * cos + key_rope[:, 128:] * sin).astype(BF16)

    cqn = _rms(proj[:, 1536:1792], gcq_ref[...]).astype(BF16)
    qm = _dot(cqn, wq_ref[...])
    ckvn = _rms(proj[:, 1792:1920], gckv_ref[...]).astype(BF16)
    kv = _dot(ckvn, wkv_ref[...])
    for h in range(MLA_HEADS):
        mq_ref[:, h * 256:h * 256 + 128] = qm[:, h * 128:(h + 1) * 128].astype(BF16)
        q_rope = qm[:, 512 + h * 128:512 + (h + 1) * 128] * cos + qm[:, 1024 + h * 128:1024 + (h + 1) * 128] * sin
        mq_ref[:, h * 256 + 128:(h + 1) * 256] = q_rope.astype(BF16)
        mk_ref[:, h * 256:h * 256 + 128] = kv[:, h * 256:h * 256 + 128].astype(BF16)
        mk_ref[:, h * 256 + 128:(h + 1) * 256] = k_rope
        mv_ref[h * 128:(h + 1) * 128, :] = kv[:, h * 256 + 128:(h + 1) * 256].T.astype(BF16)


def _alibi_slope(h):
    return 2.0 ** (-8.0 * (h + 1) / DIFF_HEADS)


def _rotated(w):
    half = w.shape[1] // 2
    return jnp.concatenate([-w[:, half:], w[:, :half]], axis=1)


def _pad_cols(w, width):
    return jnp.pad(w, ((0, 0), (0, width - w.shape[1])))


def _even_proj(h, g, w_in, g_cq, w_uq, g_ckv, w_ukv, cos_pad, sin_pad, *, seq, tm):
    n, d = h.shape
    kr = w_in[:, 1920:1984]
    w_main = w_in[:, :1920].astype(BF16)
    w_kr = jnp.concatenate([_pad_cols(kr, 128), _pad_cols(_rotated(kr), 128)], axis=1).astype(BF16)
    per_head = MLA_NOPE + MLA_ROPE
    nope = [w_uq[:, hh * per_head:hh * per_head + MLA_NOPE] for hh in range(MLA_HEADS)]
    rope = [w_uq[:, hh * per_head + MLA_NOPE:(hh + 1) * per_head] for hh in range(MLA_HEADS)]
    wq_ext = jnp.concatenate(
        nope + [_pad_cols(r, 128) for r in rope] + [_pad_cols(_rotated(r), 128) for r in rope], axis=1).astype(BF16)
    pos_blocks = seq // tm
    bf = lambda width: jax.ShapeDtypeStruct((n, width), BF16)
    row = lambda width: pl.BlockSpec((tm, width), lambda i: (i, 0))
    vt_shape = jax.ShapeDtypeStruct((n // seq, pos_blocks, 512, tm), BF16)
    vt_spec = pl.BlockSpec((None, None, 512, tm), lambda i: (i // pos_blocks, i % pos_blocks, 0, 0))
    return pl.pallas_call(
        _even_proj_kernel,
        grid=(n // tm,),
        in_specs=[
            row(d),
            _resident((1, d)),
            _resident(w_main.shape),
            _resident(w_kr.shape),
            _resident((1, g_cq.shape[-1])),
            _resident(wq_ext.shape),
            _resident((1, g_ckv.shape[-1])),
            _resident(w_ukv.shape),
            pl.BlockSpec((tm, 128), lambda i: (i % pos_blocks, 0)),
            pl.BlockSpec((tm, 128), lambda i: (i % pos_blocks, 0)),
        ],
        out_specs=[row(1024), row(1024), vt_spec, row(1024), row(1024), vt_spec],
        out_shape=[bf(1024), bf(1024), vt_shape, bf(1024), bf(1024), vt_shape],
        compiler_params=_params(1),
        name="even_proj",
    )(h, g.reshape(1, -1), w_main, w_kr, g_cq.reshape(1, -1), wq_ext, g_ckv.reshape(1, -1), w_ukv.astype(BF16),
      cos_pad, sin_pad)


class _Softmax:
    def __init__(self, scratch, values, block_shift, diagonal_bias):
        s0, s1, self.p, self.a, self.m, self.acc = scratch
        self.s = (s0, s1)
        self.values, self.block_shift, self.diagonal_bias = values, block_shift, diagonal_bias

    def reset(self):
        self.m[...] = jnp.full(self.m.shape, NEG_BIG, F32)
        self.acc[...] = jnp.zeros(self.acc.shape, F32)
        self.p[...] = jnp.zeros(self.p.shape, BF16)
        self.a[...] = jnp.ones(self.a.shape, F32)

    def weigh(self, cur, scale, bias, shift):
        m_prev = self.m[...]
        if bias is None:
            m_block = jnp.max(self.s[cur][...], axis=0, keepdims=True) * scale
            m_new = jnp.maximum(m_prev, m_block if shift is None else m_block + shift)
            p = jnp.exp2(self.s[cur][...] * scale - (m_new if shift is None else m_new - shift))
        else:
            logits = self.s[cur][...] * scale + bias
            m_block = jnp.max(logits, axis=0, keepdims=True)
            m_new = jnp.maximum(m_prev, m_block if shift is None else m_block + shift)
            p = jnp.exp2(logits - (m_new if shift is None else m_new - shift))
        self.m[...] = m_new
        self.p[...] = p.astype(BF16)
        self.a[...] = jnp.exp2(m_prev - m_new)

    def start_values(self, j):
        vt = self.values(j)
        vt_ones = jnp.concatenate([vt, jnp.ones((ONES_ROWS, vt.shape[1]), BF16)], axis=0)
        return self.a[...], _dot(vt_ones, self.p[...])

    def fold(self, alpha, pv):
        self.acc[...] = alpha * self.acc[...] + pv

    def normalized(self):
        return self.acc[:DV, :] / self.acc[DV:DV + 1, :]


def _run_key_blocks(i, sms, scores, q_ref, q_next_ref, scale):
    @pl.when(i == 0)
    def _():
        for sm, s in zip(sms, scores(q_ref, 0)):
            sm.s[0][...] = s

    for sm in sms:
        sm.reset()

    def iteration(j, cur):
        previous = jnp.maximum(j - 1, 0)
        next_scores = scores(q_ref, j + 1)
        scores_first = cur == 1 and len(sms) == 1
        if scores_first:
            sms[0].s[1 - cur][...] = next(next_scores)
        pending = [sm.start_values(previous) for sm in sms]
        if not scores_first:
            for sm in sms:
                sm.s[1 - cur][...] = next(next_scores)
        for sm in sms:
            sm.weigh(cur, scale, None, sm.block_shift(j))
        for sm, pend in zip(sms, pending):
            sm.fold(*pend)

    def pair(jj, carry):
        iteration(2 * jj, 0)
        iteration(2 * jj + 1, 1)
        return carry

    lax.fori_loop(0, i >> 1, pair, 0)

    @pl.when((i & 1) == 1)
    def _():
        iteration(i - 1, 0)
        for sm in sms:
            sm.s[0][...] = sm.s[1][...]

    pending = [sm.start_values(jnp.maximum(i - 1, 0)) for sm in sms]
    for sm in sms:
        sm.weigh(0, scale, sm.diagonal_bias(), sm.block_shift(i))
    last = [sm.start_values(i) for sm in sms]
    for sm, s in zip(sms, scores(q_next_ref, 0)):
        sm.s[0][...] = s
    for sm, pend in zip(sms, pending):
        sm.fold(*pend)
    for sm, pend in zip(sms, last):
        sm.fold(*pend)


def _next_query_spec(blocks, t, width):
    return pl.BlockSpec((None, t, width), lambda bi, g, i: (bi, jnp.minimum(i + 1, blocks - 1), g))


def _softmax_scratch(t):
    return [pltpu.VMEM((t, t), F32), pltpu.VMEM((t, t), F32), pltpu.VMEM((t, t), BF16), pltpu.VMEM((1, t), F32),
            pltpu.VMEM((1, t), F32), pltpu.VMEM((DV + ONES_ROWS, t), F32)]


def _key_query_iotas(t):
    return lax.broadcasted_iota(jnp.int32, (t, t), 0), lax.broadcasted_iota(jnp.int32, (t, t), 1)


def _diff_attn_kernel(q_ref, q_next_ref, k_ref, vt_ref, slope_ref, lq1_ref, lk1_ref, lq2_ref, lk2_ref, gsub_ref,
                      o_ref, bias_diag, *scratch, t, heads, lambda_init):
    i = pl.program_id(2)

    @pl.when(i == 0)
    def _():
        key, qry = _key_query_iotas(t)
        ahead = jnp.maximum(key - qry, 0).astype(F32)
        allowed = (key >> CHUNK_SHIFT) <= (qry >> CHUNK_SHIFT)
        for hh in range(heads):
            bias_diag[hh] = jnp.where(allowed, (-2.0 * LOG2E) * slope_ref[hh][:, :1] * ahead, NEG_BIG)

    sms = []
    for hh in range(heads):
        for half in range(2):
            n = 2 * hh + half
            sms.append(_Softmax(
                scratch[6 * n:6 * n + 6],
                values=lambda j, hh=hh: vt_ref[j, hh * DV:(hh + 1) * DV, :],
                block_shift=lambda j, hh=hh: (LOG2E * t) * slope_ref[hh][:, :1] * j.astype(F32),
                diagonal_bias=lambda hh=hh: bias_diag[hh]))

    def scores(q, j):
        rows = pl.ds(pl.multiple_of(j * t, t), t)
        for n in range(2 * heads):
            cols = slice(n * 128, (n + 1) * 128)
            yield _dot_nt(k_ref[rows, cols], q[:, cols])

    _run_key_blocks(i, sms, scores, q_ref, q_next_ref, LOG2E)

    lam = (jnp.exp(jnp.sum(lq1_ref[...] * lk1_ref[...], axis=-1, keepdims=True))
           - jnp.exp(jnp.sum(lq2_ref[...] * lk2_ref[...], axis=-1, keepdims=True)) + lambda_init)
    for hh in range(heads):
        o = (sms[2 * hh].normalized() - lam * sms[2 * hh + 1].normalized()).T
        o_ref[:, hh * DV:(hh + 1) * DV] = (_rms(o, gsub_ref[...]) * (1.0 - lambda_init)).astype(o_ref.dtype)


def _diff_attn(dq, dk, dv, lq1, lk1, lq2, lk2, g_sub, lambda_init, *, t, heads):
    b, s, _ = dq.shape
    slopes = 2.0 ** (-8.0 * jnp.arange(1, DIFF_HEADS + 1, dtype=F32) / DIFF_HEADS)
    slopes = jnp.broadcast_to(slopes[:, None, None], (DIFF_HEADS, 1, LANES))
    vec = lambda a: a.reshape(1, -1)
    small = lambda width: pl.BlockSpec((1, width), lambda bi, g, i: (0, 0))
    kern = functools.partial(_diff_attn_kernel, t=t, heads=heads, lambda_init=lambda_init)
    return pl.pallas_call(
        kern,
        grid=(b, DIFF_HEADS // heads, s // t),
        in_specs=[
            pl.BlockSpec((None, t, 256 * heads), lambda bi, g, i: (bi, i, g)),
            _next_query_spec(s // t, t, 256 * heads),
            pl.BlockSpec((None, s, 256 * heads), lambda bi, g, i: (bi, 0, g)),
            pl.BlockSpec((None, s // t, DV * heads, t), lambda bi, g, i: (bi, 0, g, 0)),
            pl.BlockSpec((heads, 1, LANES), lambda bi, g, i: (g, 0, 0)),
            small(DIFF_DK), small(DIFF_DK), small(DIFF_DK), small(DIFF_DK),
            small(2 * DIFF_DK),
        ],
        out_specs=pl.BlockSpec((None, t, DV * heads), lambda bi, g, i: (bi, i, g)),
        out_shape=jax.ShapeDtypeStruct((b, s, DIFF_HEADS * DV), BF16),
        scratch_shapes=[pltpu.VMEM((heads, t, t), F32)] + _softmax_scratch(t) * (2 * heads),
        compiler_params=_params(3),
        name="diff_attn",
    )(dq, dq, dk, dv, slopes, vec(lq1), vec(lk1), vec(lq2), vec(lk2), vec(g_sub))


def _mla_attn_kernel(q_ref, q_next_ref, k_ref, vt_ref, o_ref, mask_diag, *scratch, t, heads):
    i = pl.program_id(2)

    @pl.when(i == 0)
    def _():
        key, qry = _key_query_iotas(t)
        mask_diag[...] = jnp.where((key >> CHUNK_SHIFT) <= (qry >> CHUNK_SHIFT), 0.0, NEG_BIG)

    sms = [_Softmax(scratch[6 * hh:6 * hh + 6],
                    values=lambda j, hh=hh: vt_ref[j, hh * DV:(hh + 1) * DV, :],
                    block_shift=lambda j: None,
                    diagonal_bias=lambda: mask_diag[...]) for hh in range(heads)]

    def scores(q, j):
        rows = pl.ds(pl.multiple_of(j * t, t), t)
        for hh in range(heads):
            cols = slice(hh * 256, (hh + 1) * 256)
            yield _dot_nt(k_ref[rows, cols], q[:, cols])

    _run_key_blocks(i, sms, scores, q_ref, q_next_ref, MLA_SCALE * LOG2E)
    for hh in range(heads):
        o_ref[:, hh * DV:(hh + 1) * DV] = sms[hh].normalized().T.astype(o_ref.dtype)


def _mla_attn(mq, mk, mv, *, t, heads):
    b, s, _ = mq.shape
    return pl.pallas_call(
        functools.partial(_mla_attn_kernel, t=t, heads=heads),
        grid=(b, MLA_HEADS // heads, s // t),
        in_specs=[
            pl.BlockSpec((None, t, 256 * heads), lambda bi, g, i: (bi, i, g)),
            _next_query_spec(s // t, t, 256 * heads),
            pl.BlockSpec((None, s, 256 * heads), lambda bi, g, i: (bi, 0, g)),
            pl.BlockSpec((None, s // t, DV * heads, t), lambda bi, g, i: (bi, 0, g, 0)),
        ],
        out_specs=pl.BlockSpec((None, t, DV * heads), lambda bi, g, i: (bi, i, g)),
        out_shape=jax.ShapeDtypeStruct((b, s, MLA_HEADS * DV), BF16),
        scratch_shapes=[pltpu.VMEM((t, t), F32)] + _softmax_scratch(t) * heads,
        compiler_params=_params(3),
        name="mla_attn",
    )(mq, mq, mk, mv)


def _sb_attn_kernel(q_ref, k_ref, v_ref, o_ref, carry_ref, acc_ref, *, t, pairs):
    i = pl.program_id(2)
    heads = 2 * pairs
    low = lax.broadcasted_iota(jnp.int32, (t, LANES), 1) < SB_DH
    q_heads = []
    for p in range(pairs):
        q = q_ref[:, p * LANES:(p + 1) * LANES]
        zero = jnp.zeros_like(q)
        q_heads += [jnp.where(low, q, zero), jnp.where(low, zero, q)]
    row, col = _key_query_iotas(t)
    later = (row > col).astype(BF16)
    strict = col < row

    def block(ref, j, h):
        return ref[pl.ds(pl.multiple_of(j * t, t), t), (h // 2) * LANES:(h // 2 + 1) * LANES]

    def keep_sums(z, mask):
        neg_z = -z
        log_keep = jnp.minimum(neg_z, 0.0) - jnp.log(1.0 + jnp.exp(jnp.minimum(z, neg_z)))
        if mask is not None:
            log_keep = jnp.where(mask, log_keep, 0.0)
        sums = _dot(log_keep.astype(BF16), later)
        return z + log_keep, jnp.sum(log_keep, axis=-1, keepdims=True), sums

    def weights(log_beta, sums, carry, mask):
        between = sums if carry is None else sums + carry
        a = jnp.exp(log_beta + between)
        if mask is not None:
            a = jnp.where(mask, a, 0.0)
        return a.astype(BF16)

    def near(with_previous):
        back = (0, 1) if with_previous else (0,)
        units = [(h, d) for h in range(heads) for d in back]

        def scores(u):
            h, d = units[u]
            return _dot_nt(q_heads[h], block(k_ref, i - d, h))

        def finish(u, staged, pending):
            h, d = units[u]
            log_beta, total, sums = staged[u]
            if d == 0:
                pv = _dot(weights(log_beta, sums, None, strict), block(v_ref, i, h))
                carry = total
            else:
                carry_diag, pv_diag = pending.pop(h)
                pv = pv_diag + _dot(weights(log_beta, sums, carry_diag, None), block(v_ref, i - 1, h))
                carry = carry_diag + total
            if d == back[-1]:
                acc_ref[h] = pv
                carry_ref[h] = carry
            else:
                pending[h] = (carry, pv)

        z = {u: scores(u) for u in range(min(2, len(units)))}
        staged, pending = {}, {}
        for u in range(len(units)):
            staged[u] = keep_sums(z.pop(u), strict if units[u][1] == 0 else None)
            if u + 2 < len(units):
                z[u + 2] = scores(u + 2)
            if u >= 2:
                finish(u - 2, staged, pending)
        for u in range(max(len(units) - 2, 0), len(units)):
            finish(u, staged, pending)

    def alive():
        return jnp.max(carry_ref[...]) > EXP_UNDERFLOW

    def far(state):
        j = i - 2 - state[0]
        z = [_dot_nt(q_heads[h], block(k_ref, j, h)) for h in range(heads)]
        staged = [keep_sums(z[h], None) for h in range(heads)]
        for h in range(heads):
            log_beta, total, sums = staged[h]
            acc_ref[h] += _dot(weights(log_beta, sums, carry_ref[h], None), block(v_ref, j, h))
            carry_ref[h] += total
        return state[0] + 1, alive()

    @pl.when(i == 0)
    def _():
        near(False)

    @pl.when(i > 0)
    def _():
        near(True)
        lax.while_loop(lambda state: jnp.logical_and(state[0] < i - 1, state[1]), far, (jnp.int32(0), alive()))

    for p in range(pairs):
        o_ref[:, p * LANES:(p + 1) * LANES] = jnp.where(low, acc_ref[2 * p], acc_ref[2 * p + 1]).astype(o_ref.dtype)


def _sb_attn(qkv, *, t, pairs):
    b, s, _ = qkv.shape
    groups = SB_HEADS * SB_DH // (LANES * pairs)
    width = LANES * pairs
    return pl.pallas_call(
        functools.partial(_sb_attn_kernel, t=t, pairs=pairs),
        grid=(b, groups, s // t),
        in_specs=[
            pl.BlockSpec((None, t, width), lambda bi, g, i: (bi, i, g)),
            pl.BlockSpec((None, s, width), lambda bi, g, i: (bi, 0, groups + g)),
            pl.BlockSpec((None, s, width), lambda bi, g, i: (bi, 0, 2 * groups + g)),
        ],
        out_specs=pl.BlockSpec((None, t, width), lambda bi, g, i: (bi, i, g)),
        out_shape=jax.ShapeDtypeStruct((b, s, SB_HEADS * SB_DH), BF16),
        scratch_shapes=[pltpu.VMEM((2 * pairs, t, 1), F32), pltpu.VMEM((2 * pairs, t, LANES), F32)],
        compiler_params=_params(3),
        name="sb_attn",
    )(qkv, qkv, qkv)


def _cross_kernel(h_ref, mix_a_ref, mix_b_ref, wmix_ref, g_ref, wq_ref, kv_ref, wo_ref, out_ref, o_scr):
    d = h_ref.shape[1]
    dh = d // XA_HEADS
    half = mix_a_ref.shape[1]
    h1 = h_ref[...] + (_dot(mix_a_ref[...], wmix_ref[:half, :]) + _dot(mix_b_ref[...], wmix_ref[half:, :]))
    q = _dot(_rms(h1, g_ref[...]).astype(BF16), wq_ref[...]).astype(BF16)
    for hd in range(XA_HEADS):
        s = _dot_nt(q[:, hd * dh:(hd + 1) * dh], kv_ref[:, hd * dh:(hd + 1) * dh])
        p = jnp.exp(s - jnp.max(s, axis=-1, keepdims=True))
        o = _dot(p.astype(BF16), kv_ref[:, d + hd * dh:d + (hd + 1) * dh]) / jnp.sum(p, axis=-1, keepdims=True)
        o_scr[:, hd * dh:(hd + 1) * dh] = o.astype(BF16)
    out_ref[...] = h1 + _dot(o_scr[...], wo_ref[...])


def _cross(h, mix_a, mix_b, w_mix, g, wq, mem_kv, wo, *, seq, tm):
    n, d = h.shape
    blocks = seq // tm
    mem_len = mem_kv.shape[1]
    dh = d // XA_HEADS
    half = w_mix.shape[0] // 2
    (mix_a, col_a), (mix_b, col_b) = mix_a, mix_b
    return pl.pallas_call(
        _cross_kernel,
        grid=(n // tm,),
        in_specs=[
            pl.BlockSpec((tm, d), lambda i: (i, 0)),
            pl.BlockSpec((tm, half), lambda i: (i, col_a)),
            pl.BlockSpec((tm, half), lambda i: (i, col_b)),
            _resident(w_mix.shape),
            _resident((1, d)),
            _resident(wq.shape),
            pl.BlockSpec((None, mem_len, 2 * d), lambda i: (i // blocks, 0, 0)),
            _resident(wo.shape),
        ],
        out_specs=pl.BlockSpec((tm, d), lambda i: (i, 0)),
        out_shape=jax.ShapeDtypeStruct((n, d), F32),
        scratch_shapes=[pltpu.VMEM((tm, d), BF16)],
        compiler_params=_params(1),
        name="cross_attn",
    )(h, mix_a, mix_b, w_mix.astype(BF16), g.reshape(1, -1), (wq * dh ** -0.5).astype(BF16), mem_kv, wo.astype(BF16))


def _mlp_kernel(h_ref, g_ref, w1_ref, w2_ref, gf_ref, out_ref, *, tf, final_norm):
    x = h_ref[...]
    xn = _rms(x, g_ref[...]).astype(BF16)
    acc = x
    for c in range(w1_ref.shape[1] // tf):
        a = jnp.maximum(_dot(xn, w1_ref[:, c * tf:(c + 1) * tf]), 0.0)
        acc = acc + _dot((a * a).astype(BF16), w2_ref[c * tf:(c + 1) * tf, :])
    if final_norm:
        acc = _rms(acc, gf_ref[...])
    out_ref[...] = acc


def _mlp(h, g, w1, w2, g_final, *, tm, tf, final_norm):
    n, d = h.shape
    return pl.pallas_call(
        functools.partial(_mlp_kernel, tf=tf, final_norm=final_norm),
        grid=(n // tm,),
        in_specs=[
            pl.BlockSpec((tm, d), lambda i: (i, 0)),
            _resident((1, d)),
            _resident(w1.shape),
            _resident(w2.shape),
            _resident((1, d)),
        ],
        out_specs=pl.BlockSpec((tm, d), lambda i: (i, 0)),
        out_shape=jax.ShapeDtypeStruct((n, d), F32),
        compiler_params=_params(1),
        name="mlp",
    )(h, g.reshape(1, -1), w1.astype(BF16), w2.astype(BF16), g_final.reshape(1, -1))


def _rope_tables(seq):
    half = MLA_ROPE // 2
    inv = (ROPE_BASE ** (-np.arange(half, dtype=np.float32) / half)).astype(np.float32)
    ang = np.arange(seq, dtype=np.float32)[:, None] * inv[None, :]
    pad = lambda a: np.concatenate([a, a, np.zeros((seq, LANES - 2 * half), np.float32)], axis=1).astype(np.float32)
    return jnp.asarray(pad(np.cos(ang))), jnp.asarray(pad(np.sin(ang)))


def kernel(x, mem, ev_norm, ev_w_in, diff_lq1, diff_lk1, diff_lq2, diff_lk2, diff_subln, mla_g_cq, mla_w_uq, mla_g_ckv, mla_w_ukv, ev_w_out, od_norm, sb_w_in, sb_w_out, xa_norm, xa_mem_norm, xa_wq, xa_wkv, xa_wo, mlp_norm, mlp_w1, mlp_w2, final_norm):
    b, seq, d = x.shape
    mem_len = mem.shape[1]
    depth = xa_norm.shape[0]
    tm = ROW_TILE
    h = x.reshape(b * seq, d)
    mem2 = mem.reshape(b * mem_len, d)
    cos_pad, sin_pad = _rope_tables(seq)
    for i in range(depth):
        j = i // 2
        if i % 2 == 0:
            lambda_init = 0.8 - 0.6 * math.exp(-0.3 * i)
            dq, dk, dv, mq, mk, mv = _even_proj(
                h, ev_norm[j], ev_w_in[j], mla_g_cq[j], mla_w_uq[j], mla_g_ckv[j], mla_w_ukv[j],
                cos_pad, sin_pad, seq=seq, tm=tm)
            split = lambda a: a.reshape(b, seq, a.shape[-1])
            o_a = _diff_attn(split(dq), split(dk), dv, diff_lq1[j], diff_lk1[j], diff_lq2[j], diff_lk2[j],
                             diff_subln[j], lambda_init, t=tm, heads=DIFF_HEADS_PER_STEP)
            o_b = _mla_attn(split(mq), split(mk), mv, t=tm, heads=MLA_HEADS_PER_STEP)
            mix_a, mix_b = (o_a.reshape(b * seq, -1), 0), (o_b.reshape(b * seq, -1), 0)
            w_mix = ev_w_out[j]
        else:
            qkv = _norm_matmul(h, od_norm[j].reshape(1, -1), sb_w_in[j].astype(BF16), tm=tm, name="sb_proj",
                               scaled_cols=SB_HEADS * SB_DH, scale=SB_DH ** -0.5)
            mix = _sb_attn(qkv.reshape(b, seq, -1), t=SB_BLOCK, pairs=SB_PAIRS_PER_STEP).reshape(b * seq, -1)
            mix_a, mix_b = (mix, 0), (mix, 1)
            w_mix = sb_w_out[j]
        mem_kv = _norm_matmul(mem2, xa_mem_norm[i].reshape(1, -1), xa_wkv[i].astype(BF16), tm=mem_len, name="mem_kv")
        h = _cross(h, mix_a, mix_b, w_mix, xa_norm[i], xa_wq[i], mem_kv.reshape(b, mem_len, -1), xa_wo[i], seq=seq, tm=tm)
        h = _mlp(h, mlp_norm[i], mlp_w1[i], mlp_w2[i], final_norm, tm=tm, tf=MLP_FF_TILE, final_norm=(i == depth - 1))
    return h.reshape(b, seq, d)
```

```python
import functools
import math

import jax
import jax.numpy as jnp
import numpy as np
from jax import lax
from jax.experimental import pallas as pl
from jax.experimental.pallas import tpu as pltpu

F32 = jnp.float32
BF16 = jnp.bfloat16

EPS = 1e-6
CHUNK_SHIFT = 6
DIFF_HEADS = 4
DIFF_DK = 64
MLA_HEADS = 4
MLA_NOPE = 128
MLA_ROPE = 64
MLA_SCALE = (MLA_NOPE + MLA_ROPE) ** -0.5
ROPE_BASE = 10000.0
SB_HEADS = 16
SB_DH = 64
XA_HEADS = 4
NEG_BIG = -1e30
LOG2E = math.log2(math.e)
EXP_UNDERFLOW = -104.0
LANES = 128
DV = 128
ONES_ROWS = 16

VMEM_LIMIT_BYTES = 48 * 1024 * 1024

ROW_TILE = 512
MLP_FF_TILE = 1024
DIFF_HEADS_PER_STEP = 2
MLA_HEADS_PER_STEP = 2
SB_BLOCK = 256
SB_PAIRS_PER_STEP = 4


def _params(n_grid):
    return pltpu.CompilerParams(
        dimension_semantics=("arbitrary",) * n_grid, vmem_limit_bytes=VMEM_LIMIT_BYTES
    )


def _resident(shape):
    zeros = (0,) * len(shape)
    return pl.BlockSpec(shape, lambda *_: zeros, pipeline_mode=pl.Buffered(1))


def _rms(x, g):
    ms = jnp.mean(x * x, axis=-1, keepdims=True)
    return x * lax.rsqrt(ms + EPS) * g


def _dot(a, b):
    return jnp.dot(a, b, preferred_element_type=F32)


def _dot_nt(a, b):
    return lax.dot_general(a, b, (((1,), (1,)), ((), ())), preferred_element_type=F32)


def _norm_matmul_kernel(x_ref, g_ref, w_ref, o_ref, *, scaled_cols, scale):
    xn = _rms(x_ref[...], g_ref[...]).astype(BF16)
    if scaled_cols:
        o_ref[:, :scaled_cols] = (_dot(xn, w_ref[:, :scaled_cols]) * scale).astype(o_ref.dtype)
    o_ref[:, scaled_cols:] = _dot(xn, w_ref[:, scaled_cols:]).astype(o_ref.dtype)


def _norm_matmul(x, g, w, *, tm, name, scaled_cols=0, scale=1.0):
    n, d = x.shape
    f = w.shape[1]
    return pl.pallas_call(
        functools.partial(_norm_matmul_kernel, scaled_cols=scaled_cols, scale=scale),
        grid=(n // tm,),
        in_specs=[
            pl.BlockSpec((tm, d), lambda i: (i, 0)),
            _resident((1, d)),
            _resident((d, f)),
        ],
        out_specs=pl.BlockSpec((tm, f), lambda i: (i, 0)),
        out_shape=jax.ShapeDtypeStruct((n, f), BF16),
        compiler_params=_params(1),
        name=name,
    )(x, g, w)


def _even_proj_kernel(h_ref, g_ref, w_ref, wkr_ref, gcq_ref, wq_ref, gckv_ref, wkv_ref, cos_ref, sin_ref,
                      dq_ref, dk_ref, dv_ref, mq_ref, mk_ref, mv_ref):
    tm = h_ref.shape[0]
    xn = _rms(h_ref[...], g_ref[...]).astype(BF16)
    proj = _dot(xn, w_ref[...])
    key_rope = _dot(xn, wkr_ref[...])
    lane = lax.broadcasted_iota(jnp.int32, (tm, LANES), 1)
    low = lane < DIFF_DK
    frame_in_block = lax.broadcasted_iota(jnp.int32, (tm, LANES), 0).astype(F32)
    ones = jnp.where(lane < DIFF_DK + 3, 1.0, 0.0)
    for h in range(DIFF_HEADS):
        b = frame_in_block * _alibi_slope(h)
        b_hi = b.astype(BF16).astype(F32)
        b_mid = (b - b_hi).astype(BF16).astype(F32)
        bias = jnp.where(lane == DIFF_DK, b_hi, jnp.where(lane == DIFF_DK + 1, b_mid, b - b_hi - b_mid))
        bias = jnp.where(lane < DIFF_DK + 3, bias, 0.0)
        q_pair = proj[:, h * 128:(h + 1) * 128] * DIFF_DK ** -0.5
        k_pair = proj[:, 512 + h * 128:512 + (h + 1) * 128]
        for half, (q, k) in enumerate(((q_pair, k_pair), (pltpu.roll(q_pair, DIFF_DK, 1), pltpu.roll(k_pair, DIFF_DK, 1)))):
            cols = slice(h * 256 + half * 128, h * 256 + (half + 1) * 128)
            dq_ref[:, cols] = jnp.where(low, q, ones).astype(BF16)
            dk_ref[:, cols] = jnp.where(low, k, bias).astype(BF16)
    dv_ref[...] = proj[:, 1024:1536].T.astype(BF16)

    cos = cos_ref[...]
    sin = sin_ref[...]
    k_rope = (key_rope[:, :128] * cos + key_rope[:, 128:] * sin).astype(BF16)

    cqn = _rms(proj[:, 1536:1792], gcq_ref[...]).astype(BF16)
    qm = _dot(cqn, wq_ref[...])
    ckvn = _rms(proj[:, 1792:1920], gckv_ref[...]).astype(BF16)
    kv = _dot(ckvn, wkv_ref[...])
    for h in range(MLA_HEADS):
        mq_ref[:, h * 256:h * 256 + 128] = qm[:, h * 128:(h + 1) * 128].astype(BF16)
        q_rope = qm[:, 512 + h * 128:512 + (h + 1) * 128] * cos + qm[:, 1024 + h * 128:1024 + (h + 1) * 128] * sin
        mq_ref[:, h * 256 + 128:(h + 1) * 256] = q_rope.astype(BF16)
        mk_ref[:, h * 256:h * 256 + 128] = kv[:, h * 256:h * 256 + 128].astype(BF16)
        mk_ref[:, h * 256 + 128:(h + 1) * 256] = k_rope
        mv_ref[h * 128:(h + 1) * 128, :] = kv[:, h * 256 + 128:(h + 1) * 256].T.astype(BF16)


def _alibi_slope(h):
    return 2.0 ** (-8.0 * (h + 1) / DIFF_HEADS)


def _rotated(w):
    half = w.shape[1] // 2
    return jnp.concatenate([-w[:, half:], w[:, :half]], axis=1)


def _pad_cols(w, width):
    return jnp.pad(w, ((0, 0), (0, width - w.shape[1])))


def _even_proj(h, g, w_in, g_cq, w_uq, g_ckv, w_ukv, cos_pad, sin_pad, *, seq, tm):
    n, d = h.shape
    kr = w_in[:, 1920:1984]
    w_main = w_in[:, :1920].astype(BF16)
    w_kr = jnp.concatenate([_pad_cols(kr, 128), _pad_cols(_rotated(kr), 128)], axis=1).astype(BF16)
    per_head = MLA_NOPE + MLA_ROPE
    nope = [w_uq[:, hh * per_head:hh * per_head + MLA_NOPE] for hh in range(MLA_HEADS)]
    rope = [w_uq[:, hh * per_head + MLA_NOPE:(hh + 1) * per_head] for hh in range(MLA_HEADS)]
    wq_ext = jnp.concatenate(
        nope + [_pad_cols(r, 128) for r in rope] + [_pad_cols(_rotated(r), 128) for r in rope], axis=1).astype(BF16)
    pos_blocks = seq // tm
    bf = lambda width: jax.ShapeDtypeStruct((n, width), BF16)
    row = lambda width: pl.BlockSpec((tm, width), lambda i: (i, 0))
    vt_shape = jax.ShapeDtypeStruct((n // seq, pos_blocks, 512, tm), BF16)
    vt_spec = pl.BlockSpec((None, None, 512, tm), lambda i: (i // pos_blocks, i % pos_blocks, 0, 0))
    return pl.pallas_call(
        _even_proj_kernel,
        grid=(n // tm,),
        in_specs=[
            row(d),
            _resident((1, d)),
            _resident(w_main.shape),
            _resident(w_kr.shape),
            _resident((1, g_cq.shape[-1])),
            _resident(wq_ext.shape),
            _resident((1, g_ckv.shape[-1])),
            _resident(w_ukv.shape),
            pl.BlockSpec((tm, 128), lambda i: (i % pos_blocks, 0)),
            pl.BlockSpec((tm, 128), lambda i: (i % pos_blocks, 0)),
        ],
        out_specs=[row(1024), row(1024), vt_spec, row(1024), row(1024), vt_spec],
        out_shape=[bf(1024), bf(1024), vt_shape, bf(1024), bf(1024), vt_shape],
        compiler_params=_params(1),
        name="even_proj",
    )(h, g.reshape(1, -1), w_main, w_kr, g_cq.reshape(1, -1), wq_ext, g_ckv.reshape(1, -1), w_ukv.astype(BF16),
      cos_pad, sin_pad)


class _Softmax:
    def __init__(self, scratch, values, block_shift, diagonal_bias):
        s0, s1, self.p, self.a, self.m, self.acc = scratch
        self.s = (s0, s1)
        self.values, self.block_shift, self.diagonal_bias = values, block_shift, diagonal_bias

    def reset(self):
        self.m[...] = jnp.full(self.m.shape, NEG_BIG, F32)
        self.acc[...] = jnp.zeros(self.acc.shape, F32)
        self.p[...] = jnp.zeros(self.p.shape, BF16)
        self.a[...] = jnp.ones(self.a.shape, F32)

    def weigh(self, cur, scale, bias, shift):
        m_prev = self.m[...]
        if bias is None:
            m_block = jnp.max(self.s[cur][...], axis=0, keepdims=True) * scale
            m_new = jnp.maximum(m_prev, m_block if shift is None else m_block + shift)
            p = jnp.exp2(self.s[cur][...] * scale - (m_new if shift is None else m_new - shift))
        else:
            logits = self.s[cur][...] * scale + bias
            m_block = jnp.max(logits, axis=0, keepdims=True)
            m_new = jnp.maximum(m_prev, m_block if shift is None else m_block + shift)
            p = jnp.exp2(logits - (m_new if shift is None else m_new - shift))
        self.m[...] = m_new
        self.p[...] = p.astype(BF16)
        self.a[...] = jnp.exp2(m_prev - m_new)

    def start_values(self, j):
        vt = self.values(j)
        vt_ones = jnp.concatenate([vt, jnp.ones((ONES_ROWS, vt.shape[1]), BF16)], axis=0)
        return self.a[...], _dot(vt_ones, self.p[...])

    def fold(self, alpha, pv):
        self.acc[...] = alpha * self.acc[...] + pv

    def normalized(self):
        return self.acc[:DV, :] / self.acc[DV:DV + 1, :]


def _run_key_blocks(i, sms, scores, q_ref, q_next_ref, scale):
    @pl.when(i == 0)
    def _():
        for sm, s in zip(sms, scores(q_ref, 0)):
            sm.s[0][...] = s

    for sm in sms:
        sm.reset()

    def iteration(j, cur):
        pending = [sm.start_values(jnp.maximum(j - 1, 0)) for sm in sms]
        for sm, s in zip(sms, scores(q_ref, j + 1)):
            sm.s[1 - cur][...] = s
        for sm in sms:
            sm.weigh(cur, scale, None, sm.block_shift(j))
        for sm, pend in zip(sms, pending):
            sm.fold(*pend)

    def pair(jj, carry):
        iteration(2 * jj, 0)
        iteration(2 * jj + 1, 1)
        return carry

    lax.fori_loop(0, i >> 1, pair, 0)

    @pl.when((i & 1) == 1)
    def _():
        iteration(i - 1, 0)
        for sm in sms:
            sm.s[0][...] = sm.s[1][...]

    pending = [sm.start_values(jnp.maximum(i - 1, 0)) for sm in sms]
    for sm in sms:
        sm.weigh(0, scale, sm.diagonal_bias(), sm.block_shift(i))
    last = [sm.start_values(i) for sm in sms]
    for sm, s in zip(sms, scores(q_next_ref, 0)):
        sm.s[0][...] = s
    for sm, pend in zip(sms, pending):
        sm.fold(*pend)
    for sm, pend in zip(sms, last):
        sm.fold(*pend)


def _next_query_spec(blocks, t, width):
    return pl.BlockSpec((None, t, width), lambda bi, g, i: (bi, jnp.minimum(i + 1, blocks - 1), g))


def _softmax_scratch(t):
    return [pltpu.VMEM((t, t), F32), pltpu.VMEM((t, t), F32), pltpu.VMEM((t, t), BF16), pltpu.VMEM((1, t), F32),
            pltpu.VMEM((1, t), F32), pltpu.VMEM((DV + ONES_ROWS, t), F32)]


def _key_query_iotas(t):
    return lax.broadcasted_iota(jnp.int32, (t, t), 0), lax.broadcasted_iota(jnp.int32, (t, t), 1)


def _diff_attn_kernel(q_ref, q_next_ref, k_ref, vt_ref, slope_ref, lq1_ref, lk1_ref, lq2_ref, lk2_ref, gsub_ref,
                      o_ref, bias_diag, *scratch, t, heads, lambda_init):
    i = pl.program_id(2)

    @pl.when(i == 0)
    def _():
        key, qry = _key_query_iotas(t)
        ahead = jnp.maximum(key - qry, 0).astype(F32)
        allowed = (key >> CHUNK_SHIFT) <= (qry >> CHUNK_SHIFT)
        for hh in range(heads):
            bias_diag[hh] = jnp.where(allowed, (-2.0 * LOG2E) * slope_ref[hh][:, :1] * ahead, NEG_BIG)

    sms = []
    for hh in range(heads):
        for half in range(2):
            n = 2 * hh + half
            sms.append(_Softmax(
                scratch[6 * n:6 * n + 6],
                values=lambda j, hh=hh: vt_ref[j, hh * DV:(hh + 1) * DV, :],
                block_shift=lambda j, hh=hh: (LOG2E * t) * slope_ref[hh][:, :1] * j.astype(F32),
                diagonal_bias=lambda hh=hh: bias_diag[hh]))

    def scores(q, j):
        rows = pl.ds(pl.multiple_of(j * t, t), t)
        for n in range(2 * heads):
            cols = slice(n * 128, (n + 1) * 128)
            yield _dot_nt(k_ref[rows, cols], q[:, cols])

    _run_key_blocks(i, sms, scores, q_ref, q_next_ref, LOG2E)

    lam = (jnp.exp(jnp.sum(lq1_ref[...] * lk1_ref[...], axis=-1, keepdims=True))
           - jnp.exp(jnp.sum(lq2_ref[...] * lk2_ref[...], axis=-1, keepdims=True)) + lambda_init)
    for hh in range(heads):
        o = (sms[2 * hh].normalized() - lam * sms[2 * hh + 1].normalized()).T
        o_ref[:, hh * DV:(hh + 1) * DV] = (_rms(o, gsub_ref[...]) * (1.0 - lambda_init)).astype(o_ref.dtype)


def _diff_attn(dq, dk, dv, lq1, lk1, lq2, lk2, g_sub, lambda_init, *, t, heads):
    b, s, _ = dq.shape
    slopes = jnp.asarray([_alibi_slope(h) for h in range(DIFF_HEADS)], F32)
    slopes = jnp.broadcast_to(slopes[:, None, None], (DIFF_HEADS, 1, LANES))
    vec = lambda a: a.reshape(1, -1)
    small = lambda width: pl.BlockSpec((1, width), lambda bi, g, i: (0, 0))
    kern = functools.partial(_diff_attn_kernel, t=t, heads=heads, lambda_init=lambda_init)
    return pl.pallas_call(
        kern,
        grid=(b, DIFF_HEADS // heads, s // t),
        in_specs=[
            pl.BlockSpec((None, t, 256 * heads), lambda bi, g, i: (bi, i, g)),
            _next_query_spec(s // t, t, 256 * heads),
            pl.BlockSpec((None, s, 256 * heads), lambda bi, g, i: (bi, 0, g)),
            pl.BlockSpec((None, s // t, DV * heads, t), lambda bi, g, i: (bi, 0, g, 0)),
            pl.BlockSpec((heads, 1, LANES), lambda bi, g, i: (g, 0, 0)),
            small(DIFF_DK), small(DIFF_DK), small(DIFF_DK), small(DIFF_DK),
            small(2 * DIFF_DK),
        ],
        out_specs=pl.BlockSpec((None, t, DV * heads), lambda bi, g, i: (bi, i, g)),
        out_shape=jax.ShapeDtypeStruct((b, s, DIFF_HEADS * DV), BF16),
        scratch_shapes=[pltpu.VMEM((heads, t, t), F32)] + _softmax_scratch(t) * (2 * heads),
        compiler_params=_params(3),
        name="diff_attn",
    )(dq, dq, dk, dv, slopes, vec(lq1), vec(lk1), vec(lq2), vec(lk2), vec(g_sub))


def _mla_attn_kernel(q_ref, q_next_ref, k_ref, vt_ref, o_ref, mask_diag, *scratch, t, heads):
    i = pl.program_id(2)

    @pl.when(i == 0)
    def _():
        key, qry = _key_query_iotas(t)
        mask_diag[...] = jnp.where((key >> CHUNK_SHIFT) <= (qry >> CHUNK_SHIFT), 0.0, NEG_BIG)

    sms = [_Softmax(scratch[6 * hh:6 * hh + 6],
                    values=lambda j, hh=hh: vt_ref[j, hh * DV:(hh + 1) * DV, :],
                    block_shift=lambda j: None,
                    diagonal_bias=lambda: mask_diag[...]) for hh in range(heads)]

    def scores(q, j):
        rows = pl.ds(pl.multiple_of(j * t, t), t)
        for hh in range(heads):
            cols = slice(hh * 256, (hh + 1) * 256)
            yield _dot_nt(k_ref[rows, cols], q[:, cols])

    _run_key_blocks(i, sms, scores, q_ref, q_next_ref, MLA_SCALE * LOG2E)
    for hh in range(heads):
        o_ref[:, hh * DV:(hh + 1) * DV] = sms[hh].normalized().T.astype(o_ref.dtype)


def _mla_attn(mq, mk, mv, *, t, heads):
    b, s, _ = mq.shape
    return pl.pallas_call(
        functools.partial(_mla_attn_kernel, t=t, heads=heads),
        grid=(b, MLA_HEADS // heads, s // t),
        in_specs=[
            pl.BlockSpec((None, t, 256 * heads), lambda bi, g, i: (bi, i, g)),
            _next_query_spec(s // t, t, 256 * heads),
            pl.BlockSpec((None, s, 256 * heads), lambda bi, g, i: (bi, 0, g)),
            pl.BlockSpec((None, s // t, DV * heads, t), lambda bi, g, i: (bi, 0, g, 0)),
        ],
        out_specs=pl.BlockSpec((None, t, DV * heads), lambda bi, g, i: (bi, i, g)),
        out_shape=jax.ShapeDtypeStruct((b, s, MLA_HEADS * DV), BF16),
        scratch_shapes=[pltpu.VMEM((t, t), F32)] + _softmax_scratch(t) * heads,
        compiler_params=_params(3),
        name="mla_attn",
    )(mq, mq, mk, mv)


def _sb_attn_kernel(q_ref, k_ref, v_ref, o_ref, carry_ref, acc_ref, *, t, pairs):
    i = pl.program_id(2)
    heads = 2 * pairs
    low = lax.broadcasted_iota(jnp.int32, (t, LANES), 1) < SB_DH
    q_heads = []
    for p in range(pairs):
        q = q_ref[:, p * LANES:(p + 1) * LANES]
        zero = jnp.zeros_like(q)
        q_heads += [jnp.where(low, q, zero), jnp.where(low, zero, q)]
    row, col = _key_query_iotas(t)
    later = (row > col).astype(BF16)
    strict = col < row

    def block(ref, j, h):
        return ref[pl.ds(pl.multiple_of(j * t, t), t), (h // 2) * LANES:(h // 2 + 1) * LANES]

    def keep_sums(z, mask):
        neg_z = -z
        log_keep = jnp.minimum(neg_z, 0.0) - jnp.log(1.0 + jnp.exp(jnp.minimum(z, neg_z)))
        if mask is not None:
            log_keep = jnp.where(mask, log_keep, 0.0)
        sums = _dot(log_keep.astype(BF16), later)
        return z + log_keep, jnp.sum(log_keep, axis=-1, keepdims=True), sums

    def weights(log_beta, sums, carry, mask):
        between = sums if carry is None else sums + carry
        a = jnp.exp(log_beta + between)
        if mask is not None:
            a = jnp.where(mask, a, 0.0)
        return a.astype(BF16)

    def near(with_previous):
        back = (0, 1) if with_previous else (0,)
        units = [(h, d) for h in range(heads) for d in back]

        def scores(u):
            h, d = units[u]
            return _dot_nt(q_heads[h], block(k_ref, i - d, h))

        def finish(u, staged, pending):
            h, d = units[u]
            log_beta, total, sums = staged[u]
            if d == 0:
                pv = _dot(weights(log_beta, sums, None, strict), block(v_ref, i, h))
                carry = total
            else:
                carry_diag, pv_diag = pending.pop(h)
                pv = pv_diag + _dot(weights(log_beta, sums, carry_diag, None), block(v_ref, i - 1, h))
                carry = carry_diag + total
            if d == back[-1]:
                acc_ref[h] = pv
                carry_ref[h] = carry
            else:
                pending[h] = (carry, pv)

        z = {u: scores(u) for u in range(min(2, len(units)))}
        staged, pending = {}, {}
        for u in range(len(units)):
            staged[u] = keep_sums(z.pop(u), strict if units[u][1] == 0 else None)
            if u + 2 < len(units):
                z[u + 2] = scores(u + 2)
            if u >= 2:
                finish(u - 2, staged, pending)
        for u in range(max(len(units) - 2, 0), len(units)):
            finish(u, staged, pending)

    def alive():
        return jnp.max(carry_ref[...]) > EXP_UNDERFLOW

    def far(state):
        j = i - 2 - state[0]
        z = [_dot_nt(q_heads[h], block(k_ref, j, h)) for h in range(heads)]
        staged = [keep_sums(z[h], None) for h in range(heads)]
        for h in range(heads):
            log_beta, total, sums = staged[h]
            acc_ref[h] += _dot(weights(log_beta, sums, carry_ref[h], None), block(v_ref, j, h))
            carry_ref[h] += total
        return state[0] + 1, alive()

    @pl.when(i == 0)
    def _():
        near(False)

    @pl.when(i > 0)
    def _():
        near(True)
        lax.while_loop(lambda state: jnp.logical_and(state[0] < i - 1, state[1]), far, (jnp.int32(0), alive()))

    for p in range(pairs):
        o_ref[:, p * LANES:(p + 1) * LANES] = jnp.where(low, acc_ref[2 * p], acc_ref[2 * p + 1]).astype(o_ref.dtype)


def _sb_attn(qkv, *, t, pairs):
    b, s, _ = qkv.shape
    groups = SB_HEADS * SB_DH // (LANES * pairs)
    width = LANES * pairs
    return pl.pallas_call(
        functools.partial(_sb_attn_kernel, t=t, pairs=pairs),
        grid=(b, groups, s // t),
        in_specs=[
            pl.BlockSpec((None, t, width), lambda bi, g, i: (bi, i, g)),
            pl.BlockSpec((None, s, width), lambda bi, g, i: (bi, 0, groups + g)),
            pl.BlockSpec((None, s, width), lambda bi, g, i: (bi, 0, 2 * groups + g)),
        ],
        out_specs=pl.BlockSpec((None, t, width), lambda bi, g, i: (bi, i, g)),
        out_shape=jax.ShapeDtypeStruct((b, s, SB_HEADS * SB_DH), BF16),
        scratch_shapes=[pltpu.VMEM((2 * pairs, t, 1), F32), pltpu.VMEM((2 * pairs, t, LANES), F32)],
        compiler_params=_params(3),
        name="sb_attn",
    )(qkv, qkv, qkv)


def _cross_kernel(h_ref, mix_a_ref, mix_b_ref, wmix_ref, g_ref, wq_ref, kv_ref, wo_ref, out_ref, o_scr):
    d = h_ref.shape[1]
    dh = d // XA_HEADS
    half = mix_a_ref.shape[1]
    h1 = h_ref[...] + (_dot(mix_a_ref[...], wmix_ref[:half, :]) + _dot(mix_b_ref[...], wmix_ref[half:, :]))
    q = _dot(_rms(h1, g_ref[...]).astype(BF16), wq_ref[...]).astype(BF16)
    for hd in range(XA_HEADS):
        s = _dot_nt(q[:, hd * dh:(hd + 1) * dh], kv_ref[:, hd * dh:(hd + 1) * dh])
        p = jnp.exp(s - jnp.max(s, axis=-1, keepdims=True))
        o = _dot(p.astype(BF16), kv_ref[:, d + hd * dh:d + (hd + 1) * dh]) / jnp.sum(p, axis=-1, keepdims=True)
        o_scr[:, hd * dh:(hd + 1) * dh] = o.astype(BF16)
    out_ref[...] = h1 + _dot(o_scr[...], wo_ref[...])


def _cross(h, mix_a, mix_b, w_mix, g, wq, mem_kv, wo, *, seq, tm):
    n, d = h.shape
    blocks = seq // tm
    mem_len = mem_kv.shape[1]
    dh = d // XA_HEADS
    half = w_mix.shape[0] // 2
    (mix_a, col_a), (mix_b, col_b) = mix_a, mix_b
    return pl.pallas_call(
        _cross_kernel,
        grid=(n // tm,),
        in_specs=[
            pl.BlockSpec((tm, d), lambda i: (i, 0)),
            pl.BlockSpec((tm, half), lambda i: (i, col_a)),
            pl.BlockSpec((tm, half), lambda i: (i, col_b)),
            _resident(w_mix.shape),
            _resident((1, d)),
            _resident(wq.shape),
            pl.BlockSpec((None, mem_len, 2 * d), lambda i: (i // blocks, 0, 0)),
            _resident(wo.shape),
        ],
        out_specs=pl.BlockSpec((tm, d), lambda i: (i, 0)),
        out_shape=jax.ShapeDtypeStruct((n, d), F32),
        scratch_shapes=[pltpu.VMEM((tm, d), BF16)],
        compiler_params=_params(1),
        name="cross_attn",
    )(h, mix_a, mix_b, w_mix.astype(BF16), g.reshape(1, -1), (wq * dh ** -0.5).astype(BF16), mem_kv, wo.astype(BF16))


def _mlp_kernel(h_ref, g_ref, w1_ref, w2_ref, gf_ref, out_ref, *, tf, final_norm):
    x = h_ref[...]
    xn = _rms(x, g_ref[...]).astype(BF16)
    acc = x
    for c in range(w1_ref.shape[1] // tf):
        a = jnp.maximum(_dot(xn, w1_ref[:, c * tf:(c + 1) * tf]), 0.0)
        acc = acc + _dot((a * a).astype(BF16), w2_ref[c * tf:(c + 1) * tf, :])
    if final_norm:
        acc = _rms(acc, gf_ref[...])
    out_ref[...] = acc


def _mlp(h, g, w1, w2, g_final, *, tm, tf, final_norm):
    n, d = h.shape
    return pl.pallas_call(
        functools.partial(_mlp_kernel, tf=tf, final_norm=final_norm),
        grid=(n // tm,),
        in_specs=[
            pl.BlockSpec((tm, d), lambda i: (i, 0)),
            _resident((1, d)),
            _resident(w1.shape),
            _resident(w2.shape),
            _resident((1, d)),
        ],
        out_specs=pl.BlockSpec((tm, d), lambda i: (i, 0)),
        out_shape=jax.ShapeDtypeStruct((n, d), F32),
        compiler_params=_params(1),
        name="mlp",
    )(h, g.reshape(1, -1), w1.astype(BF16), w2.astype(BF16), g_final.reshape(1, -1))


def _rope_tables(seq):
    half = MLA_ROPE // 2
    inv = (ROPE_BASE ** (-np.arange(half, dtype=np.float32) / half)).astype(np.float32)
    ang = np.arange(seq, dtype=np.float32)[:, None] * inv[None, :]
    pad = lambda a: np.concatenate([a, a, np.zeros((seq, LANES - 2 * half), np.float32)], axis=1).astype(np.float32)
    return jnp.asarray(pad(np.cos(ang))), jnp.asarray(pad(np.sin(ang)))


def kernel(x, mem, ev_norm, ev_w_in, diff_lq1, diff_lk1, diff_lq2, diff_lk2, diff_subln, mla_g_cq, mla_w_uq, mla_g_ckv, mla_w_ukv, ev_w_out, od_norm, sb_w_in, sb_w_out, xa_norm, xa_mem_norm, xa_wq, xa_wkv, xa_wo, mlp_norm, mlp_w1, mlp_w2, final_norm):
    b, seq, d = x.shape
    mem_len = mem.shape[1]
    depth = xa_norm.shape[0]
    tm = ROW_TILE
    h = x.reshape(b * seq, d)
    mem2 = mem.reshape(b * mem_len, d)
    cos_pad, sin_pad = _rope_tables(seq)
    for i in range(depth):
        j = i // 2
        if i % 2 == 0:
            lambda_init = 0.8 - 0.6 * math.exp(-0.3 * i)
            dq, dk, dv, mq, mk, mv = _even_proj(
                h, ev_norm[j], ev_w_in[j], mla_g_cq[j], mla_w_uq[j], mla_g_ckv[j], mla_w_ukv[j],
                cos_pad, sin_pad, seq=seq, tm=tm)
            split = lambda a: a.reshape(b, seq, a.shape[-1])
            o_a = _diff_attn(split(dq), split(dk), dv, diff_lq1[j], diff_lk1[j], diff_lq2[j], diff_lk2[j],
                             diff_subln[j], lambda_init, t=tm, heads=DIFF_HEADS_PER_STEP)
            o_b = _mla_attn(split(mq), split(mk), mv, t=tm, heads=MLA_HEADS_PER_STEP)
            mix_a, mix_b = (o_a.reshape(b * seq, -1), 0), (o_b.reshape(b * seq, -1), 0)
            w_mix = ev_w_out[j]
        else:
            qkv = _norm_matmul(h, od_norm[j].reshape(1, -1), sb_w_in[j].astype(BF16), tm=tm, name="sb_proj",
                               scaled_cols=SB_HEADS * SB_DH, scale=SB_DH ** -0.5)
            mix = _sb_attn(qkv.reshape(b, seq, -1), t=SB_BLOCK, pairs=SB_PAIRS_PER_STEP).reshape(b * seq, -1)
            mix_a, mix_b = (mix, 0), (mix, 1)
            w_mix = sb_w_out[j]
        mem_kv = _norm_matmul(mem2, xa_mem_norm[i].reshape(1, -1), xa_wkv[i].astype(BF16), tm=mem_len, name="mem_kv")
        h = _cross(h, mix_a, mix_b, w_mix, xa_norm[i], xa_wq[i], mem_kv.reshape(b, mem_len, -1), xa_wo[i], seq=seq, tm=tm)
        h = _mlp(h, mlp_norm[i], mlp_w1[i], mlp_w2[i], final_norm, tm=tm, tf=MLP_FF_TILE, final_norm=(i == depth - 1))
    return h.reshape(b, seq, d)
```

```python
import functools
import math

import jax
import jax.numpy as jnp
import numpy as np
from jax import lax
from jax.experimental import pallas as pl
from jax.experimental.pallas import tpu as pltpu

F32 = jnp.float32
BF16 = jnp.bfloat16

EPS = 1e-6
CHUNK_SHIFT = 6
DIFF_HEADS = 4
DIFF_DK = 64
MLA_HEADS = 4
MLA_NOPE = 128
MLA_ROPE = 64
MLA_SCALE = (MLA_NOPE + MLA_ROPE) ** -0.5
ROPE_BASE = 10000.0
SB_HEADS = 16
SB_DH = 64
XA_HEADS = 4
NEG_BIG = -1e30
LOG2E = math.log2(math.e)
EXP_UNDERFLOW = -104.0
LANES = 128
DV = 128
ONES_ROWS = 16

VMEM_LIMIT_BYTES = 48 * 1024 * 1024

ROW_TILE = 512
MLP_FF_TILE = 1024
DIFF_HEADS_PER_STEP = 2
MLA_HEADS_PER_STEP = 2
SB_BLOCK = 256
SB_PAIRS_PER_STEP = 4


def _params(n_grid):
    return pltpu.CompilerParams(
        dimension_semantics=("arbitrary",) * n_grid, vmem_limit_bytes=VMEM_LIMIT_BYTES
    )


def _resident(shape):
    zeros = (0,) * len(shape)
    return pl.BlockSpec(shape, lambda *_: zeros, pipeline_mode=pl.Buffered(1))


def _rms(x, g):
    ms = jnp.mean(x * x, axis=-1, keepdims=True)
    return x * lax.rsqrt(ms + EPS) * g


def _dot(a, b):
    return jnp.dot(a, b, preferred_element_type=F32)


def _dot_nt(a, b):
    return lax.dot_general(a, b, (((1,), (1,)), ((), ())), preferred_element_type=F32)


def _norm_matmul_kernel(x_ref, g_ref, w_ref, o_ref, *, scaled_cols, scale):
    xn = _rms(x_ref[...], g_ref[...]).astype(BF16)
    if scaled_cols:
        o_ref[:, :scaled_cols] = (_dot(xn, w_ref[:, :scaled_cols]) * scale).astype(o_ref.dtype)
    o_ref[:, scaled_cols:] = _dot(xn, w_ref[:, scaled_cols:]).astype(o_ref.dtype)


def _norm_matmul(x, g, w, *, tm, name, scaled_cols=0, scale=1.0):
    n, d = x.shape
    f = w.shape[1]
    return pl.pallas_call(
        functools.partial(_norm_matmul_kernel, scaled_cols=scaled_cols, scale=scale),
        grid=(n // tm,),
        in_specs=[
            pl.BlockSpec((tm, d), lambda i: (i, 0)),
            _resident((1, d)),
            _resident((d, f)),
        ],
        out_specs=pl.BlockSpec((tm, f), lambda i: (i, 0)),
        out_shape=jax.ShapeDtypeStruct((n, f), BF16),
        compiler_params=_params(1),
        name=name,
    )(x, g, w)


def _even_proj_kernel(h_ref, g_ref, w_ref, wkr_ref, gcq_ref, wq_ref, gckv_ref, wkv_ref, cos_ref, sin_ref,
                      dq_ref, dk_ref, dv_ref, mq_ref, mk_ref, mv_ref):
    tm = h_ref.shape[0]
    xn = _rms(h_ref[...], g_ref[...]).astype(BF16)
    proj = _dot(xn, w_ref[...])
    key_rope = _dot(xn, wkr_ref[...])
    lane = lax.broadcasted_iota(jnp.int32, (tm, LANES), 1)
    low = lane < DIFF_DK
    frame_in_block = lax.broadcasted_iota(jnp.int32, (tm, LANES), 0).astype(F32)
    ones = jnp.where(lane < DIFF_DK + 3, 1.0, 0.0)
    for h in range(DIFF_HEADS):
        b = frame_in_block * _alibi_slope(h)
        b_hi = b.astype(BF16).astype(F32)
        b_mid = (b - b_hi).astype(BF16).astype(F32)
        bias = jnp.where(lane == DIFF_DK, b_hi, jnp.where(lane == DIFF_DK + 1, b_mid, b - b_hi - b_mid))
        bias = jnp.where(lane < DIFF_DK + 3, bias, 0.0)
        q_pair = proj[:, h * 128:(h + 1) * 128] * DIFF_DK ** -0.5
        k_pair = proj[:, 512 + h * 128:512 + (h + 1) * 128]
        for half, (q, k) in enumerate(((q_pair, k_pair), (pltpu.roll(q_pair, DIFF_DK, 1), pltpu.roll(k_pair, DIFF_DK, 1)))):
            cols = slice(h * 256 + half * 128, h * 256 + (half + 1) * 128)
            dq_ref[:, cols] = jnp.where(low, q, ones).astype(BF16)
            dk_ref[:, cols] = jnp.where(low, k, bias).astype(BF16)
    dv_ref[...] = proj[:, 1024:1536].T.astype(BF16)

    cos = cos_ref[...]
    sin = sin_ref[...]
    k_rope = (key_rope[:, :128] * cos + key_rope[:, 128:] * sin).astype(BF16)

    cqn = _rms(proj[:, 1536:1792], gcq_ref[...]).astype(BF16)
    qm = _dot(cqn, wq_ref[...])
    ckvn = _rms(proj[:, 1792:1920], gckv_ref[...]).astype(BF16)
    kv = _dot(ckvn, wkv_ref[...])
    for h in range(MLA_HEADS):
        mq_ref[:, h * 256:h * 256 + 128] = qm[:, h * 128:(h + 1) * 128].astype(BF16)
        q_rope = qm[:, 512 + h * 128:512 + (h + 1) * 128] * cos + qm[:, 1024 + h * 128:1024 + (h + 1) * 128] * sin
        mq_ref[:, h * 256 + 128:(h + 1) * 256] = q_rope.astype(BF16)
        mk_ref[:, h * 256:h * 256 + 128] = kv[:, h * 256:h * 256 + 128].astype(BF16)
        mk_ref[:, h * 256 + 128:(h + 1) * 256] = k_rope
        mv_ref[h * 128:(h + 1) * 128, :] = kv[:, h * 256 + 128:(h + 1) * 256].T.astype(BF16)


def _alibi_slope(h):
    return 2.0 ** (-8.0 * (h + 1) / DIFF_HEADS)


def _rotated(w):
    half = w.shape[1] // 2
    return jnp.concatenate([-w[:, half:], w[:, :half]], axis=1)


def _pad_cols(w, width):
    return jnp.pad(w, ((0, 0), (0, width - w.shape[1])))


def _even_proj(h, g, w_in, g_cq, w_uq, g_ckv, w_ukv, cos_pad, sin_pad, *, seq, tm):
    n, d = h.shape
    kr = w_in[:, 1920:1984]
    w_main = w_in[:, :1920].astype(BF16)
    w_kr = jnp.concatenate([_pad_cols(kr, 128), _pad_cols(_rotated(kr), 128)], axis=1).astype(BF16)
    per_head = MLA_NOPE + MLA_ROPE
    nope = [w_uq[:, hh * per_head:hh * per_head + MLA_NOPE] for hh in range(MLA_HEADS)]
    rope = [w_uq[:, hh * per_head + MLA_NOPE:(hh + 1) * per_head] for hh in range(MLA_HEADS)]
    wq_ext = jnp.concatenate(
        nope + [_pad_cols(r, 128) for r in rope] + [_pad_cols(_rotated(r), 128) for r in rope], axis=1).astype(BF16)
    pos_blocks = seq // tm
    bf = lambda width: jax.ShapeDtypeStruct((n, width), BF16)
    row = lambda width: pl.BlockSpec((tm, width), lambda i: (i, 0))
    vt_shape = jax.ShapeDtypeStruct((n // seq, pos_blocks, 512, tm), BF16)
    vt_spec = pl.BlockSpec((None, None, 512, tm), lambda i: (i // pos_blocks, i % pos_blocks, 0, 0))
    return pl.pallas_call(
        _even_proj_kernel,
        grid=(n // tm,),
        in_specs=[
            row(d),
            _resident((1, d)),
            _resident(w_main.shape),
            _resident(w_kr.shape),
            _resident((1, g_cq.shape[-1])),
            _resident(wq_ext.shape),
            _resident((1, g_ckv.shape[-1])),
            _resident(w_ukv.shape),
            pl.BlockSpec((tm, 128), lambda i: (i % pos_blocks, 0)),
            pl.BlockSpec((tm, 128), lambda i: (i % pos_blocks, 0)),
        ],
        out_specs=[row(1024), row(1024), vt_spec, row(1024), row(1024), vt_spec],
        out_shape=[bf(1024), bf(1024), vt_shape, bf(1024), bf(1024), vt_shape],
        compiler_params=_params(1),
        name="even_proj",
    )(h, g.reshape(1, -1), w_main, w_kr, g_cq.reshape(1, -1), wq_ext, g_ckv.reshape(1, -1), w_ukv.astype(BF16),
      cos_pad, sin_pad)


class _Softmax:
    def __init__(self, scratch, values, block_shift, diagonal_bias):
        s0, s1, p, self.a, self.m, self.acc = scratch
        t = self.m.shape[1]
        self.s = (s0.at[:, pl.ds(0, t)], s1.at[:, pl.ds(0, t)])
        self.p = p.at[:, pl.ds(0, t)]
        self.values, self.block_shift, self.diagonal_bias = values, block_shift, diagonal_bias

    def reset(self):
        self.m[...] = jnp.full(self.m.shape, NEG_BIG, F32)
        self.acc[...] = jnp.zeros(self.acc.shape, F32)
        self.p[...] = jnp.zeros(self.p.shape, BF16)
        self.a[...] = jnp.ones(self.a.shape, F32)

    def weigh(self, cur, scale, bias, shift):
        m_prev = self.m[...]
        if bias is None:
            m_block = jnp.max(self.s[cur][...], axis=0, keepdims=True) * scale
            m_new = jnp.maximum(m_prev, m_block if shift is None else m_block + shift)
            p = jnp.exp2(self.s[cur][...] * scale - (m_new if shift is None else m_new - shift))
        else:
            logits = self.s[cur][...] * scale + bias
            m_block = jnp.max(logits, axis=0, keepdims=True)
            m_new = jnp.maximum(m_prev, m_block if shift is None else m_block + shift)
            p = jnp.exp2(logits - (m_new if shift is None else m_new - shift))
        self.m[...] = m_new
        self.p[...] = p.astype(BF16)
        self.a[...] = jnp.exp2(m_prev - m_new)

    def start_values(self, j):
        vt = self.values(j)
        vt_ones = jnp.concatenate([vt, jnp.ones((ONES_ROWS, vt.shape[1]), BF16)], axis=0)
        return self.a[...], _dot(vt_ones, self.p[...])

    def fold(self, alpha, pv):
        self.acc[...] = alpha * self.acc[...] + pv

    def normalized(self):
        return self.acc[:DV, :] / self.acc[DV:DV + 1, :]


def _run_key_blocks(i, sms, scores, q_ref, q_next_ref, scale):
    @pl.when(i == 0)
    def _():
        for sm, s in zip(sms, scores(q_ref, 0)):
            sm.s[0][...] = s

    for sm in sms:
        sm.reset()

    def iteration(j, cur):
        pending = [sm.start_values(jnp.maximum(j - 1, 0)) for sm in sms]
        for sm, s in zip(sms, scores(q_ref, j + 1)):
            sm.s[1 - cur][...] = s
        for sm in sms:
            sm.weigh(cur, scale, None, sm.block_shift(j))
        for sm, pend in zip(sms, pending):
            sm.fold(*pend)

    def pair(jj, carry):
        iteration(2 * jj, 0)
        iteration(2 * jj + 1, 1)
        return carry

    lax.fori_loop(0, i >> 1, pair, 0)

    @pl.when((i & 1) == 1)
    def _():
        iteration(i - 1, 0)
        for sm in sms:
            sm.s[0][...] = sm.s[1][...]

    pending = [sm.start_values(jnp.maximum(i - 1, 0)) for sm in sms]
    for sm in sms:
        sm.weigh(0, scale, sm.diagonal_bias(), sm.block_shift(i))
    last = [sm.start_values(i) for sm in sms]
    for sm, s in zip(sms, scores(q_next_ref, 0)):
        sm.s[0][...] = s
    for sm, pend in zip(sms, pending):
        sm.fold(*pend)
    for sm, pend in zip(sms, last):
        sm.fold(*pend)


def _next_query_spec(blocks, t, width):
    return pl.BlockSpec((None, t, width), lambda bi, g, i: (bi, jnp.minimum(i + 1, blocks - 1), g))


def _softmax_scratch(t):
    wide = t + LANES
    return [pltpu.VMEM((t, wide), F32), pltpu.VMEM((t, wide), F32), pltpu.VMEM((t, wide), BF16),
            pltpu.VMEM((1, t), F32), pltpu.VMEM((1, t), F32), pltpu.VMEM((DV + ONES_ROWS, t), F32)]


def _key_query_iotas(t):
    return lax.broadcasted_iota(jnp.int32, (t, t), 0), lax.broadcasted_iota(jnp.int32, (t, t), 1)


def _diff_attn_kernel(q_ref, q_next_ref, k_ref, vt_ref, slope_ref, lq1_ref, lk1_ref, lq2_ref, lk2_ref, gsub_ref,
                      o_ref, bias_diag, *scratch, t, heads, lambda_init):
    i = pl.program_id(2)

    @pl.when(i == 0)
    def _():
        key, qry = _key_query_iotas(t)
        ahead = jnp.maximum(key - qry, 0).astype(F32)
        allowed = (key >> CHUNK_SHIFT) <= (qry >> CHUNK_SHIFT)
        for hh in range(heads):
            bias_diag[hh] = jnp.where(allowed, (-2.0 * LOG2E) * slope_ref[hh][:, :1] * ahead, NEG_BIG)

    sms = []
    for hh in range(heads):
        for half in range(2):
            n = 2 * hh + half
            sms.append(_Softmax(
                scratch[6 * n:6 * n + 6],
                values=lambda j, hh=hh: vt_ref[j, hh * DV:(hh + 1) * DV, :],
                block_shift=lambda j, hh=hh: (LOG2E * t) * slope_ref[hh][:, :1] * j.astype(F32),
                diagonal_bias=lambda hh=hh: bias_diag[hh]))

    def scores(q, j):
        rows = pl.ds(pl.multiple_of(j * t, t), t)
        for n in range(2 * heads):
            cols = slice(n * 128, (n + 1) * 128)
            yield _dot_nt(k_ref[rows, cols], q[:, cols])

    _run_key_blocks(i, sms, scores, q_ref, q_next_ref, LOG2E)

    lam = (jnp.exp(jnp.sum(lq1_ref[...] * lk1_ref[...], axis=-1, keepdims=True))
           - jnp.exp(jnp.sum(lq2_ref[...] * lk2_ref[...], axis=-1, keepdims=True)) + lambda_init)
    for hh in range(heads):
        o = (sms[2 * hh].normalized() - lam * sms[2 * hh + 1].normalized()).T
        o_ref[:, hh * DV:(hh + 1) * DV] = (_rms(o, gsub_ref[...]) * (1.0 - lambda_init)).astype(o_ref.dtype)


def _diff_attn(dq, dk, dv, lq1, lk1, lq2, lk2, g_sub, lambda_init, *, t, heads):
    b, s, _ = dq.shape
    slopes = jnp.asarray([_alibi_slope(h) for h in range(DIFF_HEADS)], F32)
    slopes = jnp.broadcast_to(slopes[:, None, None], (DIFF_HEADS, 1, LANES))
    vec = lambda a: a.reshape(1, -1)
    small = lambda width: pl.BlockSpec((1, width), lambda bi, g, i: (0, 0))
    kern = functools.partial(_diff_attn_kernel, t=t, heads=heads, lambda_init=lambda_init)
    return pl.pallas_call(
        kern,
        grid=(b, DIFF_HEADS // heads, s // t),
        in_specs=[
            pl.BlockSpec((None, t, 256 * heads), lambda bi, g, i: (bi, i, g)),
            _next_query_spec(s // t, t, 256 * heads),
            pl.BlockSpec((None, s, 256 * heads), lambda bi, g, i: (bi, 0, g)),
            pl.BlockSpec((None, s // t, DV * heads, t), lambda bi, g, i: (bi, 0, g, 0)),
            pl.BlockSpec((heads, 1, LANES), lambda bi, g, i: (g, 0, 0)),
            small(DIFF_DK), small(DIFF_DK), small(DIFF_DK), small(DIFF_DK),
            small(2 * DIFF_DK),
        ],
        out_specs=pl.BlockSpec((None, t, DV * heads), lambda bi, g, i: (bi, i, g)),
        out_shape=jax.ShapeDtypeStruct((b, s, DIFF_HEADS * DV), BF16),
        scratch_shapes=[pltpu.VMEM((heads, t, t), F32)] + _softmax_scratch(t) * (2 * heads),
        compiler_params=_params(3),
        name="diff_attn",
    )(dq, dq, dk, dv, slopes, vec(lq1), vec(lk1), vec(lq2), vec(lk2), vec(g_sub))


def _mla_attn_kernel(q_ref, q_next_ref, k_ref, vt_ref, o_ref, mask_diag, *scratch, t, heads):
    i = pl.program_id(2)

    @pl.when(i == 0)
    def _():
        key, qry = _key_query_iotas(t)
        mask_diag[...] = jnp.where((key >> CHUNK_SHIFT) <= (qry >> CHUNK_SHIFT), 0.0, NEG_BIG)

    sms = [_Softmax(scratch[6 * hh:6 * hh + 6],
                    values=lambda j, hh=hh: vt_ref[j, hh * DV:(hh + 1) * DV, :],
                    block_shift=lambda j: None,
                    diagonal_bias=lambda: mask_diag[...]) for hh in range(heads)]

    def scores(q, j):
        rows = pl.ds(pl.multiple_of(j * t, t), t)
        for hh in range(heads):
            cols = slice(hh * 256, (hh + 1) * 256)
            yield _dot_nt(k_ref[rows, cols], q[:, cols])

    _run_key_blocks(i, sms, scores, q_ref, q_next_ref, MLA_SCALE * LOG2E)
    for hh in range(heads):
        o_ref[:, hh * DV:(hh + 1) * DV] = sms[hh].normalized().T.astype(o_ref.dtype)


def _mla_attn(mq, mk, mv, *, t, heads):
    b, s, _ = mq.shape
    return pl.pallas_call(
        functools.partial(_mla_attn_kernel, t=t, heads=heads),
        grid=(b, MLA_HEADS // heads, s // t),
        in_specs=[
            pl.BlockSpec((None, t, 256 * heads), lambda bi, g, i: (bi, i, g)),
            _next_query_spec(s // t, t, 256 * heads),
            pl.BlockSpec((None, s, 256 * heads), lambda bi, g, i: (bi, 0, g)),
            pl.BlockSpec((None, s // t, DV * heads, t), lambda bi, g, i: (bi, 0, g, 0)),
        ],
        out_specs=pl.BlockSpec((None, t, DV * heads), lambda bi, g, i: (bi, i, g)),
        out_shape=jax.ShapeDtypeStruct((b, s, MLA_HEADS * DV), BF16),
        scratch_shapes=[pltpu.VMEM((t, t), F32)] + _softmax_scratch(t) * heads,
        compiler_params=_params(3),
        name="mla_attn",
    )(mq, mq, mk, mv)


def _sb_attn_kernel(q_ref, k_ref, v_ref, o_ref, carry_ref, acc_ref, *, t, pairs):
    i = pl.program_id(2)
    heads = 2 * pairs
    low = lax.broadcasted_iota(jnp.int32, (t, LANES), 1) < SB_DH
    q_heads = []
    for p in range(pairs):
        q = q_ref[:, p * LANES:(p + 1) * LANES]
        zero = jnp.zeros_like(q)
        q_heads += [jnp.where(low, q, zero), jnp.where(low, zero, q)]
    row, col = _key_query_iotas(t)
    later = (row > col).astype(BF16)
    strict = col < row

    def block(ref, j, h):
        return ref[pl.ds(pl.multiple_of(j * t, t), t), (h // 2) * LANES:(h // 2 + 1) * LANES]

    def keep_sums(z, mask):
        neg_z = -z
        log_keep = jnp.minimum(neg_z, 0.0) - jnp.log(1.0 + jnp.exp(jnp.minimum(z, neg_z)))
        if mask is not None:
            log_keep = jnp.where(mask, log_keep, 0.0)
        sums = _dot(log_keep.astype(BF16), later)
        return z + log_keep, jnp.sum(log_keep, axis=-1, keepdims=True), sums

    def weights(log_beta, sums, carry, mask):
        between = sums if carry is None else sums + carry
        a = jnp.exp(log_beta + between)
        if mask is not None:
            a = jnp.where(mask, a, 0.0)
        return a.astype(BF16)

    def near(with_previous):
        back = (0, 1) if with_previous else (0,)
        units = [(h, d) for h in range(heads) for d in back]

        def scores(u):
            h, d = units[u]
            return _dot_nt(q_heads[h], block(k_ref, i - d, h))

        def finish(u, staged, pending):
            h, d = units[u]
            log_beta, total, sums = staged[u]
            if d == 0:
                pv = _dot(weights(log_beta, sums, None, strict), block(v_ref, i, h))
                carry = total
            else:
                carry_diag, pv_diag = pending.pop(h)
                pv = pv_diag + _dot(weights(log_beta, sums, carry_diag, None), block(v_ref, i - 1, h))
                carry = carry_diag + total
            if d == back[-1]:
                acc_ref[h] = pv
                carry_ref[h] = carry
            else:
                pending[h] = (carry, pv)

        z = {u: scores(u) for u in range(min(2, len(units)))}
        staged, pending = {}, {}
        for u in range(len(units)):
            staged[u] = keep_sums(z.pop(u), strict if units[u][1] == 0 else None)
            if u + 2 < len(units):
                z[u + 2] = scores(u + 2)
            if u >= 2:
                finish(u - 2, staged, pending)
        for u in range(max(len(units) - 2, 0), len(units)):
            finish(u, staged, pending)

    def alive():
        return jnp.max(carry_ref[...]) > EXP_UNDERFLOW

    def far(state):
        j = i - 2 - state[0]
        z = [_dot_nt(q_heads[h], block(k_ref, j, h)) for h in range(heads)]
        staged = [keep_sums(z[h], None) for h in range(heads)]
        for h in range(heads):
            log_beta, total, sums = staged[h]
            acc_ref[h] += _dot(weights(log_beta, sums, carry_ref[h], None), block(v_ref, j, h))
            carry_ref[h] += total
        return state[0] + 1, alive()

    @pl.when(i == 0)
    def _():
        near(False)

    @pl.when(i > 0)
    def _():
        near(True)
        lax.while_loop(lambda state: jnp.logical_and(state[0] < i - 1, state[1]), far, (jnp.int32(0), alive()))

    for p in range(pairs):
        o_ref[:, p * LANES:(p + 1) * LANES] = jnp.where(low, acc_ref[2 * p], acc_ref[2 * p + 1]).astype(o_ref.dtype)


def _sb_attn(qkv, *, t, pairs):
    b, s, _ = qkv.shape
    groups = SB_HEADS * SB_DH // (LANES * pairs)
    width = LANES * pairs
    return pl.pallas_call(
        functools.partial(_sb_attn_kernel, t=t, pairs=pairs),
        grid=(b, groups, s // t),
        in_specs=[
            pl.BlockSpec((None, t, width), lambda bi, g, i: (bi, i, g)),
            pl.BlockSpec((None, s, width), lambda bi, g, i: (bi, 0, groups + g)),
            pl.BlockSpec((None, s, width), lambda bi, g, i: (bi, 0, 2 * groups + g)),
        ],
        out_specs=pl.BlockSpec((None, t, width), lambda bi, g, i: (bi, i, g)),
        out_shape=jax.ShapeDtypeStruct((b, s, SB_HEADS * SB_DH), BF16),
        scratch_shapes=[pltpu.VMEM((2 * pairs, t, 1), F32), pltpu.VMEM((2 * pairs, t, LANES), F32)],
        compiler_params=_params(3),
        name="sb_attn",
    )(qkv, qkv, qkv)


def _cross_kernel(h_ref, mix_a_ref, mix_b_ref, wmix_ref, g_ref, wq_ref, kv_ref, wo_ref, out_ref, o_scr):
    d = h_ref.shape[1]
    dh = d // XA_HEADS
    half = mix_a_ref.shape[1]
    h1 = h_ref[...] + (_dot(mix_a_ref[...], wmix_ref[:half, :]) + _dot(mix_b_ref[...], wmix_ref[half:, :]))
    q = _dot(_rms(h1, g_ref[...]).astype(BF16), wq_ref[...]).astype(BF16)
    for hd in range(XA_HEADS):
        s = _dot_nt(q[:, hd * dh:(hd + 1) * dh], kv_ref[:, hd * dh:(hd + 1) * dh])
        p = jnp.exp(s - jnp.max(s, axis=-1, keepdims=True))
        o = _dot(p.astype(BF16), kv_ref[:, d + hd * dh:d + (hd + 1) * dh]) / jnp.sum(p, axis=-1, keepdims=True)
        o_scr[:, hd * dh:(hd + 1) * dh] = o.astype(BF16)
    out_ref[...] = h1 + _dot(o_scr[...], wo_ref[...])


def _cross(h, mix_a, mix_b, w_mix, g, wq, mem_kv, wo, *, seq, tm):
    n, d = h.shape
    blocks = seq // tm
    mem_len = mem_kv.shape[1]
    dh = d // XA_HEADS
    half = w_mix.shape[0] // 2
    (mix_a, col_a), (mix_b, col_b) = mix_a, mix_b
    return pl.pallas_call(
        _cross_kernel,
        grid=(n // tm,),
        in_specs=[
            pl.BlockSpec((tm, d), lambda i: (i, 0)),
            pl.BlockSpec((tm, half), lambda i: (i, col_a)),
            pl.BlockSpec((tm, half), lambda i: (i, col_b)),
            _resident(w_mix.shape),
            _resident((1, d)),
            _resident(wq.shape),
            pl.BlockSpec((None, mem_len, 2 * d), lambda i: (i // blocks, 0, 0)),
            _resident(wo.shape),
        ],
        out_specs=pl.BlockSpec((tm, d), lambda i: (i, 0)),
        out_shape=jax.ShapeDtypeStruct((n, d), F32),
        scratch_shapes=[pltpu.VMEM((tm, d), BF16)],
        compiler_params=_params(1),
        name="cross_attn",
    )(h, mix_a, mix_b, w_mix.astype(BF16), g.reshape(1, -1), (wq * dh ** -0.5).astype(BF16), mem_kv, wo.astype(BF16))


def _mlp_kernel(h_ref, g_ref, w1_ref, w2_ref, gf_ref, out_ref, *, tf, final_norm):
    x = h_ref[...]
    xn = _rms(x, g_ref[...]).astype(BF16)
    acc = x
    for c in range(w1_ref.shape[1] // tf):
        a = jnp.maximum(_dot(xn, w1_ref[:, c * tf:(c + 1) * tf]), 0.0)
        acc = acc + _dot((a * a).astype(BF16), w2_ref[c * tf:(c + 1) * tf, :])
    if final_norm:
        acc = _rms(acc, gf_ref[...])
    out_ref[...] = acc


def _mlp(h, g, w1, w2, g_final, *, tm, tf, final_norm):
    n, d = h.shape
    return pl.pallas_call(
        functools.partial(_mlp_kernel, tf=tf, final_norm=final_norm),
        grid=(n // tm,),
        in_specs=[
            pl.BlockSpec((tm, d), lambda i: (i, 0)),
            _resident((1, d)),
            _resident(w1.shape),
            _resident(w2.shape),
            _resident((1, d)),
        ],
        out_specs=pl.BlockSpec((tm, d), lambda i: (i, 0)),
        out_shape=jax.ShapeDtypeStruct((n, d), F32),
        compiler_params=_params(1),
        name="mlp",
    )(h, g.reshape(1, -1), w1.astype(BF16), w2.astype(BF16), g_final.reshape(1, -1))


def _rope_tables(seq):
    half = MLA_ROPE // 2
    inv = (ROPE_BASE ** (-np.arange(half, dtype=np.float32) / half)).astype(np.float32)
    ang = np.arange(seq, dtype=np.float32)[:, None] * inv[None, :]
    pad = lambda a: np.concatenate([a, a, np.zeros((seq, LANES - 2 * half), np.float32)], axis=1).astype(np.float32)
    return jnp.asarray(pad(np.cos(ang))), jnp.asarray(pad(np.sin(ang)))


def kernel(x, mem, ev_norm, ev_w_in, diff_lq1, diff_lk1, diff_lq2, diff_lk2, diff_subln, mla_g_cq, mla_w_uq, mla_g_ckv, mla_w_ukv, ev_w_out, od_norm, sb_w_in, sb_w_out, xa_norm, xa_mem_norm, xa_wq, xa_wkv, xa_wo, mlp_norm, mlp_w1, mlp_w2, final_norm):
    b, seq, d = x.shape
    mem_len = mem.shape[1]
    depth = xa_norm.shape[0]
    tm = ROW_TILE
    h = x.reshape(b * seq, d)
    mem2 = mem.reshape(b * mem_len, d)
    cos_pad, sin_pad = _rope_tables(seq)
    for i in range(depth):
        j = i // 2
        if i % 2 == 0:
            lambda_init = 0.8 - 0.6 * math.exp(-0.3 * i)
            dq, dk, dv, mq, mk, mv = _even_proj(
                h, ev_norm[j], ev_w_in[j], mla_g_cq[j], mla_w_uq[j], mla_g_ckv[j], mla_w_ukv[j],
                cos_pad, sin_pad, seq=seq, tm=tm)
            split = lambda a: a.reshape(b, seq, a.shape[-1])
            o_a = _diff_attn(split(dq), split(dk), dv, diff_lq1[j], diff_lk1[j], diff_lq2[j], diff_lk2[j],
                             diff_subln[j], lambda_init, t=tm, heads=DIFF_HEADS_PER_STEP)
            o_b = _mla_attn(split(mq), split(mk), mv, t=tm, heads=MLA_HEADS_PER_STEP)
            mix_a, mix_b = (o_a.reshape(b * seq, -1), 0), (o_b.reshape(b * seq, -1), 0)
            w_mix = ev_w_out[j]
        else:
            qkv = _norm_matmul(h, od_norm[j].reshape(1, -1), sb_w_in[j].astype(BF16), tm=tm, name="sb_proj",
                               scaled_cols=SB_HEADS * SB_DH, scale=SB_DH ** -0.5)
            mix = _sb_attn(qkv.reshape(b, seq, -1), t=SB_BLOCK, pairs=SB_PAIRS_PER_STEP).reshape(b * seq, -1)
            mix_a, mix_b = (mix, 0), (mix, 1)
            w_mix = sb_w_out[j]
        mem_kv = _norm_matmul(mem2, xa_mem_norm[i].reshape(1, -1), xa_wkv[i].astype(BF16), tm=mem_len, name="mem_kv")
        h = _cross(h, mix_a, mix_b, w_mix, xa_norm[i], xa_wq[i], mem_kv.reshape(b, mem_len, -1), xa_wo[i], seq=seq, tm=tm)
        h = _mlp(h, mlp_norm[i], mlp_w1[i], mlp_w2[i], final_norm, tm=tm, tf=MLP_FF_TILE, final_norm=(i == depth - 1))
    return h.reshape(b, seq, d)
```

```python
import functools
import math

import jax
import jax.numpy as jnp
import numpy as np
from jax import lax
from jax.experimental import pallas as pl
from jax.experimental.pallas import tpu as pltpu

F32 = jnp.float32
BF16 = jnp.bfloat16

EPS = 1e-6
CHUNK_SHIFT = 6
DIFF_HEADS = 4
DIFF_DK = 64
MLA_HEADS = 4
MLA_NOPE = 128
MLA_ROPE = 64
MLA_SCALE = (MLA_NOPE + MLA_ROPE) ** -0.5
ROPE_BASE = 10000.0
SB_HEADS = 16
SB_DH = 64
XA_HEADS = 4
NEG_BIG = -1e30
LOG2E = math.log2(math.e)
EXP_UNDERFLOW = -104.0
LANES = 128
DV = 128
ONES_ROWS = 16

VMEM_LIMIT_BYTES = 48 * 1024 * 1024

ROW_TILE = 512
WIDE_ROW_TILE = 1024
MLP_FF_TILE = 1024
DIFF_HEADS_PER_STEP = 2
MLA_HEADS_PER_STEP = 2
SB_BLOCK = 256
SB_PAIRS_PER_STEP = 4


def _params(n_grid):
    return pltpu.CompilerParams(
        dimension_semantics=("arbitrary",) * n_grid, vmem_limit_bytes=VMEM_LIMIT_BYTES
    )


def _resident(shape):
    zeros = (0,) * len(shape)
    return pl.BlockSpec(shape, lambda *_: zeros, pipeline_mode=pl.Buffered(1))


def _rms(x, g):
    ms = jnp.mean(x * x, axis=-1, keepdims=True)
    return x * lax.rsqrt(ms + EPS) * g


def _dot(a, b):
    return jnp.dot(a, b, preferred_element_type=F32)


def _dot_nt(a, b):
    return lax.dot_general(a, b, (((1,), (1,)), ((), ())), preferred_element_type=F32)


def _norm_matmul_kernel(x_ref, g_ref, w_ref, o_ref, *, scaled_cols, scale):
    xn = _rms(x_ref[...], g_ref[...]).astype(BF16)
    if scaled_cols:
        o_ref[:, :scaled_cols] = (_dot(xn, w_ref[:, :scaled_cols]) * scale).astype(o_ref.dtype)
    o_ref[:, scaled_cols:] = _dot(xn, w_ref[:, scaled_cols:]).astype(o_ref.dtype)


def _norm_matmul(x, g, w, *, tm, name, scaled_cols=0, scale=1.0):
    n, d = x.shape
    f = w.shape[1]
    return pl.pallas_call(
        functools.partial(_norm_matmul_kernel, scaled_cols=scaled_cols, scale=scale),
        grid=(n // tm,),
        in_specs=[
            pl.BlockSpec((tm, d), lambda i: (i, 0)),
            _resident((1, d)),
            _resident((d, f)),
        ],
        out_specs=pl.BlockSpec((tm, f), lambda i: (i, 0)),
        out_shape=jax.ShapeDtypeStruct((n, f), BF16),
        compiler_params=_params(1),
        name=name,
    )(x, g, w)


def _even_proj_kernel(h_ref, g_ref, w_ref, wkr_ref, gcq_ref, wq_ref, gckv_ref, wkv_ref, cos_ref, sin_ref,
                      dq_ref, dk_ref, dv_ref, mq_ref, mk_ref, mv_ref):
    tm = h_ref.shape[0]
    xn = _rms(h_ref[...], g_ref[...]).astype(BF16)
    proj = _dot(xn, w_ref[...])
    key_rope = _dot(xn, wkr_ref[...])
    lane = lax.broadcasted_iota(jnp.int32, (tm, LANES), 1)
    low = lane < DIFF_DK
    frame_in_block = lax.broadcasted_iota(jnp.int32, (tm, LANES), 0).astype(F32)
    ones = jnp.where(lane < DIFF_DK + 3, 1.0, 0.0)
    for h in range(DIFF_HEADS):
        b = frame_in_block * _alibi_slope(h)
        b_hi = b.astype(BF16).astype(F32)
        b_mid = (b - b_hi).astype(BF16).astype(F32)
        bias = jnp.where(lane == DIFF_DK, b_hi, jnp.where(lane == DIFF_DK + 1, b_mid, b - b_hi - b_mid))
        bias = jnp.where(lane < DIFF_DK + 3, bias, 0.0)
        q_pair = proj[:, h * 128:(h + 1) * 128] * DIFF_DK ** -0.5
        k_pair = proj[:, 512 + h * 128:512 + (h + 1) * 128]
        for half, (q, k) in enumerate(((q_pair, k_pair), (pltpu.roll(q_pair, DIFF_DK, 1), pltpu.roll(k_pair, DIFF_DK, 1)))):
            cols = slice(h * 256 + half * 128, h * 256 + (half + 1) * 128)
            dq_ref[:, cols] = jnp.where(low, q, ones).astype(BF16)
            dk_ref[:, cols] = jnp.where(low, k, bias).astype(BF16)
    dv_ref[...] = proj[:, 1024:1536].T.astype(BF16)

    cos = cos_ref[...]
    sin = sin_ref[...]
    k_rope = (key_rope[:, :128] * cos + key_rope[:, 128:] * sin).astype(BF16)

    cqn = _rms(proj[:, 1536:1792], gcq_ref[...]).astype(BF16)
    qm = _dot(cqn, wq_ref[...])
    ckvn = _rms(proj[:, 1792:1920], gckv_ref[...]).astype(BF16)
    kv = _dot(ckvn, wkv_ref[...])
    for h in range(MLA_HEADS):
        mq_ref[:, h * 256:h * 256 + 128] = qm[:, h * 128:(h + 1) * 128].astype(BF16)
        q_rope = qm[:, 512 + h * 128:512 + (h + 1) * 128] * cos + qm[:, 1024 + h * 128:1024 + (h + 1) * 128] * sin
        mq_ref[:, h * 256 + 128:(h + 1) * 256] = q_rope.astype(BF16)
        mk_ref[:, h * 256:h * 256 + 128] = kv[:, h * 256:h * 256 + 128].astype(BF16)
        mk_ref[:, h * 256 + 128:(h + 1) * 256] = k_rope
        mv_ref[h * 128:(h + 1) * 128, :] = kv[:, h * 256 + 128:(h + 1) * 256].T.astype(BF16)


def _alibi_slope(h):
    return 2.0 ** (-8.0 * (h + 1) / DIFF_HEADS)


def _rotated(w):
    half = w.shape[1] // 2
    return jnp.concatenate([-w[:, half:], w[:, :half]], axis=1)


def _pad_cols(w, width):
    return jnp.pad(w, ((0, 0), (0, width - w.shape[1])))


def _even_proj(h, g, w_in, g_cq, w_uq, g_ckv, w_ukv, cos_pad, sin_pad, *, seq, tm):
    n, d = h.shape
    kr = w_in[:, 1920:1984]
    w_main = w_in[:, :1920].astype(BF16)
    w_kr = jnp.concatenate([_pad_cols(kr, 128), _pad_cols(_rotated(kr), 128)], axis=1).astype(BF16)
    per_head = MLA_NOPE + MLA_ROPE
    nope = [w_uq[:, hh * per_head:hh * per_head + MLA_NOPE] for hh in range(MLA_HEADS)]
    rope = [w_uq[:, hh * per_head + MLA_NOPE:(hh + 1) * per_head] for hh in range(MLA_HEADS)]
    wq_ext = jnp.concatenate(
        nope + [_pad_cols(r, 128) for r in rope] + [_pad_cols(_rotated(r), 128) for r in rope], axis=1).astype(BF16)
    pos_blocks = seq // tm
    bf = lambda width: jax.ShapeDtypeStruct((n, width), BF16)
    row = lambda width: pl.BlockSpec((tm, width), lambda i: (i, 0))
    vt_shape = jax.ShapeDtypeStruct((n // seq, pos_blocks, 512, tm), BF16)
    vt_spec = pl.BlockSpec((None, None, 512, tm), lambda i: (i // pos_blocks, i % pos_blocks, 0, 0))
    return pl.pallas_call(
        _even_proj_kernel,
        grid=(n // tm,),
        in_specs=[
            row(d),
            _resident((1, d)),
            _resident(w_main.shape),
            _resident(w_kr.shape),
            _resident((1, g_cq.shape[-1])),
            _resident(wq_ext.shape),
            _resident((1, g_ckv.shape[-1])),
            _resident(w_ukv.shape),
            pl.BlockSpec((tm, 128), lambda i: (i % pos_blocks, 0)),
            pl.BlockSpec((tm, 128), lambda i: (i % pos_blocks, 0)),
        ],
        out_specs=[row(1024), row(1024), vt_spec, row(1024), row(1024), vt_spec],
        out_shape=[bf(1024), bf(1024), vt_shape, bf(1024), bf(1024), vt_shape],
        compiler_params=_params(1),
        name="even_proj",
    )(h, g.reshape(1, -1), w_main, w_kr, g_cq.reshape(1, -1), wq_ext, g_ckv.reshape(1, -1), w_ukv.astype(BF16),
      cos_pad, sin_pad)


class _Softmax:
    def __init__(self, scratch, values, block_shift, diagonal_bias):
        s0, s1, self.p, self.a, self.m, self.acc = scratch
        self.s = (s0, s1)
        self.values, self.block_shift, self.diagonal_bias = values, block_shift, diagonal_bias

    def reset(self):
        self.m[...] = jnp.full(self.m.shape, NEG_BIG, F32)
        self.acc[...] = jnp.zeros(self.acc.shape, F32)
        self.p[...] = jnp.zeros(self.p.shape, BF16)
        self.a[...] = jnp.ones(self.a.shape, F32)

    def weigh(self, cur, scale, bias, shift):
        m_prev = self.m[...]
        if bias is None:
            m_block = jnp.max(self.s[cur][...], axis=0, keepdims=True) * scale
            m_new = jnp.maximum(m_prev, m_block if shift is None else m_block + shift)
            p = jnp.exp2(self.s[cur][...] * scale - (m_new if shift is None else m_new - shift))
        else:
            logits = self.s[cur][...] * scale + bias
            m_block = jnp.max(logits, axis=0, keepdims=True)
            m_new = jnp.maximum(m_prev, m_block if shift is None else m_block + shift)
            p = jnp.exp2(logits - (m_new if shift is None else m_new - shift))
        self.m[...] = m_new
        self.p[...] = p.astype(BF16)
        self.a[...] = jnp.exp2(m_prev - m_new)

    def start_values(self, j):
        vt = self.values(j)
        vt_ones = jnp.concatenate([vt, jnp.ones((ONES_ROWS, vt.shape[1]), BF16)], axis=0)
        return self.a[...], _dot(vt_ones, self.p[...])

    def fold(self, alpha, pv):
        self.acc[...] = alpha * self.acc[...] + pv

    def normalized(self):
        return self.acc[:DV, :] / self.acc[DV:DV + 1, :]


def _run_key_blocks(i, sms, scores, q_ref, q_next_ref, scale):
    @pl.when(i == 0)
    def _():
        for sm, s in zip(sms, scores(q_ref, 0)):
            sm.s[0][...] = s

    for sm in sms:
        sm.reset()

    def iteration(j, cur):
        pending = [sm.start_values(jnp.maximum(j - 1, 0)) for sm in sms]
        for sm, s in zip(sms, scores(q_ref, j + 1)):
            sm.s[1 - cur][...] = s
        for sm in sms:
            sm.weigh(cur, scale, None, sm.block_shift(j))
        for sm, pend in zip(sms, pending):
            sm.fold(*pend)

    def pair(jj, carry):
        iteration(2 * jj, 0)
        iteration(2 * jj + 1, 1)
        return carry

    lax.fori_loop(0, i >> 1, pair, 0)

    @pl.when((i & 1) == 1)
    def _():
        iteration(i - 1, 0)
        for sm in sms:
            sm.s[0][...] = sm.s[1][...]

    pending = [sm.start_values(jnp.maximum(i - 1, 0)) for sm in sms]
    for sm in sms:
        sm.weigh(0, scale, sm.diagonal_bias(), sm.block_shift(i))
    last = [sm.start_values(i) for sm in sms]
    for sm, s in zip(sms, scores(q_next_ref, 0)):
        sm.s[0][...] = s
    for sm, pend in zip(sms, pending):
        sm.fold(*pend)
    for sm, pend in zip(sms, last):
        sm.fold(*pend)


def _next_query_spec(blocks, t, width):
    return pl.BlockSpec((None, t, width), lambda bi, g, i: (bi, jnp.minimum(i + 1, blocks - 1), g))


def _softmax_scratch(t):
    return [pltpu.VMEM((t, t), F32), pltpu.VMEM((t, t), F32), pltpu.VMEM((t, t), BF16), pltpu.VMEM((1, t), F32),
            pltpu.VMEM((1, t), F32), pltpu.VMEM((DV + ONES_ROWS, t), F32)]


def _key_query_iotas(t):
    return lax.broadcasted_iota(jnp.int32, (t, t), 0), lax.broadcasted_iota(jnp.int32, (t, t), 1)


def _diff_attn_kernel(q_ref, q_next_ref, k_ref, vt_ref, slope_ref, lq1_ref, lk1_ref, lq2_ref, lk2_ref, gsub_ref,
                      o_ref, bias_diag, *scratch, t, heads, lambda_init):
    i = pl.program_id(2)

    @pl.when(i == 0)
    def _():
        key, qry = _key_query_iotas(t)
        ahead = jnp.maximum(key - qry, 0).astype(F32)
        allowed = (key >> CHUNK_SHIFT) <= (qry >> CHUNK_SHIFT)
        for hh in range(heads):
            bias_diag[hh] = jnp.where(allowed, (-2.0 * LOG2E) * slope_ref[hh][:, :1] * ahead, NEG_BIG)

    sms = []
    for hh in range(heads):
        for half in range(2):
            n = 2 * hh + half
            sms.append(_Softmax(
                scratch[6 * n:6 * n + 6],
                values=lambda j, hh=hh: vt_ref[j, hh * DV:(hh + 1) * DV, :],
                block_shift=lambda j, hh=hh: (LOG2E * t) * slope_ref[hh][:, :1] * j.astype(F32),
                diagonal_bias=lambda hh=hh: bias_diag[hh]))

    def scores(q, j):
        rows = pl.ds(pl.multiple_of(j * t, t), t)
        for n in range(2 * heads):
            cols = slice(n * 128, (n + 1) * 128)
            yield _dot_nt(k_ref[rows, cols], q[:, cols])

    _run_key_blocks(i, sms, scores, q_ref, q_next_ref, LOG2E)

    lam = (jnp.exp(jnp.sum(lq1_ref[...] * lk1_ref[...], axis=-1, keepdims=True))
           - jnp.exp(jnp.sum(lq2_ref[...] * lk2_ref[...], axis=-1, keepdims=True)) + lambda_init)
    for hh in range(heads):
        o = (sms[2 * hh].normalized() - lam * sms[2 * hh + 1].normalized()).T
        o_ref[:, hh * DV:(hh + 1) * DV] = (_rms(o, gsub_ref[...]) * (1.0 - lambda_init)).astype(o_ref.dtype)


def _diff_attn(dq, dk, dv, lq1, lk1, lq2, lk2, g_sub, lambda_init, *, t, heads):
    b, s, _ = dq.shape
    slopes = jnp.asarray([_alibi_slope(h) for h in range(DIFF_HEADS)], F32)
    slopes = jnp.broadcast_to(slopes[:, None, None], (DIFF_HEADS, 1, LANES))
    vec = lambda a: a.reshape(1, -1)
    small = lambda width: pl.BlockSpec((1, width), lambda bi, g, i: (0, 0))
    kern = functools.partial(_diff_attn_kernel, t=t, heads=heads, lambda_init=lambda_init)
    return pl.pallas_call(
        kern,
        grid=(b, DIFF_HEADS // heads, s // t),
        in_specs=[
            pl.BlockSpec((None, t, 256 * heads), lambda bi, g, i: (bi, i, g)),
            _next_query_spec(s // t, t, 256 * heads),
            pl.BlockSpec((None, s, 256 * heads), lambda bi, g, i: (bi, 0, g)),
            pl.BlockSpec((None, s // t, DV * heads, t), lambda bi, g, i: (bi, 0, g, 0)),
            pl.BlockSpec((heads, 1, LANES), lambda bi, g, i: (g, 0, 0)),
            small(DIFF_DK), small(DIFF_DK), small(DIFF_DK), small(DIFF_DK),
            small(2 * DIFF_DK),
        ],
        out_specs=pl.BlockSpec((None, t, DV * heads), lambda bi, g, i: (bi, i, g)),
        out_shape=jax.ShapeDtypeStruct((b, s, DIFF_HEADS * DV), BF16),
        scratch_shapes=[pltpu.VMEM((heads, t, t), F32)] + _softmax_scratch(t) * (2 * heads),
        compiler_params=_params(3),
        name="diff_attn",
    )(dq, dq, dk, dv, slopes, vec(lq1), vec(lk1), vec(lq2), vec(lk2), vec(g_sub))


def _mla_attn_kernel(q_ref, q_next_ref, k_ref, vt_ref, o_ref, mask_diag, *scratch, t, heads):
    i = pl.program_id(2)

    @pl.when(i == 0)
    def _():
        key, qry = _key_query_iotas(t)
        mask_diag[...] = jnp.where((key >> CHUNK_SHIFT) <= (qry >> CHUNK_SHIFT), 0.0, NEG_BIG)

    sms = [_Softmax(scratch[6 * hh:6 * hh + 6],
                    values=lambda j, hh=hh: vt_ref[j, hh * DV:(hh + 1) * DV, :],
                    block_shift=lambda j: None,
                    diagonal_bias=lambda: mask_diag[...]) for hh in range(heads)]

    def scores(q, j):
        rows = pl.ds(pl.multiple_of(j * t, t), t)
        for hh in range(heads):
            cols = slice(hh * 256, (hh + 1) * 256)
            yield _dot_nt(k_ref[rows, cols], q[:, cols])

    _run_key_blocks(i, sms, scores, q_ref, q_next_ref, MLA_SCALE * LOG2E)
    for hh in range(heads):
        o_ref[:, hh * DV:(hh + 1) * DV] = sms[hh].normalized().T.astype(o_ref.dtype)


def _mla_attn(mq, mk, mv, *, t, heads):
    b, s, _ = mq.shape
    return pl.pallas_call(
        functools.partial(_mla_attn_kernel, t=t, heads=heads),
        grid=(b, MLA_HEADS // heads, s // t),
        in_specs=[
            pl.BlockSpec((None, t, 256 * heads), lambda bi, g, i: (bi, i, g)),
            _next_query_spec(s // t, t, 256 * heads),
            pl.BlockSpec((None, s, 256 * heads), lambda bi, g, i: (bi, 0, g)),
            pl.BlockSpec((None, s // t, DV * heads, t), lambda bi, g, i: (bi, 0, g, 0)),
        ],
        out_specs=pl.BlockSpec((None, t, DV * heads), lambda bi, g, i: (bi, i, g)),
        out_shape=jax.ShapeDtypeStruct((b, s, MLA_HEADS * DV), BF16),
        scratch_shapes=[pltpu.VMEM((t, t), F32)] + _softmax_scratch(t) * heads,
        compiler_params=_params(3),
        name="mla_attn",
    )(mq, mq, mk, mv)


def _sb_attn_kernel(q_ref, k_ref, v_ref, o_ref, carry_ref, acc_ref, *, t, pairs):
    i = pl.program_id(2)
    heads = 2 * pairs
    low = lax.broadcasted_iota(jnp.int32, (t, LANES), 1) < SB_DH
    q_heads = []
    for p in range(pairs):
        q = q_ref[:, p * LANES:(p + 1) * LANES]
        zero = jnp.zeros_like(q)
        q_heads += [jnp.where(low, q, zero), jnp.where(low, zero, q)]
    row, col = _key_query_iotas(t)
    later = (row > col).astype(BF16)
    strict = col < row

    def block(ref, j, h):
        return ref[pl.ds(pl.multiple_of(j * t, t), t), (h // 2) * LANES:(h // 2 + 1) * LANES]

    def keep_sums(z, mask):
        neg_z = -z
        log_keep = jnp.minimum(neg_z, 0.0) - jnp.log(1.0 + jnp.exp(jnp.minimum(z, neg_z)))
        if mask is not None:
            log_keep = jnp.where(mask, log_keep, 0.0)
        sums = _dot(log_keep.astype(BF16), later)
        return z + log_keep, jnp.sum(log_keep, axis=-1, keepdims=True), sums

    def weights(log_beta, sums, carry, mask):
        between = sums if carry is None else sums + carry
        a = jnp.exp(log_beta + between)
        if mask is not None:
            a = jnp.where(mask, a, 0.0)
        return a.astype(BF16)

    def near(with_previous):
        back = (0, 1) if with_previous else (0,)
        units = [(h, d) for h in range(heads) for d in back]

        def scores(u):
            h, d = units[u]
            return _dot_nt(q_heads[h], block(k_ref, i - d, h))

        def finish(u, staged, pending):
            h, d = units[u]
            log_beta, total, sums = staged[u]
            if d == 0:
                pv = _dot(weights(log_beta, sums, None, strict), block(v_ref, i, h))
                carry = total
            else:
                carry_diag, pv_diag = pending.pop(h)
                pv = pv_diag + _dot(weights(log_beta, sums, carry_diag, None), block(v_ref, i - 1, h))
                carry = carry_diag + total
            if d == back[-1]:
                acc_ref[h] = pv
                carry_ref[h] = carry
            else:
                pending[h] = (carry, pv)

        z = {u: scores(u) for u in range(min(2, len(units)))}
        staged, pending = {}, {}
        for u in range(len(units)):
            staged[u] = keep_sums(z.pop(u), strict if units[u][1] == 0 else None)
            if u + 2 < len(units):
                z[u + 2] = scores(u + 2)
            if u >= 2:
                finish(u - 2, staged, pending)
        for u in range(max(len(units) - 2, 0), len(units)):
            finish(u, staged, pending)

    def alive():
        return jnp.max(carry_ref[...]) > EXP_UNDERFLOW

    def far(state):
        j = i - 2 - state[0]
        z = [_dot_nt(q_heads[h], block(k_ref, j, h)) for h in range(heads)]
        staged = [keep_sums(z[h], None) for h in range(heads)]
        for h in range(heads):
            log_beta, total, sums = staged[h]
            acc_ref[h] += _dot(weights(log_beta, sums, carry_ref[h], None), block(v_ref, j, h))
            carry_ref[h] += total
        return state[0] + 1, alive()

    @pl.when(i == 0)
    def _():
        near(False)

    @pl.when(i > 0)
    def _():
        near(True)
        lax.while_loop(lambda state: jnp.logical_and(state[0] < i - 1, state[1]), far, (jnp.int32(0), alive()))

    for p in range(pairs):
        o_ref[:, p * LANES:(p + 1) * LANES] = jnp.where(low, acc_ref[2 * p], acc_ref[2 * p + 1]).astype(o_ref.dtype)


def _sb_attn(qkv, *, t, pairs):
    b, s, _ = qkv.shape
    groups = SB_HEADS * SB_DH // (LANES * pairs)
    width = LANES * pairs
    return pl.pallas_call(
        functools.partial(_sb_attn_kernel, t=t, pairs=pairs),
        grid=(b, groups, s // t),
        in_specs=[
            pl.BlockSpec((None, t, width), lambda bi, g, i: (bi, i, g)),
            pl.BlockSpec((None, s, width), lambda bi, g, i: (bi, 0, groups + g)),
            pl.BlockSpec((None, s, width), lambda bi, g, i: (bi, 0, 2 * groups + g)),
        ],
        out_specs=pl.BlockSpec((None, t, width), lambda bi, g, i: (bi, i, g)),
        out_shape=jax.ShapeDtypeStruct((b, s, SB_HEADS * SB_DH), BF16),
        scratch_shapes=[pltpu.VMEM((2 * pairs, t, 1), F32), pltpu.VMEM((2 * pairs, t, LANES), F32)],
        compiler_params=_params(3),
        name="sb_attn",
    )(qkv, qkv, qkv)


def _cross_kernel(h_ref, mix_a_ref, mix_b_ref, wmix_ref, g_ref, wq_ref, kv_ref, wo_ref, out_ref, o_scr):
    d = h_ref.shape[1]
    dh = d // XA_HEADS
    half = mix_a_ref.shape[1]
    h1 = h_ref[...] + (_dot(mix_a_ref[...], wmix_ref[:half, :]) + _dot(mix_b_ref[...], wmix_ref[half:, :]))
    q = _dot(_rms(h1, g_ref[...]).astype(BF16), wq_ref[...]).astype(BF16)
    for hd in range(XA_HEADS):
        s = _dot_nt(q[:, hd * dh:(hd + 1) * dh], kv_ref[:, hd * dh:(hd + 1) * dh])
        p = jnp.exp(s - jnp.max(s, axis=-1, keepdims=True))
        o = _dot(p.astype(BF16), kv_ref[:, d + hd * dh:d + (hd + 1) * dh]) / jnp.sum(p, axis=-1, keepdims=True)
        o_scr[:, hd * dh:(hd + 1) * dh] = o.astype(BF16)
    out_ref[...] = h1 + _dot(o_scr[...], wo_ref[...])


def _cross(h, mix_a, mix_b, w_mix, g, wq, mem_kv, wo, *, seq, tm):
    n, d = h.shape
    blocks = seq // tm
    mem_len = mem_kv.shape[1]
    dh = d // XA_HEADS
    half = w_mix.shape[0] // 2
    (mix_a, col_a), (mix_b, col_b) = mix_a, mix_b
    return pl.pallas_call(
        _cross_kernel,
        grid=(n // tm,),
        in_specs=[
            pl.BlockSpec((tm, d), lambda i: (i, 0)),
            pl.BlockSpec((tm, half), lambda i: (i, col_a)),
            pl.BlockSpec((tm, half), lambda i: (i, col_b)),
            _resident(w_mix.shape),
            _resident((1, d)),
            _resident(wq.shape),
            pl.BlockSpec((None, mem_len, 2 * d), lambda i: (i // blocks, 0, 0)),
            _resident(wo.shape),
        ],
        out_specs=pl.BlockSpec((tm, d), lambda i: (i, 0)),
        out_shape=jax.ShapeDtypeStruct((n, d), F32),
        scratch_shapes=[pltpu.VMEM((tm, d), BF16)],
        compiler_params=_params(1),
        name="cross_attn",
    )(h, mix_a, mix_b, w_mix.astype(BF16), g.reshape(1, -1), (wq * dh ** -0.5).astype(BF16), mem_kv, wo.astype(BF16))


def _mlp_kernel(h_ref, g_ref, w1_ref, w2_ref, gf_ref, out_ref, *, tf, final_norm):
    x = h_ref[...]
    xn = _rms(x, g_ref[...]).astype(BF16)
    acc = x
    for c in range(w1_ref.shape[1] // tf):
        a = jnp.maximum(_dot(xn, w1_ref[:, c * tf:(c + 1) * tf]), 0.0)
        acc = acc + _dot((a * a).astype(BF16), w2_ref[c * tf:(c + 1) * tf, :])
    if final_norm:
        acc = _rms(acc, gf_ref[...])
    out_ref[...] = acc


def _mlp(h, g, w1, w2, g_final, *, tm, tf, final_norm):
    n, d = h.shape
    return pl.pallas_call(
        functools.partial(_mlp_kernel, tf=tf, final_norm=final_norm),
        grid=(n // tm,),
        in_specs=[
            pl.BlockSpec((tm, d), lambda i: (i, 0)),
            _resident((1, d)),
            _resident(w1.shape),
            _resident(w2.shape),
            _resident((1, d)),
        ],
        out_specs=pl.BlockSpec((tm, d), lambda i: (i, 0)),
        out_shape=jax.ShapeDtypeStruct((n, d), F32),
        compiler_params=_params(1),
        name="mlp",
    )(h, g.reshape(1, -1), w1.astype(BF16), w2.astype(BF16), g_final.reshape(1, -1))


def _rope_tables(seq):
    half = MLA_ROPE // 2
    inv = (ROPE_BASE ** (-np.arange(half, dtype=np.float32) / half)).astype(np.float32)
    ang = np.arange(seq, dtype=np.float32)[:, None] * inv[None, :]
    pad = lambda a: np.concatenate([a, a, np.zeros((seq, LANES - 2 * half), np.float32)], axis=1).astype(np.float32)
    return jnp.asarray(pad(np.cos(ang))), jnp.asarray(pad(np.sin(ang)))


def kernel(x, mem, ev_norm, ev_w_in, diff_lq1, diff_lk1, diff_lq2, diff_lk2, diff_subln, mla_g_cq, mla_w_uq, mla_g_ckv, mla_w_ukv, ev_w_out, od_norm, sb_w_in, sb_w_out, xa_norm, xa_mem_norm, xa_wq, xa_wkv, xa_wo, mlp_norm, mlp_w1, mlp_w2, final_norm):
    b, seq, d = x.shape
    mem_len = mem.shape[1]
    depth = xa_norm.shape[0]
    tm = ROW_TILE
    h = x.reshape(b * seq, d)
    mem2 = mem.reshape(b * mem_len, d)
    cos_pad, sin_pad = _rope_tables(seq)
    for i in range(depth):
        j = i // 2
        if i % 2 == 0:
            lambda_init = 0.8 - 0.6 * math.exp(-0.3 * i)
            dq, dk, dv, mq, mk, mv = _even_proj(
                h, ev_norm[j], ev_w_in[j], mla_g_cq[j], mla_w_uq[j], mla_g_ckv[j], mla_w_ukv[j],
                cos_pad, sin_pad, seq=seq, tm=tm)
            split = lambda a: a.reshape(b, seq, a.shape[-1])
            o_a = _diff_attn(split(dq), split(dk), dv, diff_lq1[j], diff_lk1[j], diff_lq2[j], diff_lk2[j],
                             diff_subln[j], lambda_init, t=tm, heads=DIFF_HEADS_PER_STEP)
            o_b = _mla_attn(split(mq), split(mk), mv, t=tm, heads=MLA_HEADS_PER_STEP)
            mix_a, mix_b = (o_a.reshape(b * seq, -1), 0), (o_b.reshape(b * seq, -1), 0)
            w_mix = ev_w_out[j]
        else:
            qkv = _norm_matmul(h, od_norm[j].reshape(1, -1), sb_w_in[j].astype(BF16), tm=WIDE_ROW_TILE, name="sb_proj",
                               scaled_cols=SB_HEADS * SB_DH, scale=SB_DH ** -0.5)
            mix = _sb_attn(qkv.reshape(b, seq, -1), t=SB_BLOCK, pairs=SB_PAIRS_PER_STEP).reshape(b * seq, -1)
            mix_a, mix_b = (mix, 0), (mix, 1)
            w_mix = sb_w_out[j]
        mem_kv = _norm_matmul(mem2, xa_mem_norm[i].reshape(1, -1), xa_wkv[i].astype(BF16), tm=mem_len, name="mem_kv")
        h = _cross(h, mix_a, mix_b, w_mix, xa_norm[i], xa_wq[i], mem_kv.reshape(b, mem_len, -1), xa_wo[i], seq=seq, tm=WIDE_ROW_TILE)
        h = _mlp(h, mlp_norm[i], mlp_w1[i], mlp_w2[i], final_norm, tm=WIDE_ROW_TILE, tf=MLP_FF_TILE, final_norm=(i == depth - 1))
    return h.reshape(b, seq, d)
```

```python
import functools
import math

import jax
import jax.numpy as jnp
import numpy as np
from jax import lax
from jax.experimental import pallas as pl
from jax.experimental.pallas import tpu as pltpu

F32 = jnp.float32
BF16 = jnp.bfloat16

EPS = 1e-6
CHUNK_SHIFT = 6
DIFF_HEADS = 4
DIFF_DK = 64
MLA_HEADS = 4
MLA_NOPE = 128
MLA_ROPE = 64
MLA_SCALE = (MLA_NOPE + MLA_ROPE) ** -0.5
ROPE_BASE = 10000.0
SB_HEADS = 16
SB_DH = 64
XA_HEADS = 4
NEG_BIG = -1e30
LOG2E = math.log2(math.e)
EXP_UNDERFLOW = -104.0
LANES = 128
DV = 128
ONES_ROWS = 16

VMEM_LIMIT_BYTES = 48 * 1024 * 1024

ROW_TILE = 512
WIDE_ROW_TILE = 1024
MLP_FF_TILE = 1024
DIFF_HEADS_PER_STEP = 2
MLA_HEADS_PER_STEP = 2
SB_BLOCK = 256
SB_PAIRS_PER_STEP = 4


def _params(n_grid):
    return pltpu.CompilerParams(
        dimension_semantics=("arbitrary",) * n_grid, vmem_limit_bytes=VMEM_LIMIT_BYTES
    )


def _resident(shape):
    zeros = (0,) * len(shape)
    return pl.BlockSpec(shape, lambda *_: zeros, pipeline_mode=pl.Buffered(1))


def _rms(x, g):
    ms = jnp.mean(x * x, axis=-1, keepdims=True)
    return x * lax.rsqrt(ms + EPS) * g


def _dot(a, b):
    return jnp.dot(a, b, preferred_element_type=F32)


def _dot_nt(a, b):
    return lax.dot_general(a, b, (((1,), (1,)), ((), ())), preferred_element_type=F32)


def _norm_matmul_kernel(x_ref, g_ref, w_ref, o_ref, *, scaled_cols, scale):
    xn = _rms(x_ref[...], g_ref[...]).astype(BF16)
    if scaled_cols:
        o_ref[:, :scaled_cols] = (_dot(xn, w_ref[:, :scaled_cols]) * scale).astype(o_ref.dtype)
    o_ref[:, scaled_cols:] = _dot(xn, w_ref[:, scaled_cols:]).astype(o_ref.dtype)


def _norm_matmul(x, g, w, *, tm, name, scaled_cols=0, scale=1.0):
    n, d = x.shape
    f = w.shape[1]
    return pl.pallas_call(
        functools.partial(_norm_matmul_kernel, scaled_cols=scaled_cols, scale=scale),
        grid=(n // tm,),
        in_specs=[
            pl.BlockSpec((tm, d), lambda i: (i, 0)),
            _resident((1, d)),
            _resident((d, f)),
        ],
        out_specs=pl.BlockSpec((tm, f), lambda i: (i, 0)),
        out_shape=jax.ShapeDtypeStruct((n, f), BF16),
        compiler_params=_params(1),
        name=name,
    )(x, g, w)


def _even_proj_kernel(h_ref, g_ref, w_ref, wkr_ref, gcq_ref, wq_ref, gckv_ref, wkv_ref, cos_ref, sin_ref,
                      dq_ref, dk_ref, dv_ref, mq_ref, mk_ref, mv_ref):
    tm = h_ref.shape[0]
    xn = _rms(h_ref[...], g_ref[...]).astype(BF16)
    proj = _dot(xn, w_ref[:, :1920])
    key_rope = _dot(xn, wkr_ref[...])
    lane = lax.broadcasted_iota(jnp.int32, (tm, LANES), 1)
    low = lane < DIFF_DK
    frame_in_block = lax.broadcasted_iota(jnp.int32, (tm, LANES), 0).astype(F32)
    ones = jnp.where(lane < DIFF_DK + 3, 1.0, 0.0)
    for h in range(DIFF_HEADS):
        b = frame_in_block * _alibi_slope(h)
        b_hi = b.astype(BF16).astype(F32)
        b_mid = (b - b_hi).astype(BF16).astype(F32)
        bias = jnp.where(lane == DIFF_DK, b_hi, jnp.where(lane == DIFF_DK + 1, b_mid, b - b_hi - b_mid))
        bias = jnp.where(lane < DIFF_DK + 3, bias, 0.0)
        q_pair = proj[:, h * 128:(h + 1) * 128] * DIFF_DK ** -0.5
        k_pair = proj[:, 512 + h * 128:512 + (h + 1) * 128]
        for half, (q, k) in enumerate(((q_pair, k_pair), (pltpu.roll(q_pair, DIFF_DK, 1), pltpu.roll(k_pair, DIFF_DK, 1)))):
            cols = slice(h * 256 + half * 128, h * 256 + (half + 1) * 128)
            dq_ref[:, cols] = jnp.where(low, q, ones).astype(BF16)
            dk_ref[:, cols] = jnp.where(low, k, bias).astype(BF16)
    dv_ref[...] = proj[:, 1024:1536].T.astype(BF16)

    cos = cos_ref[...]
    sin = sin_ref[...]
    k_rope = (key_rope[:, :128] * cos + key_rope[:, 128:] * sin).astype(BF16)

    cqn = _rms(proj[:, 1536:1792], gcq_ref[...]).astype(BF16)
    qm = _dot(cqn, wq_ref[...])
    ckvn = _rms(proj[:, 1792:1920], gckv_ref[...]).astype(BF16)
    kv = _dot(ckvn, wkv_ref[...])
    for h in range(MLA_HEADS):
        mq_ref[:, h * 256:h * 256 + 128] = qm[:, h * 128:(h + 1) * 128].astype(BF16)
        q_rope = qm[:, 512 + h * 128:512 + (h + 1) * 128] * cos + qm[:, 1024 + h * 128:1024 + (h + 1) * 128] * sin
        mq_ref[:, h * 256 + 128:(h + 1) * 256] = q_rope.astype(BF16)
        mk_ref[:, h * 256:h * 256 + 128] = kv[:, h * 256:h * 256 + 128].astype(BF16)
        mk_ref[:, h * 256 + 128:(h + 1) * 256] = k_rope
        mv_ref[h * 128:(h + 1) * 128, :] = kv[:, h * 256 + 128:(h + 1) * 256].T.astype(BF16)


def _alibi_slope(h):
    return 2.0 ** (-8.0 * (h + 1) / DIFF_HEADS)


def _rotated(w):
    half = w.shape[1] // 2
    return jnp.concatenate([-w[:, half:], w[:, :half]], axis=1)


def _pad_cols(w, width):
    return jnp.pad(w, ((0, 0), (0, width - w.shape[1])))


def _even_proj(h, g, w_in, g_cq, w_uq, g_ckv, w_ukv, cos_pad, sin_pad, *, seq, tm):
    n, d = h.shape
    kr = w_in[:, 1920:1984]
    w_main = w_in.astype(BF16)
    w_kr = jnp.concatenate([_pad_cols(kr, 128), _pad_cols(_rotated(kr), 128)], axis=1).astype(BF16)
    per_head = MLA_NOPE + MLA_ROPE
    nope = [w_uq[:, hh * per_head:hh * per_head + MLA_NOPE] for hh in range(MLA_HEADS)]
    rope = [w_uq[:, hh * per_head + MLA_NOPE:(hh + 1) * per_head] for hh in range(MLA_HEADS)]
    wq_ext = jnp.concatenate(
        nope + [_pad_cols(r, 128) for r in rope] + [_pad_cols(_rotated(r), 128) for r in rope], axis=1).astype(BF16)
    pos_blocks = seq // tm
    bf = lambda width: jax.ShapeDtypeStruct((n, width), BF16)
    row = lambda width: pl.BlockSpec((tm, width), lambda i: (i, 0))
    vt_shape = jax.ShapeDtypeStruct((n // seq, pos_blocks, 512, tm), BF16)
    vt_spec = pl.BlockSpec((None, None, 512, tm), lambda i: (i // pos_blocks, i % pos_blocks, 0, 0))
    return pl.pallas_call(
        _even_proj_kernel,
        grid=(n // tm,),
        in_specs=[
            row(d),
            _resident((1, d)),
            _resident(w_main.shape),
            _resident(w_kr.shape),
            _resident((1, g_cq.shape[-1])),
            _resident(wq_ext.shape),
            _resident((1, g_ckv.shape[-1])),
            _resident(w_ukv.shape),
            pl.BlockSpec((tm, 128), lambda i: (i % pos_blocks, 0)),
            pl.BlockSpec((tm, 128), lambda i: (i % pos_blocks, 0)),
        ],
        out_specs=[row(1024), row(1024), vt_spec, row(1024), row(1024), vt_spec],
        out_shape=[bf(1024), bf(1024), vt_shape, bf(1024), bf(1024), vt_shape],
        compiler_params=_params(1),
        name="even_proj",
    )(h, g.reshape(1, -1), w_main, w_kr, g_cq.reshape(1, -1), wq_ext, g_ckv.reshape(1, -1), w_ukv.astype(BF16),
      cos_pad, sin_pad)


class _Softmax:
    def __init__(self, scratch, values, block_shift, diagonal_bias):
        s0, s1, self.p, self.a, self.m, self.acc = scratch
        self.s = (s0, s1)
        self.values, self.block_shift, self.diagonal_bias = values, block_shift, diagonal_bias

    def reset(self):
        self.m[...] = jnp.full(self.m.shape, NEG_BIG, F32)
        self.acc[...] = jnp.zeros(self.acc.shape, F32)
        self.p[...] = jnp.zeros(self.p.shape, BF16)
        self.a[...] = jnp.ones(self.a.shape, F32)

    def weigh(self, cur, scale, bias, shift):
        m_prev = self.m[...]
        if bias is None:
            m_block = jnp.max(self.s[cur][...], axis=0, keepdims=True) * scale
            m_new = jnp.maximum(m_prev, m_block if shift is None else m_block + shift)
            p = jnp.exp2(self.s[cur][...] * scale - (m_new if shift is None else m_new - shift))
        else:
            logits = self.s[cur][...] * scale + bias
            m_block = jnp.max(logits, axis=0, keepdims=True)
            m_new = jnp.maximum(m_prev, m_block if shift is None else m_block + shift)
            p = jnp.exp2(logits - (m_new if shift is None else m_new - shift))
        self.m[...] = m_new
        self.p[...] = p.astype(BF16)
        self.a[...] = jnp.exp2(m_prev - m_new)

    def start_values(self, j):
        vt = self.values(j)
        vt_ones = jnp.concatenate([vt, jnp.ones((ONES_ROWS, vt.shape[1]), BF16)], axis=0)
        return self.a[...], _dot(vt_ones, self.p[...])

    def fold(self, alpha, pv):
        self.acc[...] = alpha * self.acc[...] + pv

    def normalized(self):
        return self.acc[:DV, :] / self.acc[DV:DV + 1, :]


def _run_key_blocks(i, sms, scores, q_ref, q_next_ref, scale):
    @pl.when(i == 0)
    def _():
        for sm, s in zip(sms, scores(q_ref, 0)):
            sm.s[0][...] = s

    for sm in sms:
        sm.reset()

    def iteration(j, cur):
        pending = [sm.start_values(jnp.maximum(j - 1, 0)) for sm in sms]
        for sm, s in zip(sms, scores(q_ref, j + 1)):
            sm.s[1 - cur][...] = s
        for sm in sms:
            sm.weigh(cur, scale, None, sm.block_shift(j))
        for sm, pend in zip(sms, pending):
            sm.fold(*pend)

    def pair(jj, carry):
        iteration(2 * jj, 0)
        iteration(2 * jj + 1, 1)
        return carry

    lax.fori_loop(0, i >> 1, pair, 0)

    @pl.when((i & 1) == 1)
    def _():
        iteration(i - 1, 0)
        for sm in sms:
            sm.s[0][...] = sm.s[1][...]

    pending = [sm.start_values(jnp.maximum(i - 1, 0)) for sm in sms]
    for sm in sms:
        sm.weigh(0, scale, sm.diagonal_bias(), sm.block_shift(i))
    last = [sm.start_values(i) for sm in sms]
    for sm, s in zip(sms, scores(q_next_ref, 0)):
        sm.s[0][...] = s
    for sm, pend in zip(sms, pending):
        sm.fold(*pend)
    for sm, pend in zip(sms, last):
        sm.fold(*pend)


def _next_query_spec(blocks, t, width):
    return pl.BlockSpec((None, t, width), lambda bi, g, i: (bi, jnp.minimum(i + 1, blocks - 1), g))


def _softmax_scratch(t):
    return [pltpu.VMEM((t, t), F32), pltpu.VMEM((t, t), F32), pltpu.VMEM((t, t), BF16), pltpu.VMEM((1, t), F32),
            pltpu.VMEM((1, t), F32), pltpu.VMEM((DV + ONES_ROWS, t), F32)]


def _key_query_iotas(t):
    return lax.broadcasted_iota(jnp.int32, (t, t), 0), lax.broadcasted_iota(jnp.int32, (t, t), 1)


def _diff_attn_kernel(q_ref, q_next_ref, k_ref, vt_ref, slope_ref, lq1_ref, lk1_ref, lq2_ref, lk2_ref, gsub_ref,
                      o_ref, bias_diag, *scratch, t, heads, lambda_init):
    i = pl.program_id(2)

    @pl.when(i == 0)
    def _():
        key, qry = _key_query_iotas(t)
        ahead = jnp.maximum(key - qry, 0).astype(F32)
        allowed = (key >> CHUNK_SHIFT) <= (qry >> CHUNK_SHIFT)
        for hh in range(heads):
            bias_diag[hh] = jnp.where(allowed, (-2.0 * LOG2E) * slope_ref[hh][:, :1] * ahead, NEG_BIG)

    sms = []
    for hh in range(heads):
        for half in range(2):
            n = 2 * hh + half
            sms.append(_Softmax(
                scratch[6 * n:6 * n + 6],
                values=lambda j, hh=hh: vt_ref[j, hh * DV:(hh + 1) * DV, :],
                block_shift=lambda j, hh=hh: (LOG2E * t) * slope_ref[hh][:, :1] * j.astype(F32),
                diagonal_bias=lambda hh=hh: bias_diag[hh]))

    def scores(q, j):
        rows = pl.ds(pl.multiple_of(j * t, t), t)
        for n in range(2 * heads):
            cols = slice(n * 128, (n + 1) * 128)
            yield _dot_nt(k_ref[rows, cols], q[:, cols])

    _run_key_blocks(i, sms, scores, q_ref, q_next_ref, LOG2E)

    lam = (jnp.exp(jnp.sum(lq1_ref[...] * lk1_ref[...], axis=-1, keepdims=True))
           - jnp.exp(jnp.sum(lq2_ref[...] * lk2_ref[...], axis=-1, keepdims=True)) + lambda_init)
    for hh in range(heads):
        o = (sms[2 * hh].normalized() - lam * sms[2 * hh + 1].normalized()).T
        o_ref[:, hh * DV:(hh + 1) * DV] = (_rms(o, gsub_ref[...]) * (1.0 - lambda_init)).astype(o_ref.dtype)


def _diff_attn(dq, dk, dv, lq1, lk1, lq2, lk2, g_sub, lambda_init, *, t, heads):
    b, s, _ = dq.shape
    slopes = jnp.asarray([_alibi_slope(h) for h in range(DIFF_HEADS)], F32)
    slopes = jnp.broadcast_to(slopes[:, None, None], (DIFF_HEADS, 1, LANES))
    vec = lambda a: a.reshape(1, -1)
    small = lambda width: pl.BlockSpec((1, width), lambda bi, g, i: (0, 0))
    kern = functools.partial(_diff_attn_kernel, t=t, heads=heads, lambda_init=lambda_init)
    return pl.pallas_call(
        kern,
        grid=(b, DIFF_HEADS // heads, s // t),
        in_specs=[
            pl.BlockSpec((None, t, 256 * heads), lambda bi, g, i: (bi, i, g)),
            _next_query_spec(s // t, t, 256 * heads),
            pl.BlockSpec((None, s, 256 * heads), lambda bi, g, i: (bi, 0, g)),
            pl.BlockSpec((None, s // t, DV * heads, t), lambda bi, g, i: (bi, 0, g, 0)),
            pl.BlockSpec((heads, 1, LANES), lambda bi, g, i: (g, 0, 0)),
            small(DIFF_DK), small(DIFF_DK), small(DIFF_DK), small(DIFF_DK),
            small(2 * DIFF_DK),
        ],
        out_specs=pl.BlockSpec((None, t, DV * heads), lambda bi, g, i: (bi, i, g)),
        out_shape=jax.ShapeDtypeStruct((b, s, DIFF_HEADS * DV), BF16),
        scratch_shapes=[pltpu.VMEM((heads, t, t), F32)] + _softmax_scratch(t) * (2 * heads),
        compiler_params=_params(3),
        name="diff_attn",
    )(dq, dq, dk, dv, slopes, vec(lq1), vec(lk1), vec(lq2), vec(lk2), vec(g_sub))


def _mla_attn_kernel(q_ref, q_next_ref, k_ref, vt_ref, o_ref, mask_diag, *scratch, t, heads):
    i = pl.program_id(2)

    @pl.when(i == 0)
    def _():
        key, qry = _key_query_iotas(t)
        mask_diag[...] = jnp.where((key >> CHUNK_SHIFT) <= (qry >> CHUNK_SHIFT), 0.0, NEG_BIG)

    sms = [_Softmax(scratch[6 * hh:6 * hh + 6],
                    values=lambda j, hh=hh: vt_ref[j, hh * DV:(hh + 1) * DV, :],
                    block_shift=lambda j: None,
                    diagonal_bias=lambda: mask_diag[...]) for hh in range(heads)]

    def scores(q, j):
        rows = pl.ds(pl.multiple_of(j * t, t), t)
        for hh in range(heads):
            cols = slice(hh * 256, (hh + 1) * 256)
            yield _dot_nt(k_ref[rows, cols], q[:, cols])

    _run_key_blocks(i, sms, scores, q_ref, q_next_ref, MLA_SCALE * LOG2E)
    for hh in range(heads):
        o_ref[:, hh * DV:(hh + 1) * DV] = sms[hh].normalized().T.astype(o_ref.dtype)


def _mla_attn(mq, mk, mv, *, t, heads):
    b, s, _ = mq.shape
    return pl.pallas_call(
        functools.partial(_mla_attn_kernel, t=t, heads=heads),
        grid=(b, MLA_HEADS // heads, s // t),
        in_specs=[
            pl.BlockSpec((None, t, 256 * heads), lambda bi, g, i: (bi, i, g)),
            _next_query_spec(s // t, t, 256 * heads),
            pl.BlockSpec((None, s, 256 * heads), lambda bi, g, i: (bi, 0, g)),
            pl.BlockSpec((None, s // t, DV * heads, t), lambda bi, g, i: (bi, 0, g, 0)),
        ],
        out_specs=pl.BlockSpec((None, t, DV * heads), lambda bi, g, i: (bi, i, g)),
        out_shape=jax.ShapeDtypeStruct((b, s, MLA_HEADS * DV), BF16),
        scratch_shapes=[pltpu.VMEM((t, t), F32)] + _softmax_scratch(t) * heads,
        compiler_params=_params(3),
        name="mla_attn",
    )(mq, mq, mk, mv)


def _sb_attn_kernel(q_ref, k_ref, v_ref, o_ref, carry_ref, acc_ref, *, t, pairs):
    i = pl.program_id(2)
    heads = 2 * pairs
    low = lax.broadcasted_iota(jnp.int32, (t, LANES), 1) < SB_DH
    q_heads = []
    for p in range(pairs):
        q = q_ref[:, p * LANES:(p + 1) * LANES]
        zero = jnp.zeros_like(q)
        q_heads += [jnp.where(low, q, zero), jnp.where(low, zero, q)]
    row, col = _key_query_iotas(t)
    later = (row > col).astype(BF16)
    strict = col < row

    def block(ref, j, h):
        return ref[pl.ds(pl.multiple_of(j * t, t), t), (h // 2) * LANES:(h // 2 + 1) * LANES]

    def keep_sums(z, mask):
        neg_z = -z
        log_keep = jnp.minimum(neg_z, 0.0) - jnp.log(1.0 + jnp.exp(jnp.minimum(z, neg_z)))
        if mask is not None:
            log_keep = jnp.where(mask, log_keep, 0.0)
        sums = _dot(log_keep.astype(BF16), later)
        return z + log_keep, jnp.sum(log_keep, axis=-1, keepdims=True), sums

    def weights(log_beta, sums, carry, mask):
        between = sums if carry is None else sums + carry
        a = jnp.exp(log_beta + between)
        if mask is not None:
            a = jnp.where(mask, a, 0.0)
        return a.astype(BF16)

    def near(with_previous):
        back = (0, 1) if with_previous else (0,)
        units = [(h, d) for h in range(heads) for d in back]

        def scores(u):
            h, d = units[u]
            return _dot_nt(q_heads[h], block(k_ref, i - d, h))

        def finish(u, staged, pending):
            h, d = units[u]
            log_beta, total, sums = staged[u]
            if d == 0:
                pv = _dot(weights(log_beta, sums, None, strict), block(v_ref, i, h))
                carry = total
            else:
                carry_diag, pv_diag = pending.pop(h)
                pv = pv_diag + _dot(weights(log_beta, sums, carry_diag, None), block(v_ref, i - 1, h))
                carry = carry_diag + total
            if d == back[-1]:
                acc_ref[h] = pv
                carry_ref[h] = carry
            else:
                pending[h] = (carry, pv)

        z = {u: scores(u) for u in range(min(2, len(units)))}
        staged, pending = {}, {}
        for u in range(len(units)):
            staged[u] = keep_sums(z.pop(u), strict if units[u][1] == 0 else None)
            if u + 2 < len(units):
                z[u + 2] = scores(u + 2)
            if u >= 2:
                finish(u - 2, staged, pending)
        for u in range(max(len(units) - 2, 0), len(units)):
            finish(u, staged, pending)

    def alive():
        return jnp.max(carry_ref[...]) > EXP_UNDERFLOW

    def far(state):
        j = i - 2 - state[0]
        z = [_dot_nt(q_heads[h], block(k_ref, j, h)) for h in range(heads)]
        staged = [keep_sums(z[h], None) for h in range(heads)]
        for h in range(heads):
            log_beta, total, sums = staged[h]
            acc_ref[h] += _dot(weights(log_beta, sums, carry_ref[h], None), block(v_ref, j, h))
            carry_ref[h] += total
        return state[0] + 1, alive()

    @pl.when(i == 0)
    def _():
        near(False)

    @pl.when(i > 0)
    def _():
        near(True)
        lax.while_loop(lambda state: jnp.logical_and(state[0] < i - 1, state[1]), far, (jnp.int32(0), alive()))

    for p in range(pairs):
        o_ref[:, p * LANES:(p + 1) * LANES] = jnp.where(low, acc_ref[2 * p], acc_ref[2 * p + 1]).astype(o_ref.dtype)


def _sb_attn(qkv, *, t, pairs):
    b, s, _ = qkv.shape
    groups = SB_HEADS * SB_DH // (LANES * pairs)
    width = LANES * pairs
    return pl.pallas_call(
        functools.partial(_sb_attn_kernel, t=t, pairs=pairs),
        grid=(b, groups, s // t),
        in_specs=[
            pl.BlockSpec((None, t, width), lambda bi, g, i: (bi, i, g)),
            pl.BlockSpec((None, s, width), lambda bi, g, i: (bi, 0, groups + g)),
            pl.BlockSpec((None, s, width), lambda bi, g, i: (bi, 0, 2 * groups + g)),
        ],
        out_specs=pl.BlockSpec((None, t, width), lambda bi, g, i: (bi, i, g)),
        out_shape=jax.ShapeDtypeStruct((b, s, SB_HEADS * SB_DH), BF16),
        scratch_shapes=[pltpu.VMEM((2 * pairs, t, 1), F32), pltpu.VMEM((2 * pairs, t, LANES), F32)],
        compiler_params=_params(3),
        name="sb_attn",
    )(qkv, qkv, qkv)


def _cross_kernel(h_ref, mix_a_ref, mix_b_ref, wmix_ref, g_ref, wq_ref, kv_ref, wo_ref, out_ref, o_scr):
    d = h_ref.shape[1]
    dh = d // XA_HEADS
    half = mix_a_ref.shape[1]
    h1 = h_ref[...] + (_dot(mix_a_ref[...], wmix_ref[:half, :]) + _dot(mix_b_ref[...], wmix_ref[half:, :]))
    q = _dot(_rms(h1, g_ref[...]).astype(BF16), wq_ref[...]).astype(BF16)
    for hd in range(XA_HEADS):
        s = _dot_nt(q[:, hd * dh:(hd + 1) * dh], kv_ref[:, hd * dh:(hd + 1) * dh])
        p = jnp.exp(s - jnp.max(s, axis=-1, keepdims=True))
        o = _dot(p.astype(BF16), kv_ref[:, d + hd * dh:d + (hd + 1) * dh]) / jnp.sum(p, axis=-1, keepdims=True)
        o_scr[:, hd * dh:(hd + 1) * dh] = o.astype(BF16)
    out_ref[...] = h1 + _dot(o_scr[...], wo_ref[...])


def _cross(h, mix_a, mix_b, w_mix, g, wq, mem_kv, wo, *, seq, tm):
    n, d = h.shape
    blocks = seq // tm
    mem_len = mem_kv.shape[1]
    dh = d // XA_HEADS
    half = w_mix.shape[0] // 2
    (mix_a, col_a), (mix_b, col_b) = mix_a, mix_b
    return pl.pallas_call(
        _cross_kernel,
        grid=(n // tm,),
        in_specs=[
            pl.BlockSpec((tm, d), lambda i: (i, 0)),
            pl.BlockSpec((tm, half), lambda i: (i, col_a)),
            pl.BlockSpec((tm, half), lambda i: (i, col_b)),
            _resident(w_mix.shape),
            _resident((1, d)),
            _resident(wq.shape),
            pl.BlockSpec((None, mem_len, 2 * d), lambda i: (i // blocks, 0, 0)),
            _resident(wo.shape),
        ],
        out_specs=pl.BlockSpec((tm, d), lambda i: (i, 0)),
        out_shape=jax.ShapeDtypeStruct((n, d), F32),
        scratch_shapes=[pltpu.VMEM((tm, d), BF16)],
        compiler_params=_params(1),
        name="cross_attn",
    )(h, mix_a, mix_b, w_mix.astype(BF16), g.reshape(1, -1), (wq * dh ** -0.5).astype(BF16), mem_kv, wo.astype(BF16))


def _mlp_kernel(h_ref, g_ref, w1_ref, w2_ref, gf_ref, out_ref, *, tf, final_norm):
    x = h_ref[...]
    xn = _rms(x, g_ref[...]).astype(BF16)
    acc = x
    for c in range(w1_ref.shape[1] // tf):
        a = jnp.maximum(_dot(xn, w1_ref[:, c * tf:(c + 1) * tf]), 0.0)
        acc = acc + _dot((a * a).astype(BF16), w2_ref[c * tf:(c + 1) * tf, :])
    if final_norm:
        acc = _rms(acc, gf_ref[...])
    out_ref[...] = acc


def _mlp(h, g, w1, w2, g_final, *, tm, tf, final_norm):
    n, d = h.shape
    return pl.pallas_call(
        functools.partial(_mlp_kernel, tf=tf, final_norm=final_norm),
        grid=(n // tm,),
        in_specs=[
            pl.BlockSpec((tm, d), lambda i: (i, 0)),
            _resident((1, d)),
            _resident(w1.shape),
            _resident(w2.shape),
            _resident((1, d)),
        ],
        out_specs=pl.BlockSpec((tm, d), lambda i: (i, 0)),
        out_shape=jax.ShapeDtypeStruct((n, d), F32),
        compiler_params=_params(1),
        name="mlp",
    )(h, g.reshape(1, -1), w1.astype(BF16), w2.astype(BF16), g_final.reshape(1, -1))


def _rope_tables(seq):
    half = MLA_ROPE // 2
    inv = (ROPE_BASE ** (-np.arange(half, dtype=np.float32) / half)).astype(np.float32)
    ang = np.arange(seq, dtype=np.float32)[:, None] * inv[None, :]
    pad = lambda a: np.concatenate([a, a, np.zeros((seq, LANES - 2 * half), np.float32)], axis=1).astype(np.float32)
    return jnp.asarray(pad(np.cos(ang))), jnp.asarray(pad(np.sin(ang)))


def kernel(x, mem, ev_norm, ev_w_in, diff_lq1, diff_lk1, diff_lq2, diff_lk2, diff_subln, mla_g_cq, mla_w_uq, mla_g_ckv, mla_w_ukv, ev_w_out, od_norm, sb_w_in, sb_w_out, xa_norm, xa_mem_norm, xa_wq, xa_wkv, xa_wo, mlp_norm, mlp_w1, mlp_w2, final_norm):
    b, seq, d = x.shape
    mem_len = mem.shape[1]
    depth = xa_norm.shape[0]
    tm = ROW_TILE
    h = x.reshape(b * seq, d)
    mem2 = mem.reshape(b * mem_len, d)
    cos_pad, sin_pad = _rope_tables(seq)
    for i in range(depth):
        j = i // 2
        if i % 2 == 0:
            lambda_init = 0.8 - 0.6 * math.exp(-0.3 * i)
            dq, dk, dv, mq, mk, mv = _even_proj(
                h, ev_norm[j], ev_w_in[j], mla_g_cq[j], mla_w_uq[j], mla_g_ckv[j], mla_w_ukv[j],
                cos_pad, sin_pad, seq=seq, tm=tm)
            split = lambda a: a.reshape(b, seq, a.shape[-1])
            o_a = _diff_attn(split(dq), split(dk), dv, diff_lq1[j], diff_lk1[j], diff_lq2[j], diff_lk2[j],
                             diff_subln[j], lambda_init, t=tm, heads=DIFF_HEADS_PER_STEP)
            o_b = _mla_attn(split(mq), split(mk), mv, t=tm, heads=MLA_HEADS_PER_STEP)
            mix_a, mix_b = (o_a.reshape(b * seq, -1), 0), (o_b.reshape(b * seq, -1), 0)
            w_mix = ev_w_out[j]
        else:
            qkv = _norm_matmul(h, od_norm[j].reshape(1, -1), sb_w_in[j].astype(BF16), tm=WIDE_ROW_TILE, name="sb_proj",
                               scaled_cols=SB_HEADS * SB_DH, scale=SB_DH ** -0.5)
            mix = _sb_attn(qkv.reshape(b, seq, -1), t=SB_BLOCK, pairs=SB_PAIRS_PER_STEP).reshape(b * seq, -1)
            mix_a, mix_b = (mix, 0), (mix, 1)
            w_mix = sb_w_out[j]
        mem_kv = _norm_matmul(mem2, xa_mem_norm[i].reshape(1, -1), xa_wkv[i].astype(BF16), tm=mem_len, name="mem_kv")
        h = _cross(h, mix_a, mix_b, w_mix, xa_norm[i], xa_wq[i], mem_kv.reshape(b, mem_len, -1), xa_wo[i], seq=seq, tm=WIDE_ROW_TILE)
        h = _mlp(h, mlp_norm[i], mlp_w1[i], mlp_w2[i], final_norm, tm=WIDE_ROW_TILE, tf=MLP_FF_TILE, final_norm=(i == depth - 1))
    return h.reshape(b, seq, d)
```

```python
import functools
import math

import jax
import jax.numpy as jnp
import numpy as np
from jax import lax
from jax.experimental import pallas as pl
from jax.experimental.pallas import tpu as pltpu

F32 = jnp.float32
BF16 = jnp.bfloat16

EPS = 1e-6
CHUNK_SHIFT = 6
DIFF_HEADS = 4
DIFF_DK = 64
MLA_HEADS = 4
MLA_NOPE = 128
MLA_ROPE = 64
MLA_SCALE = (MLA_NOPE + MLA_ROPE) ** -0.5
ROPE_BASE = 10000.0
SB_HEADS = 16
SB_DH = 64
XA_HEADS = 4
NEG_BIG = -1e30
LOG2E = math.log2(math.e)
EXP_UNDERFLOW = -104.0
LANES = 128
DV = 128
ONES_ROWS = 16

VMEM_LIMIT_BYTES = 48 * 1024 * 1024

ROW_TILE = 512
WIDE_ROW_TILE = 1024
MLP_FF_TILE = 1024
DIFF_HEADS_PER_STEP = 2
MLA_HEADS_PER_STEP = 2
SB_BLOCK = 256
SB_PAIRS_PER_STEP = 4


def _params(n_grid):
    return pltpu.CompilerParams(
        dimension_semantics=("arbitrary",) * n_grid, vmem_limit_bytes=VMEM_LIMIT_BYTES
    )


def _resident(shape):
    zeros = (0,) * len(shape)
    return pl.BlockSpec(shape, lambda *_: zeros, pipeline_mode=pl.Buffered(1))


def _rms(x, g):
    ms = jnp.mean(x * x, axis=-1, keepdims=True)
    return x * lax.rsqrt(ms + EPS) * g


def _dot(a, b):
    return jnp.dot(a, b, preferred_element_type=F32)


def _dot_nt(a, b):
    return lax.dot_general(a, b, (((1,), (1,)), ((), ())), preferred_element_type=F32)


def _norm_matmul_kernel(x_ref, g_ref, w_ref, o_ref, *, scaled_cols, scale):
    xn = _rms(x_ref[...], g_ref[...]).astype(BF16)
    if scaled_cols:
        o_ref[:, :scaled_cols] = (_dot(xn, w_ref[:, :scaled_cols]) * scale).astype(o_ref.dtype)
    o_ref[:, scaled_cols:] = _dot(xn, w_ref[:, scaled_cols:]).astype(o_ref.dtype)


def _norm_matmul(x, g, w, *, tm, name, scaled_cols=0, scale=1.0):
    n, d = x.shape
    f = w.shape[1]
    return pl.pallas_call(
        functools.partial(_norm_matmul_kernel, scaled_cols=scaled_cols, scale=scale),
        grid=(n // tm,),
        in_specs=[
            pl.BlockSpec((tm, d), lambda i: (i, 0)),
            _resident((1, d)),
            _resident((d, f)),
        ],
        out_specs=pl.BlockSpec((tm, f), lambda i: (i, 0)),
        out_shape=jax.ShapeDtypeStruct((n, f), BF16),
        compiler_params=_params(1),
        name=name,
    )(x, g, w)


def _even_proj_kernel(h_ref, g_ref, w_ref, wkr_ref, gcq_ref, wq_ref, gckv_ref, wkv_ref, cos_ref, sin_ref,
                      dq_ref, dk_ref, dv_ref, mq_ref, mk_ref, mv_ref):
    tm = h_ref.shape[0]
    xn = _rms(h_ref[...], g_ref[...]).astype(BF16)
    proj = _dot(xn, w_ref[...])
    key_rope = _dot(xn, wkr_ref[...])
    lane = lax.broadcasted_iota(jnp.int32, (tm, LANES), 1)
    low = lane < DIFF_DK
    t = dv_ref.shape[-1]
    frame_in_block = (lax.broadcasted_iota(jnp.int32, (tm, LANES), 0) & (t - 1)).astype(F32)
    ones = jnp.where(lane < DIFF_DK + 3, 1.0, 0.0)
    for h in range(DIFF_HEADS):
        b = frame_in_block * _alibi_slope(h)
        b_hi = b.astype(BF16).astype(F32)
        b_mid = (b - b_hi).astype(BF16).astype(F32)
        bias = jnp.where(lane == DIFF_DK, b_hi, jnp.where(lane == DIFF_DK + 1, b_mid, b - b_hi - b_mid))
        bias = jnp.where(lane < DIFF_DK + 3, bias, 0.0)
        q_pair = proj[:, h * 128:(h + 1) * 128] * DIFF_DK ** -0.5
        k_pair = proj[:, 512 + h * 128:512 + (h + 1) * 128]
        for half, (q, k) in enumerate(((q_pair, k_pair), (pltpu.roll(q_pair, DIFF_DK, 1), pltpu.roll(k_pair, DIFF_DK, 1)))):
            cols = slice(h * 256 + half * 128, h * 256 + (half + 1) * 128)
            dq_ref[:, cols] = jnp.where(low, q, ones).astype(BF16)
            dk_ref[:, cols] = jnp.where(low, k, bias).astype(BF16)
    for r in range(tm // t):
        dv_ref[r] = proj[r * t:(r + 1) * t, 1024:1536].T.astype(BF16)

    cos = cos_ref[...]
    sin = sin_ref[...]
    k_rope = (key_rope[:, :128] * cos + key_rope[:, 128:] * sin).astype(BF16)

    cqn = _rms(proj[:, 1536:1792], gcq_ref[...]).astype(BF16)
    qm = _dot(cqn, wq_ref[...])
    ckvn = _rms(proj[:, 1792:1920], gckv_ref[...]).astype(BF16)
    kv = _dot(ckvn, wkv_ref[...])
    for h in range(MLA_HEADS):
        mq_ref[:, h * 256:h * 256 + 128] = qm[:, h * 128:(h + 1) * 128].astype(BF16)
        q_rope = qm[:, 512 + h * 128:512 + (h + 1) * 128] * cos + qm[:, 1024 + h * 128:1024 + (h + 1) * 128] * sin
        mq_ref[:, h * 256 + 128:(h + 1) * 256] = q_rope.astype(BF16)
        mk_ref[:, h * 256:h * 256 + 128] = kv[:, h * 256:h * 256 + 128].astype(BF16)
        mk_ref[:, h * 256 + 128:(h + 1) * 256] = k_rope
        for r in range(tm // t):
            mv_ref[r, h * 128:(h + 1) * 128, :] = kv[r * t:(r + 1) * t, h * 256 + 128:(h + 1) * 256].T.astype(BF16)


def _alibi_slope(h):
    return 2.0 ** (-8.0 * (h + 1) / DIFF_HEADS)


def _rotated(w):
    half = w.shape[1] // 2
    return jnp.concatenate([-w[:, half:], w[:, :half]], axis=1)


def _pad_cols(w, width):
    return jnp.pad(w, ((0, 0), (0, width - w.shape[1])))


def _even_proj(h, g, w_in, g_cq, w_uq, g_ckv, w_ukv, cos_pad, sin_pad, *, seq, tm, t):
    n, d = h.shape
    kr = w_in[:, 1920:1984]
    w_main = w_in[:, :1920].astype(BF16)
    w_kr = jnp.concatenate([_pad_cols(kr, 128), _pad_cols(_rotated(kr), 128)], axis=1).astype(BF16)
    per_head = MLA_NOPE + MLA_ROPE
    nope = [w_uq[:, hh * per_head:hh * per_head + MLA_NOPE] for hh in range(MLA_HEADS)]
    rope = [w_uq[:, hh * per_head + MLA_NOPE:(hh + 1) * per_head] for hh in range(MLA_HEADS)]
    wq_ext = jnp.concatenate(
        nope + [_pad_cols(r, 128) for r in rope] + [_pad_cols(_rotated(r), 128) for r in rope], axis=1).astype(BF16)
    pos_blocks = seq // tm
    bf = lambda width: jax.ShapeDtypeStruct((n, width), BF16)
    row = lambda width: pl.BlockSpec((tm, width), lambda i: (i, 0))
    vt_shape = jax.ShapeDtypeStruct((n // seq, seq // t, 512, t), BF16)
    vt_spec = pl.BlockSpec((None, tm // t, 512, t), lambda i: (i // pos_blocks, i % pos_blocks, 0, 0))
    return pl.pallas_call(
        _even_proj_kernel,
        grid=(n // tm,),
        in_specs=[
            row(d),
            _resident((1, d)),
            _resident(w_main.shape),
            _resident(w_kr.shape),
            _resident((1, g_cq.shape[-1])),
            _resident(wq_ext.shape),
            _resident((1, g_ckv.shape[-1])),
            _resident(w_ukv.shape),
            pl.BlockSpec((tm, 128), lambda i: (i % pos_blocks, 0)),
            pl.BlockSpec((tm, 128), lambda i: (i % pos_blocks, 0)),
        ],
        out_specs=[row(1024), row(1024), vt_spec, row(1024), row(1024), vt_spec],
        out_shape=[bf(1024), bf(1024), vt_shape, bf(1024), bf(1024), vt_shape],
        compiler_params=_params(1),
        name="even_proj",
    )(h, g.reshape(1, -1), w_main, w_kr, g_cq.reshape(1, -1), wq_ext, g_ckv.reshape(1, -1), w_ukv.astype(BF16),
      cos_pad, sin_pad)


class _Softmax:
    def __init__(self, scratch, values, block_shift, diagonal_bias):
        s0, s1, self.p, self.a, self.m, self.acc = scratch
        self.s = (s0, s1)
        self.values, self.block_shift, self.diagonal_bias = values, block_shift, diagonal_bias

    def reset(self):
        self.m[...] = jnp.full(self.m.shape, NEG_BIG, F32)
        self.acc[...] = jnp.zeros(self.acc.shape, F32)
        self.p[...] = jnp.zeros(self.p.shape, BF16)
        self.a[...] = jnp.ones(self.a.shape, F32)

    def weigh(self, cur, scale, bias, shift):
        m_prev = self.m[...]
        if bias is None:
            m_block = jnp.max(self.s[cur][...], axis=0, keepdims=True) * scale
            m_new = jnp.maximum(m_prev, m_block if shift is None else m_block + shift)
            p = jnp.exp2(self.s[cur][...] * scale - (m_new if shift is None else m_new - shift))
        else:
            logits = self.s[cur][...] * scale + bias
            m_block = jnp.max(logits, axis=0, keepdims=True)
            m_new = jnp.maximum(m_prev, m_block if shift is None else m_block + shift)
            p = jnp.exp2(logits - (m_new if shift is None else m_new - shift))
        self.m[...] = m_new
        self.p[...] = p.astype(BF16)
        self.a[...] = jnp.exp2(m_prev - m_new)

    def start_values(self, j):
        vt = self.values(j)
        vt_ones = jnp.concatenate([vt, jnp.ones((ONES_ROWS, vt.shape[1]), BF16)], axis=0)
        return self.a[...], _dot(vt_ones, self.p[...])

    def fold(self, alpha, pv):
        self.acc[...] = alpha * self.acc[...] + pv

    def normalized(self):
        return self.acc[:DV, :] / self.acc[DV:DV + 1, :]


def _run_key_blocks(i, sms, scores, q_ref, q_next_ref, scale):
    @pl.when(i == 0)
    def _():
        for sm, s in zip(sms, scores(q_ref, 0)):
            sm.s[0][...] = s

    for sm in sms:
        sm.reset()

    def iteration(j, cur):
        pending = [sm.start_values(jnp.maximum(j - 1, 0)) for sm in sms]
        for sm, s in zip(sms, scores(q_ref, j + 1)):
            sm.s[1 - cur][...] = s
        for sm in sms:
            sm.weigh(cur, scale, None, sm.block_shift(j))
        for sm, pend in zip(sms, pending):
            sm.fold(*pend)

    def pair(jj, carry):
        iteration(2 * jj, 0)
        iteration(2 * jj + 1, 1)
        return carry

    lax.fori_loop(0, i >> 1, pair, 0)

    @pl.when((i & 1) == 1)
    def _():
        iteration(i - 1, 0)
        for sm in sms:
            sm.s[0][...] = sm.s[1][...]

    pending = [sm.start_values(jnp.maximum(i - 1, 0)) for sm in sms]
    for sm in sms:
        sm.weigh(0, scale, sm.diagonal_bias(), sm.block_shift(i))
    last = [sm.start_values(i) for sm in sms]
    for sm, s in zip(sms, scores(q_next_ref, 0)):
        sm.s[0][...] = s
    for sm, pend in zip(sms, pending):
        sm.fold(*pend)
    for sm, pend in zip(sms, last):
        sm.fold(*pend)


def _next_query_spec(blocks, t, width):
    return pl.BlockSpec((None, t, width), lambda bi, g, i: (bi, jnp.minimum(i + 1, blocks - 1), g))


def _softmax_scratch(t):
    return [pltpu.VMEM((t, t), F32), pltpu.VMEM((t, t), F32), pltpu.VMEM((t, t), BF16), pltpu.VMEM((1, t), F32),
            pltpu.VMEM((1, t), F32), pltpu.VMEM((DV + ONES_ROWS, t), F32)]


def _key_query_iotas(t):
    return lax.broadcasted_iota(jnp.int32, (t, t), 0), lax.broadcasted_iota(jnp.int32, (t, t), 1)


def _diff_attn_kernel(q_ref, q_next_ref, k_ref, vt_ref, slope_ref, lq1_ref, lk1_ref, lq2_ref, lk2_ref, gsub_ref,
                      o_ref, bias_diag, *scratch, t, heads, lambda_init):
    i = pl.program_id(2)

    @pl.when(i == 0)
    def _():
        key, qry = _key_query_iotas(t)
        ahead = jnp.maximum(key - qry, 0).astype(F32)
        allowed = (key >> CHUNK_SHIFT) <= (qry >> CHUNK_SHIFT)
        for hh in range(heads):
            bias_diag[hh] = jnp.where(allowed, (-2.0 * LOG2E) * slope_ref[hh][:, :1] * ahead, NEG_BIG)

    sms = []
    for hh in range(heads):
        for half in range(2):
            n = 2 * hh + half
            sms.append(_Softmax(
                scratch[6 * n:6 * n + 6],
                values=lambda j, hh=hh: vt_ref[j, hh * DV:(hh + 1) * DV, :],
                block_shift=lambda j, hh=hh: (LOG2E * t) * slope_ref[hh][:, :1] * j.astype(F32),
                diagonal_bias=lambda hh=hh: bias_diag[hh]))

    def scores(q, j):
        rows = pl.ds(pl.multiple_of(j * t, t), t)
        for n in range(2 * heads):
            cols = slice(n * 128, (n + 1) * 128)
            yield _dot_nt(k_ref[rows, cols], q[:, cols])

    _run_key_blocks(i, sms, scores, q_ref, q_next_ref, LOG2E)

    lam = (jnp.exp(jnp.sum(lq1_ref[...] * lk1_ref[...], axis=-1, keepdims=True))
           - jnp.exp(jnp.sum(lq2_ref[...] * lk2_ref[...], axis=-1, keepdims=True)) + lambda_init)
    for hh in range(heads):
        o = (sms[2 * hh].normalized() - lam * sms[2 * hh + 1].normalized()).T
        o_ref[:, hh * DV:(hh + 1) * DV] = (_rms(o, gsub_ref[...]) * (1.0 - lambda_init)).astype(o_ref.dtype)


def _diff_attn(dq, dk, dv, lq1, lk1, lq2, lk2, g_sub, lambda_init, *, t, heads):
    b, s, _ = dq.shape
    slopes = jnp.asarray([_alibi_slope(h) for h in range(DIFF_HEADS)], F32)
    slopes = jnp.broadcast_to(slopes[:, None, None], (DIFF_HEADS, 1, LANES))
    vec = lambda a: a.reshape(1, -1)
    small = lambda width: pl.BlockSpec((1, width), lambda bi, g, i: (0, 0))
    kern = functools.partial(_diff_attn_kernel, t=t, heads=heads, lambda_init=lambda_init)
    return pl.pallas_call(
        kern,
        grid=(b, DIFF_HEADS // heads, s // t),
        in_specs=[
            pl.BlockSpec((None, t, 256 * heads), lambda bi, g, i: (bi, i, g)),
            _next_query_spec(s // t, t, 256 * heads),
            pl.BlockSpec((None, s, 256 * heads), lambda bi, g, i: (bi, 0, g)),
            pl.BlockSpec((None, s // t, DV * heads, t), lambda bi, g, i: (bi, 0, g, 0)),
            pl.BlockSpec((heads, 1, LANES), lambda bi, g, i: (g, 0, 0)),
            small(DIFF_DK), small(DIFF_DK), small(DIFF_DK), small(DIFF_DK),
            small(2 * DIFF_DK),
        ],
        out_specs=pl.BlockSpec((None, t, DV * heads), lambda bi, g, i: (bi, i, g)),
        out_shape=jax.ShapeDtypeStruct((b, s, DIFF_HEADS * DV), BF16),
        scratch_shapes=[pltpu.VMEM((heads, t, t), F32)] + _softmax_scratch(t) * (2 * heads),
        compiler_params=_params(3),
        name="diff_attn",
    )(dq, dq, dk, dv, slopes, vec(lq1), vec(lk1), vec(lq2), vec(lk2), vec(g_sub))


def _mla_attn_kernel(q_ref, q_next_ref, k_ref, vt_ref, o_ref, mask_diag, *scratch, t, heads):
    i = pl.program_id(2)

    @pl.when(i == 0)
    def _():
        key, qry = _key_query_iotas(t)
        mask_diag[...] = jnp.where((key >> CHUNK_SHIFT) <= (qry >> CHUNK_SHIFT), 0.0, NEG_BIG)

    sms = [_Softmax(scratch[6 * hh:6 * hh + 6],
                    values=lambda j, hh=hh: vt_ref[j, hh * DV:(hh + 1) * DV, :],
                    block_shift=lambda j: None,
                    diagonal_bias=lambda: mask_diag[...]) for hh in range(heads)]

    def scores(q, j):
        rows = pl.ds(pl.multiple_of(j * t, t), t)
        for hh in range(heads):
            cols = slice(hh * 256, (hh + 1) * 256)
            yield _dot_nt(k_ref[rows, cols], q[:, cols])

    _run_key_blocks(i, sms, scores, q_ref, q_next_ref, MLA_SCALE * LOG2E)
    for hh in range(heads):
        o_ref[:, hh * DV:(hh + 1) * DV] = sms[hh].normalized().T.astype(o_ref.dtype)


def _mla_attn(mq, mk, mv, *, t, heads):
    b, s, _ = mq.shape
    return pl.pallas_call(
        functools.partial(_mla_attn_kernel, t=t, heads=heads),
        grid=(b, MLA_HEADS // heads, s // t),
        in_specs=[
            pl.BlockSpec((None, t, 256 * heads), lambda bi, g, i: (bi, i, g)),
            _next_query_spec(s // t, t, 256 * heads),
            pl.BlockSpec((None, s, 256 * heads), lambda bi, g, i: (bi, 0, g)),
            pl.BlockSpec((None, s // t, DV * heads, t), lambda bi, g, i: (bi, 0, g, 0)),
        ],
        out_specs=pl.BlockSpec((None, t, DV * heads), lambda bi, g, i: (bi, i, g)),
        out_shape=jax.ShapeDtypeStruct((b, s, MLA_HEADS * DV), BF16),
        scratch_shapes=[pltpu.VMEM((t, t), F32)] + _softmax_scratch(t) * heads,
        compiler_params=_params(3),
        name="mla_attn",
    )(mq, mq, mk, mv)


def _sb_attn_kernel(q_ref, k_ref, v_ref, o_ref, carry_ref, acc_ref, *, t, pairs):
    i = pl.program_id(2)
    heads = 2 * pairs
    low = lax.broadcasted_iota(jnp.int32, (t, LANES), 1) < SB_DH
    q_heads = []
    for p in range(pairs):
        q = q_ref[:, p * LANES:(p + 1) * LANES]
        zero = jnp.zeros_like(q)
        q_heads += [jnp.where(low, q, zero), jnp.where(low, zero, q)]
    row, col = _key_query_iotas(t)
    later = (row > col).astype(BF16)
    strict = col < row

    def block(ref, j, h):
        return ref[pl.ds(pl.multiple_of(j * t, t), t), (h // 2) * LANES:(h // 2 + 1) * LANES]

    def keep_sums(z, mask):
        neg_z = -z
        log_keep = jnp.minimum(neg_z, 0.0) - jnp.log(1.0 + jnp.exp(jnp.minimum(z, neg_z)))
        if mask is not None:
            log_keep = jnp.where(mask, log_keep, 0.0)
        sums = _dot(log_keep.astype(BF16), later)
        return z + log_keep, jnp.sum(log_keep, axis=-1, keepdims=True), sums

    def weights(log_beta, sums, carry, mask):
        between = sums if carry is None else sums + carry
        a = jnp.exp(log_beta + between)
        if mask is not None:
            a = jnp.where(mask, a, 0.0)
        return a.astype(BF16)

    def near(with_previous):
        back = (0, 1) if with_previous else (0,)
        units = [(h, d) for h in range(heads) for d in back]

        def scores(u):
            h, d = units[u]
            return _dot_nt(q_heads[h], block(k_ref, i - d, h))

        def finish(u, staged, pending):
            h, d = units[u]
            log_beta, total, sums = staged[u]
            if d == 0:
                pv = _dot(weights(log_beta, sums, None, strict), block(v_ref, i, h))
                carry = total
            else:
                carry_diag, pv_diag = pending.pop(h)
                pv = pv_diag + _dot(weights(log_beta, sums, carry_diag, None), block(v_ref, i - 1, h))
                carry = carry_diag + total
            if d == back[-1]:
                acc_ref[h] = pv
                carry_ref[h] = carry
            else:
                pending[h] = (carry, pv)

        z = {u: scores(u) for u in range(min(2, len(units)))}
        staged, pending = {}, {}
        for u in range(len(units)):
            staged[u] = keep_sums(z.pop(u), strict if units[u][1] == 0 else None)
            if u + 2 < len(units):
                z[u + 2] = scores(u + 2)
            if u >= 2:
                finish(u - 2, staged, pending)
        carries = [sum(staged[u][1] for u in range(len(units)) if units[u][0] == h) for h in range(heads)]
        go_on = alive(functools.reduce(jnp.maximum, carries))
        for u in range(max(len(units) - 2, 0), len(units)):
            finish(u, staged, pending)
        return go_on

    def alive(carry):
        return jnp.max(carry) > EXP_UNDERFLOW

    def far(state):
        j = i - 2 - state[0]
        z = [_dot_nt(q_heads[h], block(k_ref, j, h)) for h in range(heads)]
        staged = [keep_sums(z[h], None) for h in range(heads)]
        for h in range(heads):
            log_beta, total, sums = staged[h]
            acc_ref[h] += _dot(weights(log_beta, sums, carry_ref[h], None), block(v_ref, j, h))
            carry_ref[h] += total
        return state[0] + 1, alive(carry_ref[...])

    @pl.when(i == 0)
    def _():
        near(False)

    @pl.when(i > 0)
    def _():
        go_on = near(True)
        lax.while_loop(lambda state: jnp.logical_and(state[0] < i - 1, state[1]), far, (jnp.int32(0), go_on))

    for p in range(pairs):
        o_ref[:, p * LANES:(p + 1) * LANES] = jnp.where(low, acc_ref[2 * p], acc_ref[2 * p + 1]).astype(o_ref.dtype)


def _sb_attn(qkv, *, t, pairs):
    b, s, _ = qkv.shape
    groups = SB_HEADS * SB_DH // (LANES * pairs)
    width = LANES * pairs
    return pl.pallas_call(
        functools.partial(_sb_attn_kernel, t=t, pairs=pairs),
        grid=(b, groups, s // t),
        in_specs=[
            pl.BlockSpec((None, t, width), lambda bi, g, i: (bi, i, g)),
            pl.BlockSpec((None, s, width), lambda bi, g, i: (bi, 0, groups + g)),
            pl.BlockSpec((None, s, width), lambda bi, g, i: (bi, 0, 2 * groups + g)),
        ],
        out_specs=pl.BlockSpec((None, t, width), lambda bi, g, i: (bi, i, g)),
        out_shape=jax.ShapeDtypeStruct((b, s, SB_HEADS * SB_DH), BF16),
        scratch_shapes=[pltpu.VMEM((2 * pairs, t, 1), F32), pltpu.VMEM((2 * pairs, t, LANES), F32)],
        compiler_params=_params(3),
        name="sb_attn",
    )(qkv, qkv, qkv)


def _cross_kernel(h_ref, mix_a_ref, mix_b_ref, wmix_ref, g_ref, wq_ref, kv_ref, wo_ref, out_ref, o_scr):
    d = h_ref.shape[1]
    dh = d // XA_HEADS
    half = mix_a_ref.shape[1]
    h1 = h_ref[...] + (_dot(mix_a_ref[...], wmix_ref[:half, :]) + _dot(mix_b_ref[...], wmix_ref[half:, :]))
    q = _dot(_rms(h1, g_ref[...]).astype(BF16), wq_ref[...]).astype(BF16)
    for hd in range(XA_HEADS):
        s = _dot_nt(q[:, hd * dh:(hd + 1) * dh], kv_ref[:, hd * dh:(hd + 1) * dh])
        p = jnp.exp(s - jnp.max(s, axis=-1, keepdims=True))
        o = _dot(p.astype(BF16), kv_ref[:, d + hd * dh:d + (hd + 1) * dh]) / jnp.sum(p, axis=-1, keepdims=True)
        o_scr[:, hd * dh:(hd + 1) * dh] = o.astype(BF16)
    out_ref[...] = h1 + _dot(o_scr[...], wo_ref[...])


def _cross(h, mix_a, mix_b, w_mix, g, wq, mem_kv, wo, *, seq, tm):
    n, d = h.shape
    blocks = seq // tm
    mem_len = mem_kv.shape[1]
    dh = d // XA_HEADS
    half = w_mix.shape[0] // 2
    (mix_a, col_a), (mix_b, col_b) = mix_a, mix_b
    return pl.pallas_call(
        _cross_kernel,
        grid=(n // tm,),
        in_specs=[
            pl.BlockSpec((tm, d), lambda i: (i, 0)),
            pl.BlockSpec((tm, half), lambda i: (i, col_a)),
            pl.BlockSpec((tm, half), lambda i: (i, col_b)),
            _resident(w_mix.shape),
            _resident((1, d)),
            _resident(wq.shape),
            pl.BlockSpec((None, mem_len, 2 * d), lambda i: (i // blocks, 0, 0)),
            _resident(wo.shape),
        ],
        out_specs=pl.BlockSpec((tm, d), lambda i: (i, 0)),
        out_shape=jax.ShapeDtypeStruct((n, d), F32),
        scratch_shapes=[pltpu.VMEM((tm, d), BF16)],
        compiler_params=_params(1),
        name="cross_attn",
    )(h, mix_a, mix_b, w_mix.astype(BF16), g.reshape(1, -1), (wq * dh ** -0.5).astype(BF16), mem_kv, wo.astype(BF16))


def _mlp_kernel(h_ref, g_ref, w1_ref, w2_ref, gf_ref, out_ref, *, tf, final_norm):
    x = h_ref[...]
    xn = _rms(x, g_ref[...]).astype(BF16)
    acc = x
    for c in range(w1_ref.shape[1] // tf):
        a = jnp.maximum(_dot(xn, w1_ref[:, c * tf:(c + 1) * tf]), 0.0)
        acc = acc + _dot((a * a).astype(BF16), w2_ref[c * tf:(c + 1) * tf, :])
    if final_norm:
        acc = _rms(acc, gf_ref[...])
    out_ref[...] = acc


def _mlp(h, g, w1, w2, g_final, *, tm, tf, final_norm):
    n, d = h.shape
    return pl.pallas_call(
        functools.partial(_mlp_kernel, tf=tf, final_norm=final_norm),
        grid=(n // tm,),
        in_specs=[
            pl.BlockSpec((tm, d), lambda i: (i, 0)),
            _resident((1, d)),
            _resident(w1.shape),
            _resident(w2.shape),
            _resident((1, d)),
        ],
        out_specs=pl.BlockSpec((tm, d), lambda i: (i, 0)),
        out_shape=jax.ShapeDtypeStruct((n, d), F32),
        compiler_params=_params(1),
        name="mlp",
    )(h, g.reshape(1, -1), w1.astype(BF16), w2.astype(BF16), g_final.reshape(1, -1))


def _rope_tables(seq):
    half = MLA_ROPE // 2
    inv = (ROPE_BASE ** (-np.arange(half, dtype=np.float32) / half)).astype(np.float32)
    ang = np.arange(seq, dtype=np.float32)[:, None] * inv[None, :]
    pad = lambda a: np.concatenate([a, a, np.zeros((seq, LANES - 2 * half), np.float32)], axis=1).astype(np.float32)
    return jnp.asarray(pad(np.cos(ang))), jnp.asarray(pad(np.sin(ang)))


def kernel(x, mem, ev_norm, ev_w_in, diff_lq1, diff_lk1, diff_lq2, diff_lk2, diff_subln, mla_g_cq, mla_w_uq, mla_g_ckv, mla_w_ukv, ev_w_out, od_norm, sb_w_in, sb_w_out, xa_norm, xa_mem_norm, xa_wq, xa_wkv, xa_wo, mlp_norm, mlp_w1, mlp_w2, final_norm):
    b, seq, d = x.shape
    mem_len = mem.shape[1]
    depth = xa_norm.shape[0]
    tm = ROW_TILE
    h = x.reshape(b * seq, d)
    mem2 = mem.reshape(b * mem_len, d)
    cos_pad, sin_pad = _rope_tables(seq)
    for i in range(depth):
        j = i // 2
        if i % 2 == 0:
            lambda_init = 0.8 - 0.6 * math.exp(-0.3 * i)
            dq, dk, dv, mq, mk, mv = _even_proj(
                h, ev_norm[j], ev_w_in[j], mla_g_cq[j], mla_w_uq[j], mla_g_ckv[j], mla_w_ukv[j],
                cos_pad, sin_pad, seq=seq, tm=WIDE_ROW_TILE, t=tm)
            split = lambda a: a.reshape(b, seq, a.shape[-1])
            o_a = _diff_attn(split(dq), split(dk), dv, diff_lq1[j], diff_lk1[j], diff_lq2[j], diff_lk2[j],
                             diff_subln[j], lambda_init, t=tm, heads=DIFF_HEADS_PER_STEP)
            o_b = _mla_attn(split(mq), split(mk), mv, t=tm, heads=MLA_HEADS_PER_STEP)
            mix_a, mix_b = (o_a.reshape(b * seq, -1), 0), (o_b.reshape(b * seq, -1), 0)
            w_mix = ev_w_out[j]
        else:
            qkv = _norm_matmul(h, od_norm[j].reshape(1, -1), sb_w_in[j].astype(BF16), tm=WIDE_ROW_TILE, name="sb_proj",
                               scaled_cols=SB_HEADS * SB_DH, scale=SB_DH ** -0.5)
            mix = _sb_attn(qkv.reshape(b, seq, -1), t=SB_BLOCK, pairs=SB_PAIRS_PER_STEP).reshape(b * seq, -1)
            mix_a, mix_b = (mix, 0), (mix, 1)
            w_mix = sb_w_out[j]
        mem_kv = _norm_matmul(mem2, xa_mem_norm[i].reshape(1, -1), xa_wkv[i].astype(BF16), tm=mem_len, name="mem_kv")
        h = _cross(h, mix_a, mix_b, w_mix, xa_norm[i], xa_wq[i], mem_kv.reshape(b, mem_len, -1), xa_wo[i], seq=seq, tm=WIDE_ROW_TILE)
        h = _mlp(h, mlp_norm[i], mlp_w1[i], mlp_w2[i], final_norm, tm=WIDE_ROW_TILE, tf=MLP_FF_TILE, final_norm=(i == depth - 1))
    return h.reshape(b, seq, d)
```

```python
import functools
import math

import jax
import jax.numpy as jnp
import numpy as np
from jax import lax
from jax.experimental import pallas as pl
from jax.experimental.pallas import tpu as pltpu

F32 = jnp.float32
BF16 = jnp.bfloat16

EPS = 1e-6
CHUNK_SHIFT = 6
DIFF_HEADS = 4
DIFF_DK = 64
MLA_HEADS = 4
MLA_NOPE = 128
MLA_ROPE = 64
MLA_SCALE = (MLA_NOPE + MLA_ROPE) ** -0.5
ROPE_BASE = 10000.0
SB_HEADS = 16
SB_DH = 64
XA_HEADS = 4
NEG_BIG = -1e30
LOG2E = math.log2(math.e)
EXP_UNDERFLOW = -104.0
LANES = 128
DV = 128
ONES_ROWS = 16

VMEM_LIMIT_BYTES = 56 * 1024 * 1024

ROW_TILE = 512
WIDE_ROW_TILE = 1024
MLP_FF_TILE = 1024
DIFF_HEADS_PER_STEP = 2
MLA_HEADS_PER_STEP = 2
SB_BLOCK = 256
SB_PAIRS_PER_STEP = 8


def _params(n_grid):
    return pltpu.CompilerParams(
        dimension_semantics=("arbitrary",) * n_grid, vmem_limit_bytes=VMEM_LIMIT_BYTES
    )


def _resident(shape):
    zeros = (0,) * len(shape)
    return pl.BlockSpec(shape, lambda *_: zeros, pipeline_mode=pl.Buffered(1))


def _rms(x, g):
    ms = jnp.mean(x * x, axis=-1, keepdims=True)
    return x * lax.rsqrt(ms + EPS) * g


def _dot(a, b):
    return jnp.dot(a, b, preferred_element_type=F32)


def _dot_nt(a, b):
    return lax.dot_general(a, b, (((1,), (1,)), ((), ())), preferred_element_type=F32)


def _norm_matmul_kernel(x_ref, g_ref, w_ref, o_ref, *, scaled_cols, scale):
    xn = _rms(x_ref[...], g_ref[...]).astype(BF16)
    if scaled_cols:
        o_ref[:, :scaled_cols] = (_dot(xn, w_ref[:, :scaled_cols]) * scale).astype(o_ref.dtype)
    o_ref[:, scaled_cols:] = _dot(xn, w_ref[:, scaled_cols:]).astype(o_ref.dtype)


def _norm_matmul(x, g, w, *, tm, name, scaled_cols=0, scale=1.0):
    n, d = x.shape
    f = w.shape[1]
    return pl.pallas_call(
        functools.partial(_norm_matmul_kernel, scaled_cols=scaled_cols, scale=scale),
        grid=(n // tm,),
        in_specs=[
            pl.BlockSpec((tm, d), lambda i: (i, 0)),
            _resident((1, d)),
            _resident((d, f)),
        ],
        out_specs=pl.BlockSpec((tm, f), lambda i: (i, 0)),
        out_shape=jax.ShapeDtypeStruct((n, f), BF16),
        compiler_params=_params(1),
        name=name,
    )(x, g, w)


def _even_proj_kernel(h_ref, g_ref, w_ref, wkr_ref, gcq_ref, wq_ref, gckv_ref, wkv_ref, cos_ref, sin_ref,
                      dq_ref, dk_ref, dv_ref, mq_ref, mk_ref, mv_ref):
    tm = h_ref.shape[0]
    xn = _rms(h_ref[...], g_ref[...]).astype(BF16)
    proj = _dot(xn, w_ref[...])
    key_rope = _dot(xn, wkr_ref[...])
    lane = lax.broadcasted_iota(jnp.int32, (tm, LANES), 1)
    low = lane < DIFF_DK
    t = dv_ref.shape[-1]
    frame_in_block = (lax.broadcasted_iota(jnp.int32, (tm, LANES), 0) & (t - 1)).astype(F32)
    ones = jnp.where(lane < DIFF_DK + 3, 1.0, 0.0)
    for h in range(DIFF_HEADS):
        b = frame_in_block * _alibi_slope(h)
        b_hi = b.astype(BF16).astype(F32)
        b_mid = (b - b_hi).astype(BF16).astype(F32)
        bias = jnp.where(lane == DIFF_DK, b_hi, jnp.where(lane == DIFF_DK + 1, b_mid, b - b_hi - b_mid))
        bias = jnp.where(lane < DIFF_DK + 3, bias, 0.0)
        q_pair = proj[:, h * 128:(h + 1) * 128] * DIFF_DK ** -0.5
        k_pair = proj[:, 512 + h * 128:512 + (h + 1) * 128]
        for half, (q, k) in enumerate(((q_pair, k_pair), (pltpu.roll(q_pair, DIFF_DK, 1), pltpu.roll(k_pair, DIFF_DK, 1)))):
            cols = slice(h * 256 + half * 128, h * 256 + (half + 1) * 128)
            dq_ref[:, cols] = jnp.where(low, q, ones).astype(BF16)
            dk_ref[:, cols] = jnp.where(low, k, bias).astype(BF16)
    for r in range(tm // t):
        dv_ref[r] = proj[r * t:(r + 1) * t, 1024:1536].T.astype(BF16)

    cos = cos_ref[...]
    sin = sin_ref[...]
    k_rope = (key_rope[:, :128] * cos + key_rope[:, 128:] * sin).astype(BF16)

    cqn = _rms(proj[:, 1536:1792], gcq_ref[...]).astype(BF16)
    qm = _dot(cqn, wq_ref[...])
    ckvn = _rms(proj[:, 1792:1920], gckv_ref[...]).astype(BF16)
    kv = _dot(ckvn, wkv_ref[...])
    for h in range(MLA_HEADS):
        mq_ref[:, h * 256:h * 256 + 128] = qm[:, h * 128:(h + 1) * 128].astype(BF16)
        q_rope = qm[:, 512 + h * 128:512 + (h + 1) * 128] * cos + qm[:, 1024 + h * 128:1024 + (h + 1) * 128] * sin
        mq_ref[:, h * 256 + 128:(h + 1) * 256] = q_rope.astype(BF16)
        mk_ref[:, h * 256:h * 256 + 128] = kv[:, h * 256:h * 256 + 128].astype(BF16)
        mk_ref[:, h * 256 + 128:(h + 1) * 256] = k_rope
        for r in range(tm // t):
            mv_ref[r, h * 128:(h + 1) * 128, :] = kv[r * t:(r + 1) * t, h * 256 + 128:(h + 1) * 256].T.astype(BF16)


def _alibi_slope(h):
    return 2.0 ** (-8.0 * (h + 1) / DIFF_HEADS)


def _rotated(w):
    half = w.shape[1] // 2
    return jnp.concatenate([-w[:, half:], w[:, :half]], axis=1)


def _pad_cols(w, width):
    return jnp.pad(w, ((0, 0), (0, width - w.shape[1])))


def _even_proj(h, g, w_in, g_cq, w_uq, g_ckv, w_ukv, cos_pad, sin_pad, *, seq, tm, t):
    n, d = h.shape
    kr = w_in[:, 1920:1984]
    w_main = w_in[:, :1920].astype(BF16)
    w_kr = jnp.concatenate([_pad_cols(kr, 128), _pad_cols(_rotated(kr), 128)], axis=1).astype(BF16)
    per_head = MLA_NOPE + MLA_ROPE
    nope = [w_uq[:, hh * per_head:hh * per_head + MLA_NOPE] for hh in range(MLA_HEADS)]
    rope = [w_uq[:, hh * per_head + MLA_NOPE:(hh + 1) * per_head] for hh in range(MLA_HEADS)]
    wq_ext = jnp.concatenate(
        nope + [_pad_cols(r, 128) for r in rope] + [_pad_cols(_rotated(r), 128) for r in rope], axis=1).astype(BF16)
    pos_blocks = seq // tm
    bf = lambda width: jax.ShapeDtypeStruct((n, width), BF16)
    row = lambda width: pl.BlockSpec((tm, width), lambda i: (i, 0))
    vt_shape = jax.ShapeDtypeStruct((n // seq, seq // t, 512, t), BF16)
    vt_spec = pl.BlockSpec((None, tm // t, 512, t), lambda i: (i // pos_blocks, i % pos_blocks, 0, 0))
    return pl.pallas_call(
        _even_proj_kernel,
        grid=(n // tm,),
        in_specs=[
            row(d),
            _resident((1, d)),
            _resident(w_main.shape),
            _resident(w_kr.shape),
            _resident((1, g_cq.shape[-1])),
            _resident(wq_ext.shape),
            _resident((1, g_ckv.shape[-1])),
            _resident(w_ukv.shape),
            pl.BlockSpec((tm, 128), lambda i: (i % pos_blocks, 0)),
            pl.BlockSpec((tm, 128), lambda i: (i % pos_blocks, 0)),
        ],
        out_specs=[row(1024), row(1024), vt_spec, row(1024), row(1024), vt_spec],
        out_shape=[bf(1024), bf(1024), vt_shape, bf(1024), bf(1024), vt_shape],
        compiler_params=_params(1),
        name="even_proj",
    )(h, g.reshape(1, -1), w_main, w_kr, g_cq.reshape(1, -1), wq_ext, g_ckv.reshape(1, -1), w_ukv.astype(BF16),
      cos_pad, sin_pad)


class _Softmax:
    def __init__(self, scratch, values, block_shift, diagonal_bias):
        s0, s1, self.p, self.a, self.m, self.acc = scratch
        self.s = (s0, s1)
        self.values, self.block_shift, self.diagonal_bias = values, block_shift, diagonal_bias

    def reset(self):
        self.m[...] = jnp.full(self.m.shape, NEG_BIG, F32)
        self.acc[...] = jnp.zeros(self.acc.shape, F32)
        self.p[...] = jnp.zeros(self.p.shape, BF16)
        self.a[...] = jnp.ones(self.a.shape, F32)

    def weigh(self, cur, scale, bias, shift):
        m_prev = self.m[...]
        if bias is None:
            m_block = jnp.max(self.s[cur][...], axis=0, keepdims=True) * scale
            m_new = jnp.maximum(m_prev, m_block if shift is None else m_block + shift)
            p = jnp.exp2(self.s[cur][...] * scale - (m_new if shift is None else m_new - shift))
        else:
            logits = self.s[cur][...] * scale + bias
            m_block = jnp.max(logits, axis=0, keepdims=True)
            m_new = jnp.maximum(m_prev, m_block if shift is None else m_block + shift)
            p = jnp.exp2(logits - (m_new if shift is None else m_new - shift))
        self.m[...] = m_new
        self.p[...] = p.astype(BF16)
        self.a[...] = jnp.exp2(m_prev - m_new)

    def start_values(self, j):
        vt = self.values(j)
        vt_ones = jnp.concatenate([vt, jnp.ones((ONES_ROWS, vt.shape[1]), BF16)], axis=0)
        return self.a[...], _dot(vt_ones, self.p[...])

    def fold(self, alpha, pv):
        self.acc[...] = alpha * self.acc[...] + pv

    def normalized(self):
        return self.acc[:DV, :] / self.acc[DV:DV + 1, :]


def _run_key_blocks(i, sms, scores, q_ref, q_next_ref, scale):
    @pl.when(i == 0)
    def _():
        for sm, s in zip(sms, scores(q_ref, 0)):
            sm.s[0][...] = s

    for sm in sms:
        sm.reset()

    def iteration(j, cur):
        pending = [sm.start_values(jnp.maximum(j - 1, 0)) for sm in sms]
        for sm, s in zip(sms, scores(q_ref, j + 1)):
            sm.s[1 - cur][...] = s
        for sm in sms:
            sm.weigh(cur, scale, None, sm.block_shift(j))
        for sm, pend in zip(sms, pending):
            sm.fold(*pend)

    def pair(jj, carry):
        iteration(2 * jj, 0)
        iteration(2 * jj + 1, 1)
        return carry

    lax.fori_loop(0, i >> 1, pair, 0)

    @pl.when((i & 1) == 1)
    def _():
        iteration(i - 1, 0)
        for sm in sms:
            sm.s[0][...] = sm.s[1][...]

    pending = [sm.start_values(jnp.maximum(i - 1, 0)) for sm in sms]
    for sm in sms:
        sm.weigh(0, scale, sm.diagonal_bias(), sm.block_shift(i))
    last = [sm.start_values(i) for sm in sms]
    for sm, s in zip(sms, scores(q_next_ref, 0)):
        sm.s[0][...] = s
    for sm, pend in zip(sms, pending):
        sm.fold(*pend)
    for sm, pend in zip(sms, last):
        sm.fold(*pend)


def _next_query_spec(blocks, t, width):
    return pl.BlockSpec((None, t, width), lambda bi, g, i: (bi, jnp.minimum(i + 1, blocks - 1), g))


def _softmax_scratch(t):
    return [pltpu.VMEM((t, t), F32), pltpu.VMEM((t, t), F32), pltpu.VMEM((t, t), BF16), pltpu.VMEM((1, t), F32),
            pltpu.VMEM((1, t), F32), pltpu.VMEM((DV + ONES_ROWS, t), F32)]


def _key_query_iotas(t):
    return lax.broadcasted_iota(jnp.int32, (t, t), 0), lax.broadcasted_iota(jnp.int32, (t, t), 1)


def _diff_attn_kernel(q_ref, q_next_ref, k_ref, vt_ref, slope_ref, lq1_ref, lk1_ref, lq2_ref, lk2_ref, gsub_ref,
                      o_ref, bias_diag, *scratch, t, heads, lambda_init):
    i = pl.program_id(2)

    @pl.when(i == 0)
    def _():
        key, qry = _key_query_iotas(t)
        ahead = jnp.maximum(key - qry, 0).astype(F32)
        allowed = (key >> CHUNK_SHIFT) <= (qry >> CHUNK_SHIFT)
        for hh in range(heads):
            bias_diag[hh] = jnp.where(allowed, (-2.0 * LOG2E) * slope_ref[hh][:, :1] * ahead, NEG_BIG)

    sms = []
    for hh in range(heads):
        for half in range(2):
            n = 2 * hh + half
            sms.append(_Softmax(
                scratch[6 * n:6 * n + 6],
                values=lambda j, hh=hh: vt_ref[j, hh * DV:(hh + 1) * DV, :],
                block_shift=lambda j, hh=hh: (LOG2E * t) * slope_ref[hh][:, :1] * j.astype(F32),
                diagonal_bias=lambda hh=hh: bias_diag[hh]))

    def scores(q, j):
        rows = pl.ds(pl.multiple_of(j * t, t), t)
        for n in range(2 * heads):
            cols = slice(n * 128, (n + 1) * 128)
            yield _dot_nt(k_ref[rows, cols], q[:, cols])

    _run_key_blocks(i, sms, scores, q_ref, q_next_ref, LOG2E)

    lam = (jnp.exp(jnp.sum(lq1_ref[...] * lk1_ref[...], axis=-1, keepdims=True))
           - jnp.exp(jnp.sum(lq2_ref[...] * lk2_ref[...], axis=-1, keepdims=True)) + lambda_init)
    for hh in range(heads):
        o = (sms[2 * hh].normalized() - lam * sms[2 * hh + 1].normalized()).T
        o_ref[:, hh * DV:(hh + 1) * DV] = (_rms(o, gsub_ref[...]) * (1.0 - lambda_init)).astype(o_ref.dtype)


def _diff_attn(dq, dk, dv, lq1, lk1, lq2, lk2, g_sub, lambda_init, *, t, heads):
    b, s, _ = dq.shape
    slopes = jnp.asarray([_alibi_slope(h) for h in range(DIFF_HEADS)], F32)
    slopes = jnp.broadcast_to(slopes[:, None, None], (DIFF_HEADS, 1, LANES))
    vec = lambda a: a.reshape(1, -1)
    small = lambda width: pl.BlockSpec((1, width), lambda bi, g, i: (0, 0))
    kern = functools.partial(_diff_attn_kernel, t=t, heads=heads, lambda_init=lambda_init)
    return pl.pallas_call(
        kern,
        grid=(b, DIFF_HEADS // heads, s // t),
        in_specs=[
            pl.BlockSpec((None, t, 256 * heads), lambda bi, g, i: (bi, i, g)),
            _next_query_spec(s // t, t, 256 * heads),
            pl.BlockSpec((None, s, 256 * heads), lambda bi, g, i: (bi, 0, g)),
            pl.BlockSpec((None, s // t, DV * heads, t), lambda bi, g, i: (bi, 0, g, 0)),
            pl.BlockSpec((heads, 1, LANES), lambda bi, g, i: (g, 0, 0)),
            small(DIFF_DK), small(DIFF_DK), small(DIFF_DK), small(DIFF_DK),
            small(2 * DIFF_DK),
        ],
        out_specs=pl.BlockSpec((None, t, DV * heads), lambda bi, g, i: (bi, i, g)),
        out_shape=jax.ShapeDtypeStruct((b, s, DIFF_HEADS * DV), BF16),
        scratch_shapes=[pltpu.VMEM((heads, t, t), F32)] + _softmax_scratch(t) * (2 * heads),
        compiler_params=_params(3),
        name="diff_attn",
    )(dq, dq, dk, dv, slopes, vec(lq1), vec(lk1), vec(lq2), vec(lk2), vec(g_sub))


def _mla_attn_kernel(q_ref, q_next_ref, k_ref, vt_ref, o_ref, mask_diag, *scratch, t, heads):
    i = pl.program_id(2)

    @pl.when(i == 0)
    def _():
        key, qry = _key_query_iotas(t)
        mask_diag[...] = jnp.where((key >> CHUNK_SHIFT) <= (qry >> CHUNK_SHIFT), 0.0, NEG_BIG)

    sms = [_Softmax(scratch[6 * hh:6 * hh + 6],
                    values=lambda j, hh=hh: vt_ref[j, hh * DV:(hh + 1) * DV, :],
                    block_shift=lambda j: None,
                    diagonal_bias=lambda: mask_diag[...]) for hh in range(heads)]

    def scores(q, j):
        rows = pl.ds(pl.multiple_of(j * t, t), t)
        for hh in range(heads):
            cols = slice(hh * 256, (hh + 1) * 256)
            yield _dot_nt(k_ref[rows, cols], q[:, cols])

    _run_key_blocks(i, sms, scores, q_ref, q_next_ref, MLA_SCALE * LOG2E)
    for hh in range(heads):
        o_ref[:, hh * DV:(hh + 1) * DV] = sms[hh].normalized().T.astype(o_ref.dtype)


def _mla_attn(mq, mk, mv, *, t, heads):
    b, s, _ = mq.shape
    return pl.pallas_call(
        functools.partial(_mla_attn_kernel, t=t, heads=heads),
        grid=(b, MLA_HEADS // heads, s // t),
        in_specs=[
            pl.BlockSpec((None, t, 256 * heads), lambda bi, g, i: (bi, i, g)),
            _next_query_spec(s // t, t, 256 * heads),
            pl.BlockSpec((None, s, 256 * heads), lambda bi, g, i: (bi, 0, g)),
            pl.BlockSpec((None, s // t, DV * heads, t), lambda bi, g, i: (bi, 0, g, 0)),
        ],
        out_specs=pl.BlockSpec((None, t, DV * heads), lambda bi, g, i: (bi, i, g)),
        out_shape=jax.ShapeDtypeStruct((b, s, MLA_HEADS * DV), BF16),
        scratch_shapes=[pltpu.VMEM((t, t), F32)] + _softmax_scratch(t) * heads,
        compiler_params=_params(3),
        name="mla_attn",
    )(mq, mq, mk, mv)


def _sb_attn_kernel(q_ref, k_ref, v_ref, o_ref, carry_ref, acc_ref, *, t, pairs):
    i = pl.program_id(2)
    heads = 2 * pairs
    low = lax.broadcasted_iota(jnp.int32, (t, LANES), 1) < SB_DH
    q_heads = []
    for p in range(pairs):
        q = q_ref[:, p * LANES:(p + 1) * LANES]
        zero = jnp.zeros_like(q)
        q_heads += [jnp.where(low, q, zero), jnp.where(low, zero, q)]
    row, col = _key_query_iotas(t)
    later = (row > col).astype(BF16)
    strict = col < row

    def block(ref, j, h):
        return ref[pl.ds(pl.multiple_of(j * t, t), t), (h // 2) * LANES:(h // 2 + 1) * LANES]

    def keep_sums(z, mask):
        neg_z = -z
        log_keep = jnp.minimum(neg_z, 0.0) - jnp.log(1.0 + jnp.exp(jnp.minimum(z, neg_z)))
        if mask is not None:
            log_keep = jnp.where(mask, log_keep, 0.0)
        sums = _dot(log_keep.astype(BF16), later)
        return z + log_keep, jnp.sum(log_keep, axis=-1, keepdims=True), sums

    def weights(log_beta, sums, carry, mask):
        between = sums if carry is None else sums + carry
        a = jnp.exp(log_beta + between)
        if mask is not None:
            a = jnp.where(mask, a, 0.0)
        return a.astype(BF16)

    def near(with_previous):
        back = (0, 1) if with_previous else (0,)
        units = [(h, d) for h in range(heads) for d in back]

        def scores(u):
            h, d = units[u]
            return _dot_nt(q_heads[h], block(k_ref, i - d, h))

        def finish(u, staged, pending):
            h, d = units[u]
            log_beta, total, sums = staged[u]
            if d == 0:
                pv = _dot(weights(log_beta, sums, None, strict), block(v_ref, i, h))
                carry = total
            else:
                carry_diag, pv_diag = pending.pop(h)
                pv = pv_diag + _dot(weights(log_beta, sums, carry_diag, None), block(v_ref, i - 1, h))
                carry = carry_diag + total
            if d == back[-1]:
                acc_ref[h] = pv
                carry_ref[h] = carry
            else:
                pending[h] = (carry, pv)

        z = {u: scores(u) for u in range(min(2, len(units)))}
        staged, pending = {}, {}
        for u in range(len(units)):
            staged[u] = keep_sums(z.pop(u), strict if units[u][1] == 0 else None)
            if u + 2 < len(units):
                z[u + 2] = scores(u + 2)
            if u >= 2:
                finish(u - 2, staged, pending)
        carries = [sum(staged[u][1] for u in range(len(units)) if units[u][0] == h) for h in range(heads)]
        go_on = alive(functools.reduce(jnp.maximum, carries))
        for u in range(max(len(units) - 2, 0), len(units)):
            finish(u, staged, pending)
        return go_on

    def alive(carry):
        return jnp.max(carry) > EXP_UNDERFLOW

    def far(state):
        j = i - 2 - state[0]
        z = [_dot_nt(q_heads[h], block(k_ref, j, h)) for h in range(heads)]
        staged = [keep_sums(z[h], None) for h in range(heads)]
        for h in range(heads):
            log_beta, total, sums = staged[h]
            acc_ref[h] += _dot(weights(log_beta, sums, carry_ref[h], None), block(v_ref, j, h))
            carry_ref[h] += total
        return state[0] + 1, alive(carry_ref[...])

    @pl.when(i == 0)
    def _():
        near(False)

    @pl.when(i > 0)
    def _():
        go_on = near(True)
        lax.while_loop(lambda state: jnp.logical_and(state[0] < i - 1, state[1]), far, (jnp.int32(0), go_on))

    for p in range(pairs):
        o_ref[:, p * LANES:(p + 1) * LANES] = jnp.where(low, acc_ref[2 * p], acc_ref[2 * p + 1]).astype(o_ref.dtype)


def _sb_attn(qkv, *, t, pairs):
    b, s, _ = qkv.shape
    groups = SB_HEADS * SB_DH // (LANES * pairs)
    width = LANES * pairs
    return pl.pallas_call(
        functools.partial(_sb_attn_kernel, t=t, pairs=pairs),
        grid=(b, groups, s // t),
        in_specs=[
            pl.BlockSpec((None, t, width), lambda bi, g, i: (bi, i, g)),
            pl.BlockSpec((None, s, width), lambda bi, g, i: (bi, 0, groups + g)),
            pl.BlockSpec((None, s, width), lambda bi, g, i: (bi, 0, 2 * groups + g)),
        ],
        out_specs=pl.BlockSpec((None, t, width), lambda bi, g, i: (bi, i, g)),
        out_shape=jax.ShapeDtypeStruct((b, s, SB_HEADS * SB_DH), BF16),
        scratch_shapes=[pltpu.VMEM((2 * pairs, t, 1), F32), pltpu.VMEM((2 * pairs, t, LANES), F32)],
        compiler_params=_params(3),
        name="sb_attn",
    )(qkv, qkv, qkv)


def _cross_kernel(h_ref, mix_a_ref, mix_b_ref, wmix_ref, g_ref, wq_ref, kv_ref, wo_ref, out_ref, o_scr):
    d = h_ref.shape[1]
    dh = d // XA_HEADS
    half = mix_a_ref.shape[1]
    h1 = h_ref[...] + (_dot(mix_a_ref[...], wmix_ref[:half, :]) + _dot(mix_b_ref[...], wmix_ref[half:, :]))
    q = _dot(_rms(h1, g_ref[...]).astype(BF16), wq_ref[...]).astype(BF16)
    for hd in range(XA_HEADS):
        s = _dot_nt(q[:, hd * dh:(hd + 1) * dh], kv_ref[:, hd * dh:(hd + 1) * dh])
        p = jnp.exp(s - jnp.max(s, axis=-1, keepdims=True))
        o = _dot(p.astype(BF16), kv_ref[:, d + hd * dh:d + (hd + 1) * dh]) / jnp.sum(p, axis=-1, keepdims=True)
        o_scr[:, hd * dh:(hd + 1) * dh] = o.astype(BF16)
    out_ref[...] = h1 + _dot(o_scr[...], wo_ref[...])


def _cross(h, mix_a, mix_b, w_mix, g, wq, mem_kv, wo, *, seq, tm):
    n, d = h.shape
    blocks = seq // tm
    mem_len = mem_kv.shape[1]
    dh = d // XA_HEADS
    half = w_mix.shape[0] // 2
    (mix_a, col_a), (mix_b, col_b) = mix_a, mix_b
    return pl.pallas_call(
        _cross_kernel,
        grid=(n // tm,),
        in_specs=[
            pl.BlockSpec((tm, d), lambda i: (i, 0)),
            pl.BlockSpec((tm, half), lambda i: (i, col_a)),
            pl.BlockSpec((tm, half), lambda i: (i, col_b)),
            _resident(w_mix.shape),
            _resident((1, d)),
            _resident(wq.shape),
            pl.BlockSpec((None, mem_len, 2 * d), lambda i: (i // blocks, 0, 0)),
            _resident(wo.shape),
        ],
        out_specs=pl.BlockSpec((tm, d), lambda i: (i, 0)),
        out_shape=jax.ShapeDtypeStruct((n, d), F32),
        scratch_shapes=[pltpu.VMEM((tm, d), BF16)],
        compiler_params=_params(1),
        name="cross_attn",
    )(h, mix_a, mix_b, w_mix.astype(BF16), g.reshape(1, -1), (wq * dh ** -0.5).astype(BF16), mem_kv, wo.astype(BF16))


def _mlp_kernel(h_ref, g_ref, w1_ref, w2_ref, gf_ref, out_ref, *, tf, final_norm):
    x = h_ref[...]
    xn = _rms(x, g_ref[...]).astype(BF16)
    acc = x
    for c in range(w1_ref.shape[1] // tf):
        a = jnp.maximum(_dot(xn, w1_ref[:, c * tf:(c + 1) * tf]), 0.0)
        acc = acc + _dot((a * a).astype(BF16), w2_ref[c * tf:(c + 1) * tf, :])
    if final_norm:
        acc = _rms(acc, gf_ref[...])
    out_ref[...] = acc


def _mlp(h, g, w1, w2, g_final, *, tm, tf, final_norm):
    n, d = h.shape
    return pl.pallas_call(
        functools.partial(_mlp_kernel, tf=tf, final_norm=final_norm),
        grid=(n // tm,),
        in_specs=[
            pl.BlockSpec((tm, d), lambda i: (i, 0)),
            _resident((1, d)),
            _resident(w1.shape),
            _resident(w2.shape),
            _resident((1, d)),
        ],
        out_specs=pl.BlockSpec((tm, d), lambda i: (i, 0)),
        out_shape=jax.ShapeDtypeStruct((n, d), F32),
        compiler_params=_params(1),
        name="mlp",
    )(h, g.reshape(1, -1), w1.astype(BF16), w2.astype(BF16), g_final.reshape(1, -1))


def _rope_tables(seq):
    half = MLA_ROPE // 2
    inv = (ROPE_BASE ** (-np.arange(half, dtype=np.float32) / half)).astype(np.float32)
    ang = np.arange(seq, dtype=np.float32)[:, None] * inv[None, :]
    pad = lambda a: np.concatenate([a, a, np.zeros((seq, LANES - 2 * half), np.float32)], axis=1).astype(np.float32)
    return jnp.asarray(pad(np.cos(ang))), jnp.asarray(pad(np.sin(ang)))


def kernel(x, mem, ev_norm, ev_w_in, diff_lq1, diff_lk1, diff_lq2, diff_lk2, diff_subln, mla_g_cq, mla_w_uq, mla_g_ckv, mla_w_ukv, ev_w_out, od_norm, sb_w_in, sb_w_out, xa_norm, xa_mem_norm, xa_wq, xa_wkv, xa_wo, mlp_norm, mlp_w1, mlp_w2, final_norm):
    b, seq, d = x.shape
    mem_len = mem.shape[1]
    depth = xa_norm.shape[0]
    tm = ROW_TILE
    h = x.reshape(b * seq, d)
    mem2 = mem.reshape(b * mem_len, d)
    cos_pad, sin_pad = _rope_tables(seq)
    for i in range(depth):
        j = i // 2
        if i % 2 == 0:
            lambda_init = 0.8 - 0.6 * math.exp(-0.3 * i)
            dq, dk, dv, mq, mk, mv = _even_proj(
                h, ev_norm[j], ev_w_in[j], mla_g_cq[j], mla_w_uq[j], mla_g_ckv[j], mla_w_ukv[j],
                cos_pad, sin_pad, seq=seq, tm=WIDE_ROW_TILE, t=tm)
            split = lambda a: a.reshape(b, seq, a.shape[-1])
            o_a = _diff_attn(split(dq), split(dk), dv, diff_lq1[j], diff_lk1[j], diff_lq2[j], diff_lk2[j],
                             diff_subln[j], lambda_init, t=tm, heads=DIFF_HEADS_PER_STEP)
            o_b = _mla_attn(split(mq), split(mk), mv, t=tm, heads=MLA_HEADS_PER_STEP)
            mix_a, mix_b = (o_a.reshape(b * seq, -1), 0), (o_b.reshape(b * seq, -1), 0)
            w_mix = ev_w_out[j]
        else:
            qkv = _norm_matmul(h, od_norm[j].reshape(1, -1), sb_w_in[j].astype(BF16), tm=WIDE_ROW_TILE, name="sb_proj",
                               scaled_cols=SB_HEADS * SB_DH, scale=SB_DH ** -0.5)
            mix = _sb_attn(qkv.reshape(b, seq, -1), t=SB_BLOCK, pairs=SB_PAIRS_PER_STEP).reshape(b * seq, -1)
            mix_a, mix_b = (mix, 0), (mix, 1)
            w_mix = sb_w_out[j]
        mem_kv = _norm_matmul(mem2, xa_mem_norm[i].reshape(1, -1), xa_wkv[i].astype(BF16), tm=mem_len, name="mem_kv")
        h = _cross(h, mix_a, mix_b, w_mix, xa_norm[i], xa_wq[i], mem_kv.reshape(b, mem_len, -1), xa_wo[i], seq=seq, tm=WIDE_ROW_TILE)
        h = _mlp(h, mlp_norm[i], mlp_w1[i], mlp_w2[i], final_norm, tm=WIDE_ROW_TILE, tf=MLP_FF_TILE, final_norm=(i == depth - 1))
    return h.reshape(b, seq, d)
```
